```python
import jax, jax.numpy as jnp
from jax import lax
import numpy as np

D_MODEL = 1024
BATCH = 4
SEQ = 4096
DEPTH = 1
DEC_BATCH = 32
DEC_SEQ = 8
PAST_LEN = 16384
PAGE_SIZE = 128

N_HEADS = 16
HEAD_DIM = 64
N_KV = 4
GROUP = N_HEADS // N_KV
CMP_STRIDE = 16
CMP_LEN = 2 * CMP_STRIDE
SEL_BLOCK = 64
N_SEL = 16
WINDOW = 512
C_CONV = D_MODEL // 2
CONV_K = 31
FFN_CONV_K = 3
D_FF = 2816
ROPE_THETA = 10000.0
QBLK = 128
EPS = 1e-6
KV_W = 2 * N_KV * HEAD_DIM
IN_WIDTHS = (2 * C_CONV, N_HEADS * HEAD_DIM, KV_W, KV_W, KV_W, 3 * N_HEADS, 2 * D_MODEL)

kernel_name = 'hybrid_conformer_nsa_convffn_step'


def _rmsnorm(x, g):
    xf = x.astype(jnp.float32)
    y = xf * lax.rsqrt(jnp.mean(xf * xf, axis=-1, keepdims=True) + EPS)
    return (y * g.astype(jnp.float32)).astype(x.dtype)


def _layernorm(x, g, b):
    xf = x.astype(jnp.float32)
    mu = jnp.mean(xf, axis=-1, keepdims=True)
    var = jnp.mean(jnp.square(xf - mu), axis=-1, keepdims=True)
    y = (xf - mu) * lax.rsqrt(var + EPS) * g.astype(jnp.float32) + b.astype(jnp.float32)
    return y.astype(x.dtype)


def _rope(x, pos):
    half = HEAD_DIM // 2
    inv = ROPE_THETA ** (-jnp.arange(half, dtype=jnp.float32) * (2.0 / HEAD_DIM))
    ang = pos.astype(jnp.float32)[:, None] * inv[None, :]
    cos = jnp.cos(ang)[:, None, :]
    sin = jnp.sin(ang)[:, None, :]
    xf = x.astype(jnp.float32)
    x1, x2 = xf[..., :half], xf[..., half:]
    return jnp.concatenate([x1 * cos - x2 * sin, x2 * cos + x1 * sin], axis=-1).astype(x.dtype)


def _masked_softmax(s, mask):
    s = jnp.where(mask, s, -jnp.inf)
    m = jnp.max(s, axis=-1, keepdims=True)
    m = jnp.where(jnp.isfinite(m), m, 0.0)
    p = jnp.exp(s - m)
    return p / jnp.maximum(jnp.sum(p, axis=-1, keepdims=True), 1e-30)


def _causal_dwconv(prev, new, w, b):
    xc = jnp.concatenate([prev, new], axis=1)
    width, ch = w.shape
    y = lax.conv_general_dilated(xc, w[:, None, :], (1,), 'VALID',
                                 dimension_numbers=('NWC', 'WIO', 'NWC'),
                                 feature_group_count=ch)
    return y + b, xc[:, xc.shape[1] - (width - 1):]


def _compress(rows, w_c, pe):
    B, L = rows.shape[0], rows.shape[1]
    n_chunk = -(-L // CMP_STRIDE)
    rows = jnp.pad(rows, ((0, 0), (0, n_chunk * CMP_STRIDE - L), (0, 0), (0, 0)))
    chunks = rows.reshape(B, n_chunk, CMP_STRIDE, N_KV, HEAD_DIM)
    first = jnp.einsum('bnlgd,lde->bnge', chunks, w_c[:CMP_STRIDE])
    second = jnp.einsum('bnlgd,lde->bnge', chunks, w_c[CMP_STRIDE:])
    pos_term = jnp.einsum('ld,lde->e', pe, w_c)
    return first[:, :-1] + second[:, 1:] + pos_term


def _nsa_attend(q, gates, pos0, cmp_kv, slc_kv, win_kv, w0,
                w_cmp_k, w_cmp_v, pe_cmp_k, pe_cmp_v, g_k_cmp):
    B, T = q.shape[0], q.shape[1]
    L = cmp_kv.shape[1]
    scale = HEAD_DIM ** -0.5
    kc = _compress(cmp_kv[:, :, 0], w_cmp_k, pe_cmp_k)
    vc = _compress(cmp_kv[:, :, 1], w_cmp_v, pe_cmp_v)
    n_cmp = kc.shape[1]
    cmp_start = jnp.arange(n_cmp) * CMP_STRIDE
    cmp_last = cmp_start + (CMP_LEN - 1)
    kc = _rope(_rmsnorm(kc, g_k_cmp), cmp_last)
    n_slc = -(-L // SEL_BLOCK)
    slc_pad = jnp.pad(slc_kv, ((0, 0), (0, n_slc * SEL_BLOCK - L), (0, 0), (0, 0), (0, 0)))
    slc_blocks = slc_pad.reshape(B, n_slc, SEL_BLOCK, 2, N_KV, HEAD_DIM).transpose(0, 4, 1, 2, 3, 5)
    slc_start = jnp.arange(n_slc) * SEL_BLOCK
    overlap = ((cmp_start[:, None] < slc_start[None, :] + SEL_BLOCK)
               & (cmp_start[:, None] + CMP_LEN > slc_start[None, :])).astype(jnp.float32)
    n_top = min(N_SEL, n_slc)
    win_pad = jnp.pad(win_kv, ((0, 0), (WINDOW, 0), (0, 0), (0, 0), (0, 0)))
    qb = QBLK if T % QBLK == 0 else T
    nb = T // qb
    q_blocks = q.reshape(B, nb, qb, N_KV, GROUP, HEAD_DIM).transpose(1, 0, 2, 3, 4, 5)
    g_blocks = gates.reshape(B, nb, qb, N_KV, GROUP, 3).transpose(1, 0, 2, 3, 4, 5)
    b_idx = jnp.arange(B)[:, None, None, None]
    g_idx = jnp.arange(N_KV)[None, None, :, None]
    blk_off = jnp.arange(SEL_BLOCK)
    win_off = jnp.arange(WINDOW + qb)
    j_all = jnp.arange(n_slc)

    def block(args):
        qblk, gblk, c = args
        qs = pos0 + c * qb
        t = qs + jnp.arange(qb)
        s = jnp.einsum('bqgrd,bngd->bqgrn', qblk, kc).astype(jnp.float32) * scale
        p_c = _masked_softmax(s, (cmp_last[None, :] <= t[:, None])[None, :, None, None, :])
        o_c = jnp.einsum('bqgrn,bngd->bqgrd', p_c.astype(vc.dtype), vc)
        imp = jnp.einsum('bqgrn,nj->bqgj', p_c, overlap)
        cur = t // SEL_BLOCK
        forced = ((j_all[None, :] == 0) | (j_all[None, :] == cur[:, None])
                  | (j_all[None, :] == cur[:, None] - 1))
        valid = slc_start[None, :] <= t[:, None]
        imp = jnp.where(forced[None, :, None, :], jnp.inf,
                        jnp.where(valid[None, :, None, :], imp, -jnp.inf))
        _, idx = lax.top_k(imp, n_top)
        sel = slc_blocks[b_idx, g_idx, idx]
        sel = sel.reshape(B, qb, N_KV, n_top * SEL_BLOCK, 2, HEAD_DIM)
        kpos = (idx[..., None] * SEL_BLOCK + blk_off).reshape(B, qb, N_KV, 1, n_top * SEL_BLOCK)
        s = jnp.einsum('bqgrd,bqgmd->bqgrm', qblk, sel[..., 0, :]).astype(jnp.float32) * scale
        p_s = _masked_softmax(s, kpos <= t[None, :, None, None, None])
        o_s = jnp.einsum('bqgrm,bqgmd->bqgrd', p_s.astype(sel.dtype), sel[..., 1, :])
        wblk = lax.dynamic_slice_in_dim(win_pad, qs - w0, WINDOW + qb, axis=1)
        wpos = qs - WINDOW + win_off
        dist = t[:, None] - wpos[None, :]
        mask_w = (dist >= 0) & (dist <= WINDOW) & (wpos[None, :] >= 0)
        s = jnp.einsum('bqgrd,bkgd->bqgrk', qblk, wblk[:, :, 0]).astype(jnp.float32) * scale
        p_w = _masked_softmax(s, mask_w[None, :, None, None, :])
        o_w = jnp.einsum('bqgrk,bkgd->bqgrd', p_w.astype(wblk.dtype), wblk[:, :, 1])
        g = jax.nn.sigmoid(gblk.astype(jnp.float32))
        o = g[..., 0:1] * o_c + g[..., 1:2] * o_s + g[..., 2:3] * o_w
        return o.astype(q.dtype)

    out = lax.map(block, (q_blocks, g_blocks, jnp.arange(nb)))
    return out.transpose(1, 0, 2, 3, 4, 5).reshape(B, T, N_HEADS * HEAD_DIM)


def _trunk_layer(x, pos0, conv_prev, ffn_prev, past_cmp, past_slc, win_prev, params):
    (g_norm_mix, w_in, w_dw, b_dw, g_ln_conv, b_ln_conv, w_conv_out, g_q, g_k_cmp,
     g_k_slc, g_k_win, w_cmp_k, w_cmp_v, pe_cmp_k, pe_cmp_v, w_nsa_out, w_out,
     g_norm_ffn, w_up, w_ffn_dw, b_ffn_dw, w_down) = params
    B, T = x.shape[0], x.shape[1]
    pos = pos0 + jnp.arange(T)
    h = _rmsnorm(x, g_norm_mix)
    z = h @ w_in
    u, q, kvc, kvs, kvw, g_nsa, g_mix = jnp.split(z, np.cumsum(IN_WIDTHS)[:-1].tolist(), axis=-1)
    glu = u[..., :C_CONV] * jax.nn.sigmoid(u[..., C_CONV:])
    conv, conv_state = _causal_dwconv(conv_prev, glu, w_dw, b_dw)
    a_out = jax.nn.silu(_layernorm(conv, g_ln_conv, b_ln_conv)) @ w_conv_out
    q = _rope(_rmsnorm(q.reshape(B, T, N_HEADS, HEAD_DIM), g_q), pos)
    q = q.reshape(B, T, N_KV, GROUP, HEAD_DIM)
    cmp_new = kvc.reshape(B, T, 2, N_KV, HEAD_DIM)
    kvs = kvs.reshape(B, T, 2, N_KV, HEAD_DIM)
    slc_new = jnp.stack([_rope(_rmsnorm(kvs[:, :, 0], g_k_slc), pos), kvs[:, :, 1]], axis=2)
    kvw = kvw.reshape(B, T, 2, N_KV, HEAD_DIM)
    win_new = jnp.stack([_rope(_rmsnorm(kvw[:, :, 0], g_k_win), pos), kvw[:, :, 1]], axis=2)
    cmp_all = jnp.concatenate([past_cmp, cmp_new], axis=1)
    slc_all = jnp.concatenate([past_slc, slc_new], axis=1)
    win_all = jnp.concatenate([win_prev, win_new], axis=1)
    w0 = pos0 - win_prev.shape[1]
    gates = g_nsa.reshape(B, T, N_KV, GROUP, 3)
    o = _nsa_attend(q, gates, pos0, cmp_all, slc_all, win_all, w0,
                    w_cmp_k, w_cmp_v, pe_cmp_k, pe_cmp_v, g_k_cmp)
    b_out = o @ w_nsa_out
    g_a, g_b = jnp.split(g_mix, 2, axis=-1)
    x1 = x + (jax.nn.sigmoid(g_a) * a_out + jax.nn.sigmoid(g_b) * b_out) @ w_out
    up = _rmsnorm(x1, g_norm_ffn) @ w_up
    upc, ffn_state = _causal_dwconv(ffn_prev, up, w_ffn_dw, b_ffn_dw)
    gate, val = jnp.split(upc, 2, axis=-1)
    y = x1 + (jax.nn.silu(gate) * val) @ w_down
    win_state = win_all[:, win_all.shape[1] - min(WINDOW, pos0 + T):]
    return y, cmp_new, slc_new, win_state, conv_state, ffn_state


def setup_inputs(seed: int = 0) -> dict:
    key = jax.random.key(seed)
    ks = jax.random.split(key, 32)
    n_pages = PAST_LEN // PAGE_SIZE
    n_pool = (DEC_BATCH * n_pages * 5) // 4
    w_buf = min(WINDOW, PAST_LEN)
    f32 = jnp.float32

    def nrm(k, shape, scale):
        return jax.random.normal(k, shape, f32) * scale

    def gain(k, n):
        return 1.0 + nrm(k, (n,), 0.02)

    kv_shape = (n_pool, PAGE_SIZE, 2, N_KV, HEAD_DIM)
    page_table = jax.random.permutation(ks[7], n_pool)[:DEC_BATCH * n_pages]
    page_table = page_table.reshape(DEC_BATCH, n_pages).astype(jnp.int32)
    d_in = sum(IN_WIDTHS)
    return {
        'x_prompt': nrm(ks[0], (BATCH, SEQ, D_MODEL), 1.0),
        'x_sample': nrm(ks[1], (DEC_BATCH, DEC_SEQ, D_MODEL), 1.0),
        'cache_cmp_kv': nrm(ks[2], kv_shape, 1.0),
        'cache_slc_kv': nrm(ks[3], kv_shape, 1.0),
        'cache_win_kv': nrm(ks[4], (DEC_BATCH, w_buf, 2, N_KV, HEAD_DIM), 1.0),
        'state_conv': nrm(ks[5], (DEC_BATCH, CONV_K - 1, C_CONV), 0.5),
        'state_ffn_conv': nrm(ks[6], (DEC_BATCH, FFN_CONV_K - 1, 2 * D_FF), 0.5),
        'page_table': page_table,
        'g_norm_mix': gain(ks[8], D_MODEL),
        'w_in': nrm(ks[9], (D_MODEL, d_in), D_MODEL ** -0.5),
        'w_dw': nrm(ks[10], (CONV_K, C_CONV), CONV_K ** -0.5),
        'b_dw': nrm(ks[11], (C_CONV,), 0.02),
        'g_ln_conv': gain(ks[12], C_CONV),
        'b_ln_conv': nrm(ks[13], (C_CONV,), 0.02),
        'w_conv_out': nrm(ks[14], (C_CONV, D_MODEL), C_CONV ** -0.5),
        'g_q': gain(ks[15], HEAD_DIM),
        'g_k_cmp': gain(ks[16], HEAD_DIM),
        'g_k_slc': gain(ks[17], HEAD_DIM),
        'g_k_win': gain(ks[18], HEAD_DIM),
        'w_cmp_k': nrm(ks[19], (CMP_LEN, HEAD_DIM, HEAD_DIM), (CMP_LEN * HEAD_DIM) ** -0.5),
        'w_cmp_v': nrm(ks[20], (CMP_LEN, HEAD_DIM, HEAD_DIM), (CMP_LEN * HEAD_DIM) ** -0.5),
        'pe_cmp_k': nrm(ks[21], (CMP_LEN, HEAD_DIM), 0.1),
        'pe_cmp_v': nrm(ks[22], (CMP_LEN, HEAD_DIM), 0.1),
        'w_nsa_out': nrm(ks[23], (N_HEADS * HEAD_DIM, D_MODEL), (N_HEADS * HEAD_DIM) ** -0.5),
        'w_out': nrm(ks[24], (D_MODEL, D_MODEL), D_MODEL ** -0.5),
        'g_norm_ffn': gain(ks[25], D_MODEL),
        'w_up': nrm(ks[26], (D_MODEL, 2 * D_FF), D_MODEL ** -0.5),
        'w_ffn_dw': nrm(ks[27], (FFN_CONV_K, 2 * D_FF), FFN_CONV_K ** -0.5),
        'b_ffn_dw': nrm(ks[28], (2 * D_FF,), 0.02),
        'w_down': nrm(ks[29], (D_FF, D_MODEL), D_FF ** -0.5),
    }


def reference(x_prompt, x_sample, cache_cmp_kv, cache_slc_kv, cache_win_kv, state_conv,
              state_ffn_conv, page_table, g_norm_mix, w_in, w_dw, b_dw, g_ln_conv, b_ln_conv,
              w_conv_out, g_q, g_k_cmp, g_k_slc, g_k_win, w_cmp_k, w_cmp_v, pe_cmp_k, pe_cmp_v,
              w_nsa_out, w_out, g_norm_ffn, w_up, w_ffn_dw, b_ffn_dw, w_down):
    params = (g_norm_mix, w_in, w_dw, b_dw, g_ln_conv, b_ln_conv, w_conv_out, g_q, g_k_cmp,
              g_k_slc, g_k_win, w_cmp_k, w_cmp_v, pe_cmp_k, pe_cmp_v, w_nsa_out, w_out,
              g_norm_ffn, w_up, w_ffn_dw, b_ffn_dw, w_down)
    bp = x_prompt.shape[0]
    dt = x_prompt.dtype
    empty_kv = jnp.zeros((bp, 0, 2, N_KV, HEAD_DIM), dt)
    db, n_pages = page_table.shape
    past_len = n_pages * cache_cmp_kv.shape[1]
    past_cmp = cache_cmp_kv[page_table].reshape(db, past_len, 2, N_KV, HEAD_DIM)
    past_slc = cache_slc_kv[page_table].reshape(db, past_len, 2, N_KV, HEAD_DIM)
    h_p, h_s = x_prompt, x_sample
    for _ in range(DEPTH):
        h_p, cmp_p, slc_p, win_p, conv_p, ffn_p = _trunk_layer(
            h_p, 0, jnp.zeros((bp, CONV_K - 1, C_CONV), dt),
            jnp.zeros((bp, FFN_CONV_K - 1, 2 * D_FF), dt), empty_kv, empty_kv, empty_kv, params)
        h_s, cmp_s, slc_s, win_s, conv_s, ffn_s = _trunk_layer(
            h_s, past_len, state_conv, state_ffn_conv, past_cmp, past_slc, cache_win_kv, params)
    return (h_p, h_s, cmp_p, slc_p, win_p, conv_p, ffn_p, cmp_s, slc_s, win_s, conv_s, ffn_s)
```

```python
import functools

import numpy as np
import jax
import jax.numpy as jnp
from jax import lax
from jax.experimental import pallas as pl
from jax.experimental.pallas import tpu as pltpu

F32 = jnp.float32
BF16 = jnp.bfloat16

D_MODEL = 1024
N_HEADS = 16
HEAD_DIM = 64
N_KV = 4
GROUP = N_HEADS // N_KV
CMP_STRIDE = 16
CMP_LEN = 2 * CMP_STRIDE
SEL_BLOCK = 64
N_SEL = 16
WINDOW = 512
C_CONV = D_MODEL // 2
CONV_K = 31
FFN_CONV_K = 3
D_FF = 2816
ROPE_THETA = 10000.0
EPS = 1e-6
KV_W = 2 * N_KV * HEAD_DIM
K_W = N_KV * HEAD_DIM

LANES = 128
SUBLANES = 8
QBLK = 256
CONV_HALO = 32
VMEM_LIMIT = 56 * 1024 * 1024
FFN_CW = 256
FFN_NC = D_FF // FFN_CW
CMP_PAGES = 16
SLC_PAGES = 8
QROWS = GROUP * N_KV * 8

NEG = -1e30
PEN = -1e9
RANK_FORCED = 1e30
RANK_INVALID = -1e30
RANK_REMOVED = -2e30


def _dot(a, b):
    return jnp.dot(a, b, preferred_element_type=F32)


def _dot_nt(a, b):
    return lax.dot_general(a, b, (((1,), (1,)), ((), ())), preferred_element_type=F32)


def _sigmoid(x):
    return 1.0 / (1.0 + jnp.exp(-x))


def _iota(shape, dim):
    return lax.broadcasted_iota(jnp.int32, shape, dim)


def _params(sem):
    return pltpu.CompilerParams(dimension_semantics=sem, vmem_limit_bytes=VMEM_LIMIT)


def _const_spec(a, ngrid):
    nd = a.ndim
    if ngrid == 1:
        return pl.BlockSpec(a.shape, lambda i: (0,) * nd)
    if ngrid == 2:
        return pl.BlockSpec(a.shape, lambda i, j: (0,) * nd)
    return pl.BlockSpec(a.shape, lambda i, j, k: (0,) * nd)


def _rope_tables(pos):
    half = HEAD_DIM // 2
    inv = ROPE_THETA ** (-jnp.arange(half, dtype=F32) * (2.0 / HEAD_DIM))
    ang = pos.astype(F32)[:, None] * inv[None, :]
    cos, sin = jnp.cos(ang), jnp.sin(ang)
    return jnp.tile(cos, (1, 4)), jnp.tile(jnp.concatenate([-sin, sin], axis=1), (1, 2))


def _head_sum_mats(width):
    heads = width // HEAD_DIM
    lane_head = np.arange(width) // HEAD_DIM
    bd = (lane_head[:, None] == np.arange(LANES)[None, :]).astype(np.float32)
    ex = np.zeros((2 * LANES, width), np.float32)
    ex[:heads] = (np.arange(heads)[:, None] == lane_head[None, :])
    ex[LANES:LANES + heads] = ex[:heads]
    return jnp.asarray(bd, BF16), jnp.asarray(ex, BF16)


def _overlap_t(n_slc_rows, n_cols, col_shift, n_cmp):
    n = np.arange(n_cols) - col_shift
    j = np.arange(n_slc_rows)
    cs = n * CMP_STRIDE
    ss = j * SEL_BLOCK
    ov = (cs[None, :] < ss[:, None] + SEL_BLOCK) & (cs[None, :] + CMP_LEN > ss[:, None])
    ov &= (n[None, :] >= 0) & (n[None, :] < n_cmp)
    return ov.astype(np.float32)


def _head_rmsnorm(x, gain, bd, ex):
    ssum = _dot((x * x).astype(BF16), bd)
    r = lax.rsqrt(ssum * (1.0 / HEAD_DIM) + EPS)
    r_hi = r.astype(BF16)
    r_lo = (r - r_hi.astype(F32)).astype(BF16)
    rfull = _dot(jnp.concatenate([r_hi, r_lo], axis=1), ex)
    return x * rfull * gain


def _rope128(y, cos, sin):
    lane = _iota(y.shape, 1)
    first = (lane & (HEAD_DIM - 1)) < (HEAD_DIM // 2)
    rot = jnp.where(first, pltpu.roll(y, LANES - HEAD_DIM // 2, 1), pltpu.roll(y, HEAD_DIM // 2, 1))
    return y * cos + rot * sin


def _dup_heads(chunk):
    lo = _iota(chunk.shape, 1) < HEAD_DIM
    swapped = pltpu.roll(chunk, HEAD_DIM, 1)
    return jnp.where(lo, chunk, swapped), jnp.where(lo, swapped, chunk)


def _rank_select(val, n_rounds):
    rows = val.shape[0]
    j = _iota(val.shape, 0).astype(F32)
    sel = jnp.zeros(val.shape, F32)
    for _ in range(n_rounds):
        mx = jnp.max(val, axis=0, keepdims=True)
        idx = jnp.min(jnp.where(val == mx, j, float(rows)), axis=0, keepdims=True)
        hit = j == idx
        sel = jnp.where(hit, 1.0, sel)
        val = jnp.where(hit, RANK_REMOVED, val)
    return sel > 0.5


def _inproj_kernel(x_ref, gmix_ref, wu_ref, wq_ref, wkv_ref, wgn_ref, wgm_ref, gq_ref, gks_ref,
                   gkw_ref, bd16_ref, ex16_ref, bd4_ref, ex4_ref, cos_ref, sin_ref,
                   glu_ref, q_ref, cmp_ref, slc_ref, win_ref, gn_ref, gm_ref, *attn_refs):
    x = x_ref[...]
    ms = jnp.mean(x * x, axis=-1, keepdims=True)
    h = (x * lax.rsqrt(ms + EPS) * gmix_ref[...]).astype(BF16)
    cos = cos_ref[...]
    sin = sin_ref[...]

    u = _dot(h, wu_ref[...])
    glu_ref[...] = u[:, :C_CONV] * _sigmoid(u[:, C_CONV:])

    yq = _head_rmsnorm(_dot(h, wq_ref[...]), gq_ref[...], bd16_ref[...], ex16_ref[...])
    scale = HEAD_DIM ** -0.5
    for c in range(D_MODEL // LANES):
        sl = slice(c * LANES, (c + 1) * LANES)
        q_ref[:, sl] = (_rope128(yq[:, sl], cos, sin) * scale).astype(BF16)

    zkv = _dot(h, wkv_ref[...])
    cmp_ref[...] = zkv[:, :KV_W]

    def kv_branch(z, gain_ref, rows_ref, k2_ref, vt_ref):
        yk = _head_rmsnorm(z[:, :K_W], gain_ref[...], bd4_ref[...], ex4_ref[...])
        v = z[:, K_W:]
        rows_ref[:, K_W:] = v
        for c in range(K_W // LANES):
            sl = slice(c * LANES, (c + 1) * LANES)
            kr = _rope128(yk[:, sl], cos, sin)
            rows_ref[:, sl] = kr
            if k2_ref is not None:
                a2, b2 = _dup_heads(kr)
                k2_ref[2 * c] = a2.astype(BF16)
                k2_ref[2 * c + 1] = b2.astype(BF16)
        if vt_ref is not None:
            vt_ref[...] = v.T.astype(BF16)

    if attn_refs:
        k2s_ref, vts_ref, k2w_ref, vtw_ref = attn_refs
    else:
        k2s_ref = vts_ref = k2w_ref = vtw_ref = None
    kv_branch(zkv[:, KV_W:2 * KV_W], gks_ref, slc_ref, k2s_ref, vts_ref)
    kv_branch(zkv[:, 2 * KV_W:], gkw_ref, win_ref, k2w_ref, vtw_ref)

    gn_ref[...] = _sigmoid(_dot(h, wgn_ref[...]))
    gm_ref[...] = _sigmoid(_dot(h, wgm_ref[...]))


def _inproj(x, cos, sin, wts, attn_layouts):
    B, T, _ = x.shape
    tm = QBLK
    assert T % tm == 0
    nt = T // tm
    tok = lambda w: pl.BlockSpec((None, tm, w), lambda b, t: (b, t, 0))
    tab = pl.BlockSpec((tm, LANES), lambda b, t: (t, 0))
    in_specs = [tok(D_MODEL)] + [_const_spec(a, 2) for a in wts] + [tab, tab]
    out_shape = [
        jax.ShapeDtypeStruct((B, T, C_CONV), F32),
        jax.ShapeDtypeStruct((B, T, D_MODEL), BF16),
        jax.ShapeDtypeStruct((B, T, KV_W), F32),
        jax.ShapeDtypeStruct((B, T, KV_W), F32),
        jax.ShapeDtypeStruct((B, T, KV_W), F32),
        jax.ShapeDtypeStruct((B, T, LANES), F32),
        jax.ShapeDtypeStruct((B, T, 2 * D_MODEL), F32),
    ]
    out_specs = [tok(C_CONV), tok(D_MODEL), tok(KV_W), tok(KV_W), tok(KV_W), tok(LANES), tok(2 * D_MODEL)]
    if attn_layouts:
        k2 = jax.ShapeDtypeStruct((B, N_KV, T, LANES), BF16)
        vt = jax.ShapeDtypeStruct((B, nt, K_W, tm), BF16)
        k2_spec = pl.BlockSpec((None, N_KV, tm, LANES), lambda b, t: (b, 0, t, 0))
        vt_spec = pl.BlockSpec((None, None, K_W, tm), lambda b, t: (b, t, 0, 0))
        out_shape += [k2, vt, k2, vt]
        out_specs += [k2_spec, vt_spec, k2_spec, vt_spec]
    return pl.pallas_call(
        _inproj_kernel,
        grid=(B, nt),
        in_specs=in_specs,
        out_specs=out_specs,
        out_shape=out_shape,
        compiler_params=_params(("parallel", "parallel")),
        name="inproj",
    )(x, *wts, cos, sin)


def _conv_kernel(glu_ref, halo_ref, st_ref, wdw_ref, bdw_ref, gln_ref, bln_ref, wco_ref,
                 a_ref, cst_ref, xc_ref, s_ref, *, tt, rb):
    ti = pl.program_id(1)

    @pl.when(ti == 0)
    def _():
        xc_ref[0:CONV_HALO, :] = st_ref[...]

    @pl.when(ti > 0)
    def _():
        xc_ref[0:CONV_HALO, :] = halo_ref[...]

    xc_ref[CONV_HALO:CONV_HALO + tt, :] = glu_ref[...]
    first = CONV_HALO - (CONV_K - 1)
    for r0 in range(0, tt, rb):
        acc = jnp.broadcast_to(bdw_ref[...], (rb, C_CONV))
        for k in range(CONV_K):
            acc = acc + wdw_ref[k:k + 1, :] * xc_ref[r0 + first + k:r0 + first + k + rb, :]
        mu = jnp.mean(acc, axis=-1, keepdims=True)
        d = acc - mu
        var = jnp.mean(d * d, axis=-1, keepdims=True)
        y = d * lax.rsqrt(var + EPS) * gln_ref[...] + bln_ref[...]
        s_ref[r0:r0 + rb, :] = y * _sigmoid(y)
    a_ref[...] = _dot(s_ref[...].astype(BF16), wco_ref[...])
    cst_ref[...] = xc_ref[tt:tt + CONV_HALO, :]


def _conv_module(glu, halo_src, state32, wdw, bdw, gln, bln, wco, tt, rb):
    B, T, _ = glu.shape
    nt = T // tt
    hb = max(tt // CONV_HALO, 1)
    return pl.pallas_call(
        functools.partial(_conv_kernel, tt=tt, rb=rb),
        grid=(B, nt),
        in_specs=[
            pl.BlockSpec((None, tt, C_CONV), lambda b, t: (b, t, 0)),
            pl.BlockSpec((None, CONV_HALO, C_CONV), lambda b, t: (b, jnp.maximum(t * hb - 1, 0), 0)),
            pl.BlockSpec((None, CONV_HALO, C_CONV), lambda b, t: (b, 0, 0)),
            _const_spec(wdw, 2), _const_spec(bdw, 2), _const_spec(gln, 2), _const_spec(bln, 2),
            _const_spec(wco, 2),
        ],
        out_specs=[
            pl.BlockSpec((None, tt, D_MODEL), lambda b, t: (b, t, 0)),
            pl.BlockSpec((None, CONV_HALO, C_CONV), lambda b, t: (b, 0, 0)),
        ],
        out_shape=[
            jax.ShapeDtypeStruct((B, T, D_MODEL), F32),
            jax.ShapeDtypeStruct((B, CONV_HALO, C_CONV), F32),
        ],
        scratch_shapes=[pltpu.VMEM((CONV_HALO + tt, C_CONV), F32), pltpu.VMEM((tt, C_CONV), F32)],
        compiler_params=_params(("parallel", "arbitrary")),
        name="conv_module",
    )(glu, halo_src, state32, wdw, bdw, gln, bln, wco)


def _merge_kernel(o_ref, a_ref, gm_ref, x_ref, wno_ref, wout_ref, x1_ref):
    b_out = _dot(o_ref[...], wno_ref[...])
    gm = gm_ref[...]
    m = gm[:, :D_MODEL] * a_ref[...] + gm[:, D_MODEL:] * b_out
    x1_ref[...] = x_ref[...] + _dot(m.astype(BF16), wout_ref[...])


def _merge(o, a_out, gm, x, wno, wout, tm):
    N = x.shape[0]
    row = lambda w: pl.BlockSpec((tm, w), lambda i: (i, 0))
    return pl.pallas_call(
        _merge_kernel,
        grid=(N // tm,),
        in_specs=[row(D_MODEL), row(D_MODEL), row(2 * D_MODEL), row(D_MODEL),
                  _const_spec(wno, 1), _const_spec(wout, 1)],
        out_specs=row(D_MODEL),
        out_shape=jax.ShapeDtypeStruct((N, D_MODEL), F32),
        compiler_params=_params(("parallel",)),
        name="merge",
    )(o, a_out, gm, x, wno, wout)


def _ffn_kernel(x1_ref, gn_ref, wup_ref, wdw_ref, wdn_ref, fix1_ref, fix2_ref,
                y_ref, st_ref, prev_ref, acc_ref, *, tm, seg):
    streaming = seg == tm
    x1 = x1_ref[...]
    ms = jnp.mean(x1 * x1, axis=-1, keepdims=True)
    h = (x1 * lax.rsqrt(ms + EPS) * gn_ref[...]).astype(BF16)
    row = _iota((tm, FFN_CW), 0) & (seg - 1)

    if streaming:
        @pl.when(pl.program_id(1) == 0)
        def _():
            prev_ref[...] = fix2_ref[...]

    def conv3(c):
        cs = slice(c * FFN_CW, (c + 1) * FFN_CW)
        up = _dot(h, wup_ref[c])
        r1 = pltpu.roll(up, 1, 0)
        r2 = pltpu.roll(up, 2, 0)
        if streaming:
            p0 = prev_ref[0:1, cs]
            p1 = prev_ref[1:2, cs]
            s1 = jnp.where(row == 0, p1, r1)
            s2 = jnp.where(row == 0, p0, jnp.where(row == 1, p1, r2))
            prev_ref[0:2, cs] = up[tm - 2:tm, :]
        else:
            s1 = jnp.where(row == 0, fix1_ref[:, cs], r1)
            s2 = jnp.where(row < 2, fix2_ref[:, cs], r2)
            st_ref[:, cs] = up
        w = wdw_ref[c]
        return w[0:1] * s2 + w[1:2] * s1 + w[2:3] * up + w[3:4]

    acc_ref[...] = x1
    for c in range(FFN_NC):
        gate = conv3(c)
        val = conv3(FFN_NC + c)
        act = gate * _sigmoid(gate) * val
        acc_ref[...] += _dot(act.astype(BF16), wdn_ref[c])
    y_ref[...] = acc_ref[...]
    if streaming:
        st_ref[...] = prev_ref[...]


def _ffn(x1, gn, wup, wdw, wdn, fix1, fix2, tm, seg):
    B, T, _ = x1.shape
    nt = T // tm
    streaming = seg == tm
    tok = lambda w: pl.BlockSpec((None, tm, w), lambda b, t: (b, t, 0))
    if streaming:
        fix_specs = [pl.BlockSpec((None, SUBLANES, 2 * D_FF), lambda b, t: (b, 0, 0))] * 2
        st_shape = jax.ShapeDtypeStruct((B, SUBLANES, 2 * D_FF), F32)
        st_spec = pl.BlockSpec((None, SUBLANES, 2 * D_FF), lambda b, t: (b, 0, 0))
    else:
        fix_specs = [tok(2 * D_FF)] * 2
        st_shape = jax.ShapeDtypeStruct((B, T, 2 * D_FF), F32)
        st_spec = tok(2 * D_FF)
    return pl.pallas_call(
        functools.partial(_ffn_kernel, tm=tm, seg=seg),
        grid=(B, nt),
        in_specs=[tok(D_MODEL), _const_spec(gn, 2), _const_spec(wup, 2), _const_spec(wdw, 2),
                  _const_spec(wdn, 2)] + fix_specs,
        out_specs=[tok(D_MODEL), st_spec],
        out_shape=[jax.ShapeDtypeStruct((B, T, D_MODEL), F32), st_shape],
        scratch_shapes=[pltpu.VMEM((SUBLANES, 2 * D_FF), F32), pltpu.VMEM((tm, D_MODEL), F32)],
        compiler_params=_params(("parallel", "arbitrary")),
        name="conv_ffn",
    )(x1, gn, wup, wdw, wdn, fix1, fix2)


def _cmp_prompt_kernel(rows_ref, bdk_ref, bdv_ref, pek_ref, wk4_ref, pev_ref, wv4_ref, gk_ref,
                       bd4_ref, ex4_ref, cos_ref, sin_ref, kc2_ref, vct_ref, *, nch):
    acc_k = jnp.zeros((nch, 2 * K_W), F32)
    acc_v = jnp.zeros((nch, 2 * K_W), F32)
    for l in range(CMP_STRIDE):
        acc_k += _dot(rows_ref[:, l * KV_W:l * KV_W + K_W].astype(BF16), bdk_ref[l])
        acc_v += _dot(rows_ref[:, l * KV_W + K_W:(l + 1) * KV_W].astype(BF16), bdv_ref[l])
    pos_k = _dot(pek_ref[...], wk4_ref[...])[0:1]
    pos_v = _dot(pev_ref[...], wv4_ref[...])[0:1]
    kc = acc_k[:, :K_W] + pltpu.roll(acc_k[:, K_W:], nch - 1, 0) + pos_k
    vc = acc_v[:, :K_W] + pltpu.roll(acc_v[:, K_W:], nch - 1, 0) + pos_v
    yk = _head_rmsnorm(kc, gk_ref[...], bd4_ref[...], ex4_ref[...])
    for c in range(K_W // LANES):
        sl = slice(c * LANES, (c + 1) * LANES)
        a2, b2 = _dup_heads(_rope128(yk[:, sl], cos_ref[...], sin_ref[...]))
        kc2_ref[2 * c] = a2.astype(BF16)
        kc2_ref[2 * c + 1] = b2.astype(BF16)
    vct_ref[...] = vc.T.astype(BF16)


def _cmp_prompt(rows, cw, cos, sin):
    B, nch, _ = rows.shape
    return pl.pallas_call(
        functools.partial(_cmp_prompt_kernel, nch=nch),
        grid=(B,),
        in_specs=[pl.BlockSpec((None, nch, CMP_STRIDE * KV_W), lambda b: (b, 0, 0))] + [_const_spec(a, 1) for a in cw]
        + [_const_spec(cos, 1), _const_spec(sin, 1)],
        out_specs=[pl.BlockSpec((None, N_KV, nch, LANES), lambda b: (b, 0, 0, 0)),
                   pl.BlockSpec((None, K_W, nch), lambda b: (b, 0, 0))],
        out_shape=[jax.ShapeDtypeStruct((B, N_KV, nch, LANES), BF16),
                   jax.ShapeDtypeStruct((B, K_W, nch), BF16)],
        compiler_params=_params(("parallel",)),
        name="cmp_prompt",
    )(rows, *cw, cos, sin)


def _attn_prompt_kernel(q_ref, gates_ref, kc2_ref, vct_ref, k2s_ref, vts_ref, k2w_ref, vtw_ref,
                        e_ref, ovt_ref, o_ref, m_ref, l_ref, acc_ref, gt_ref, *, n_cmp, n_top):
    g = pl.program_id(1)
    qi = pl.program_id(2)
    tq = qi * QBLK + _iota((1, QBLK), 1)
    lo = _iota((QBLK, LANES), 1) < HEAD_DIM
    zero = jnp.zeros((QBLK, LANES), BF16)
    qm = []
    for p in range(GROUP // 2):
        qp = q_ref[:, p * LANES:(p + 1) * LANES]
        qm += [jnp.where(lo, qp, zero), jnp.where(lo, zero, qp)]

    nch = kc2_ref.shape[0]
    n = _iota((nch, QBLK), 0)
    cmask = jnp.logical_and(n * CMP_STRIDE + (CMP_LEN - 1) <= tq, n < n_cmp)
    kc2 = kc2_ref[...]
    vct = vct_ref[...]
    probs, o_cmp = [], []
    for r in range(GROUP):
        s = jnp.where(cmask, _dot_nt(kc2, qm[r]), NEG)
        mx = jnp.max(s, axis=0, keepdims=True)
        e = jnp.where(cmask, jnp.exp(s - mx), 0.0)
        p = e / jnp.maximum(jnp.sum(e, axis=0, keepdims=True), 1e-30)
        pb = p.astype(BF16)
        probs.append(pb)
        o_cmp.append(_dot(vct, pb))
    imp = _dot(ovt_ref[...], jnp.concatenate(probs, axis=0))
    ns = imp.shape[0]
    j = _iota((ns, QBLK), 0)
    cur = tq >> 6
    forced = jnp.logical_or(j == 0, jnp.logical_or(j == cur, j == cur - 1))
    rank = jnp.where(forced, RANK_FORCED, jnp.where(j * SEL_BLOCK <= tq, imp, RANK_INVALID))
    sel = _rank_select(rank, n_top)
    pen_t = jnp.where(sel, 0.0, PEN)
    if ns < LANES:
        pen_t = jnp.concatenate([pen_t, jnp.zeros((LANES - ns, QBLK), F32)], axis=0)
    pen = pen_t.T.astype(BF16)
    qa = [jnp.concatenate([x, pen], axis=1) for x in qm]

    kl = _iota((QBLK, QBLK), 0)
    tl = _iota((QBLK, QBLK), 1)

    def reset():
        m_ref[...] = jnp.full(m_ref.shape, NEG, F32)
        l_ref[...] = jnp.zeros(l_ref.shape, F32)
        acc_ref[...] = jnp.zeros(acc_ref.shape, F32)

    def attend(kblk, vt, qlist, mode):
        for r in range(GROUP):
            s = _dot_nt(kblk, qlist[r])
            if mode == "causal":
                s = jnp.where(kl <= tl, s, NEG)
            elif mode == "band":
                s = jnp.where(kl >= tl, s, NEG)
            m_old = m_ref[r]
            m_new = jnp.maximum(m_old, jnp.max(s, axis=0, keepdims=True))
            alpha = jnp.exp(m_old - m_new)
            p = jnp.exp(s - m_new)
            l_ref[r] = alpha * l_ref[r] + jnp.sum(p, axis=0, keepdims=True)
            acc_ref[r] = alpha * acc_ref[r] + _dot(vt, p.astype(BF16))
            m_ref[r] = m_new

    def finish():
        return [acc_ref[r] / l_ref[r] for r in range(GROUP)]

    def slc_block(kb):
        off = pl.multiple_of(kb * QBLK, QBLK)
        return jnp.concatenate([k2s_ref[pl.ds(off, QBLK), :], e_ref[pl.ds(off, QBLK), :]], axis=1)

    reset()

    def slc_body(kb, carry):
        attend(slc_block(kb), vts_ref[kb], qa, "none")
        return carry

    lax.fori_loop(0, qi, slc_body, 0)
    attend(slc_block(qi), vts_ref[qi], qa, "causal")
    o_slc = finish()

    reset()

    def win_block(kb):
        return k2w_ref[pl.ds(pl.multiple_of(kb * QBLK, QBLK), QBLK), :]

    @pl.when(qi >= 2)
    def _():
        attend(win_block(qi - 2), vtw_ref[qi - 2], qm, "band")

    @pl.when(qi >= 1)
    def _():
        attend(win_block(qi - 1), vtw_ref[qi - 1], qm, "none")

    attend(win_block(qi), vtw_ref[qi], qm, "causal")
    o_win = finish()

    gt_ref[...] = gates_ref[...].T
    outs = []
    for r in range(GROUP):
        base = (g * GROUP + r) * 3
        outs.append(gt_ref[pl.ds(base, 1), :] * o_cmp[r] + gt_ref[pl.ds(base + 1, 1), :] * o_slc[r]
                    + gt_ref[pl.ds(base + 2, 1), :] * o_win[r])
    o_ref[...] = jnp.concatenate(outs, axis=0).T.astype(BF16)


def _attn_prompt(q, gates, kc2, vct, k2s, vts, k2w, vtw, e_tab, ovt, n_cmp, n_top):
    B, T, _ = q.shape
    nq = T // QBLK
    nch = kc2.shape[2]
    return pl.pallas_call(
        functools.partial(_attn_prompt_kernel, n_cmp=n_cmp, n_top=n_top),
        grid=(B, N_KV, nq),
        in_specs=[
            pl.BlockSpec((None, QBLK, K_W), lambda b, g, i: (b, i, g)),
            pl.BlockSpec((None, QBLK, LANES), lambda b, g, i: (b, i, 0)),
            pl.BlockSpec((None, None, nch, LANES), lambda b, g, i: (b, g, 0, 0)),
            pl.BlockSpec((None, HEAD_DIM, nch), lambda b, g, i: (b, g, 0)),
            pl.BlockSpec((None, None, T, LANES), lambda b, g, i: (b, g, 0, 0)),
            pl.BlockSpec((None, nq, HEAD_DIM, QBLK), lambda b, g, i: (b, 0, g, 0)),
            pl.BlockSpec((None, None, T, LANES), lambda b, g, i: (b, g, 0, 0)),
            pl.BlockSpec((None, nq, HEAD_DIM, QBLK), lambda b, g, i: (b, 0, g, 0)),
            _const_spec(e_tab, 3), _const_spec(ovt, 3),
        ],
        out_specs=pl.BlockSpec((None, QBLK, K_W), lambda b, g, i: (b, i, g)),
        out_shape=jax.ShapeDtypeStruct((B, T, D_MODEL), BF16),
        scratch_shapes=[pltpu.VMEM((GROUP, 1, QBLK), F32), pltpu.VMEM((GROUP, 1, QBLK), F32),
                        pltpu.VMEM((GROUP, HEAD_DIM, QBLK), F32), pltpu.VMEM((LANES, QBLK), F32)],
        compiler_params=_params(("parallel", "parallel", "arbitrary")),
        name="attn_prompt",
    )(q, gates, kc2, vct, k2s, vts, k2w, vtw, e_tab, ovt)


def _cmp_stream_kernel(pt_ref, *refs, m):
    pages = refs[:CMP_PAGES]
    bdk_ref, bdv_ref, kc_ref, vc_ref, fs_ref = refs[CMP_PAGES:]
    s = pl.program_id(1)
    per_page = m // CMP_PAGES

    @pl.when(s == 0)
    def _():
        fs_ref[0:SUBLANES, :] = jnp.zeros((SUBLANES, 2 * K_W), F32)

    acc_k = jnp.zeros((m, 2 * K_W), F32)
    acc_v = jnp.zeros((m, 2 * K_W), F32)
    for l in range(CMP_STRIDE):
        x = jnp.concatenate([pg[:, l * KV_W:(l + 1) * KV_W] for pg in pages], axis=0)
        acc_k += _dot(x[:, :K_W].astype(BF16), bdk_ref[l])
        acc_v += _dot(x[:, K_W:].astype(BF16), bdv_ref[l])
    fs_ref[SUBLANES:SUBLANES + m, 0:K_W] = acc_k[:, :K_W]
    fs_ref[SUBLANES:SUBLANES + m, K_W:] = acc_v[:, :K_W]
    shifted = fs_ref[SUBLANES - 1:SUBLANES - 1 + m, :]
    kc_ref[...] = shifted[:, :K_W] + acc_k[:, K_W:]
    vc_ref[...] = shifted[:, K_W:] + acc_v[:, K_W:]
    fs_ref[SUBLANES - 1:SUBLANES, :] = fs_ref[SUBLANES - 1 + m:SUBLANES + m, :]


def _cmp_stream(page_table, cache3, bdk, bdv):
    db, n_pages = page_table.shape
    per_page = cache3.shape[1]
    m = CMP_PAGES * per_page
    nch = n_pages * per_page
    nsteps = n_pages // CMP_PAGES

    def page_spec(k):
        return pl.BlockSpec((None, per_page, CMP_STRIDE * KV_W),
                            lambda b, s, pt: (pt[b, s * CMP_PAGES + k], 0, 0))

    grid_spec = pltpu.PrefetchScalarGridSpec(
        num_scalar_prefetch=1,
        grid=(db, nsteps),
        in_specs=[page_spec(k) for k in range(CMP_PAGES)]
        + [pl.BlockSpec(bdk.shape, lambda b, s, pt: (0, 0, 0)), pl.BlockSpec(bdv.shape, lambda b, s, pt: (0, 0, 0))],
        out_specs=[pl.BlockSpec((None, m, K_W), lambda b, s, pt: (b, s, 0))] * 2,
        scratch_shapes=[pltpu.VMEM((SUBLANES + m, 2 * K_W), F32)],
    )
    return pl.pallas_call(
        functools.partial(_cmp_stream_kernel, m=m),
        grid_spec=grid_spec,
        out_shape=[jax.ShapeDtypeStruct((db, nch, K_W), F32)] * 2,
        compiler_params=_params(("parallel", "arbitrary")),
        name="cmp_stream",
    )(page_table, *([cache3] * CMP_PAGES), bdk, bdv)


def _block_diag_q(q):
    lo = _iota((8, LANES), 1) < HEAD_DIM
    zero = jnp.zeros((8, LANES), BF16)
    blocks = []
    for r in range(GROUP):
        for g in range(N_KV):
            hh = g * GROUP + r
            pair = q[:, (hh // 2) * LANES:(hh // 2 + 1) * LANES]
            if hh % 2 != g % 2:
                pair = pltpu.roll(pair.astype(F32), HEAD_DIM, 1).astype(BF16)
            keep = jnp.where(lo, pair, zero) if g % 2 == 0 else jnp.where(lo, zero, pair)
            blocks.append(jnp.concatenate([keep, zero] if g < 2 else [zero, keep], axis=1))
    return jnp.concatenate(blocks, axis=0)


def _cmp_sample_kernel(kcs_ref, vcs_ref, q_ref, pek_ref, wk4_ref, pev_ref, wv4_ref, gk_ref, bd4_ref,
                       ex4_ref, cos_ref, sin_ref, ovt_ref, qbd_ref, oc_ref, pen_ref,
                       *, past_len, n_slc, n_top):
    nch = kcs_ref.shape[0]
    pos_k = _dot(pek_ref[...], wk4_ref[...])[0:1]
    pos_v = _dot(pev_ref[...], wv4_ref[...])[0:1]
    yk = _head_rmsnorm(kcs_ref[...] + pos_k, gk_ref[...], bd4_ref[...], ex4_ref[...])
    kcn = jnp.concatenate([_rope128(yk[:, c * LANES:(c + 1) * LANES], cos_ref[...], sin_ref[...])
                           for c in range(K_W // LANES)], axis=1).astype(BF16)
    vc = (vcs_ref[...] + pos_v).astype(BF16)
    qbd = _block_diag_q(q_ref[...])
    qbd_ref[...] = qbd

    s = _dot_nt(qbd, kcn)
    mi = _iota((QROWS, nch), 1)
    t_row = past_len + (_iota((QROWS, nch), 0) & 7)
    mask = jnp.logical_and(mi * CMP_STRIDE + (CMP_STRIDE - 1) <= t_row, mi >= 1)
    s = jnp.where(mask, s, NEG)
    mx = jnp.max(s, axis=1, keepdims=True)
    e = jnp.where(mask, jnp.exp(s - mx), 0.0)
    p = (e / jnp.maximum(jnp.sum(e, axis=1, keepdims=True), 1e-30)).astype(BF16)
    oc_ref[...] = _dot(p, vc)

    imp = _dot_nt(ovt_ref[...], p)
    imp = imp + pltpu.roll(imp, 32, 1) + pltpu.roll(imp, 64, 1) + pltpu.roll(imp, 96, 1)
    nsp = imp.shape[0]
    j = _iota((nsp, QROWS), 0)
    t_lane = past_len + (_iota((nsp, QROWS), 1) & 7)
    cur = t_lane >> 6
    forced = jnp.logical_or(j == 0, jnp.logical_or(j == cur, j == cur - 1))
    rank = jnp.where(forced, RANK_FORCED, jnp.where(j * SEL_BLOCK <= t_lane, imp, RANK_INVALID))
    rank = jnp.where(j < n_slc, rank, RANK_REMOVED)
    sel = _rank_select(rank, n_top)
    pen_t = jnp.where(sel, 0.0, PEN)
    for blk in range(nsp // LANES):
        pen_ref[blk] = pen_t[blk * LANES:(blk + 1) * LANES, :].T.astype(BF16)


def _cmp_sample(kcs, vcs, q, cw_tail, cos, sin, ovt, past_len, n_slc, n_top):
    db, nch, _ = kcs.shape
    nsp = ovt.shape[0]
    per_b = lambda shape: pl.BlockSpec((None,) + shape, lambda b: (b,) + (0,) * len(shape))
    return pl.pallas_call(
        functools.partial(_cmp_sample_kernel, past_len=past_len, n_slc=n_slc, n_top=n_top),
        grid=(db,),
        in_specs=[per_b((nch, K_W)), per_b((nch, K_W)), per_b((8, D_MODEL))]
        + [_const_spec(a, 1) for a in cw_tail] + [_const_spec(cos, 1), _const_spec(sin, 1), _const_spec(ovt, 1)],
        out_specs=[per_b((QROWS, K_W)), per_b((QROWS, K_W)), per_b((nsp // LANES, QROWS, LANES))],
        out_shape=[jax.ShapeDtypeStruct((db, QROWS, K_W), BF16),
                   jax.ShapeDtypeStruct((db, QROWS, K_W), F32),
                   jax.ShapeDtypeStruct((db, nsp // LANES, QROWS, LANES), BF16)],
        compiler_params=_params(("parallel",)),
        name="cmp_sample",
    )(kcs, vcs, q, *cw_tail, cos, sin, ovt)


def _slc_stream_kernel(pt_ref, *refs):
    pages = refs[:SLC_PAGES]
    qbd_ref, pen_ref, m_out, l_out, acc_out, m_ref, l_ref, acc_ref = refs[SLC_PAGES:]
    s = pl.program_id(1)
    page = pages[0].shape[0]
    nk = SLC_PAGES * page
    blocks_per_step = nk // SEL_BLOCK

    @pl.when(s == 0)
    def _():
        m_ref[...] = jnp.full(m_ref.shape, NEG, F32)
        l_ref[...] = jnp.zeros(l_ref.shape, F32)
        acc_ref[...] = jnp.zeros(acc_ref.shape, F32)

    k = jnp.concatenate([pg[:, 0:K_W] for pg in pages], axis=0).astype(BF16)
    v = jnp.concatenate([pg[:, K_W:] for pg in pages], axis=0).astype(BF16)
    steps_per_lane_block = LANES // blocks_per_step
    pen = pen_ref[s // steps_per_lane_block]
    jrow = _iota((LANES, nk), 0)
    blk = (s % steps_per_lane_block) * blocks_per_step + (_iota((LANES, nk), 1) >> 6)
    expand = jnp.where(jrow == blk, 1.0, 0.0).astype(BF16)
    sc = _dot_nt(qbd_ref[...], k) + _dot(pen, expand)
    m_old = m_ref[...]
    m_new = jnp.maximum(m_old, jnp.max(sc, axis=1, keepdims=True))
    alpha = jnp.exp(m_old - m_new)
    p = jnp.exp(sc - m_new)
    l_ref[...] = alpha * l_ref[...] + jnp.sum(p, axis=1, keepdims=True)
    acc_ref[...] = alpha * acc_ref[...] + _dot(p.astype(BF16), v)
    m_ref[...] = m_new

    @pl.when(s == pl.num_programs(1) - 1)
    def _():
        m_out[...] = jnp.broadcast_to(m_ref[...], m_out.shape)
        l_out[...] = jnp.broadcast_to(l_ref[...], l_out.shape)
        acc_out[...] = acc_ref[...]


def _slc_stream(page_table, cache3, qbd, pen):
    db, n_pages = page_table.shape
    page = cache3.shape[1]
    nsteps = n_pages // SLC_PAGES

    def page_spec(k):
        return pl.BlockSpec((None, page, KV_W), lambda b, s, pt: (pt[b, s * SLC_PAGES + k], 0, 0))

    per_b = lambda shape: pl.BlockSpec((None,) + shape, lambda b, s, pt: (b,) + (0,) * len(shape))
    grid_spec = pltpu.PrefetchScalarGridSpec(
        num_scalar_prefetch=1,
        grid=(db, nsteps),
        in_specs=[page_spec(k) for k in range(SLC_PAGES)] + [per_b((QROWS, K_W)), per_b(pen.shape[1:])],
        out_specs=[per_b((QROWS, LANES)), per_b((QROWS, LANES)), per_b((QROWS, K_W))],
        scratch_shapes=[pltpu.VMEM((QROWS, 1), F32), pltpu.VMEM((QROWS, 1), F32), pltpu.VMEM((QROWS, K_W), F32)],
    )
    return pl.pallas_call(
        _slc_stream_kernel,
        grid_spec=grid_spec,
        out_shape=[jax.ShapeDtypeStruct((db, QROWS, LANES), F32), jax.ShapeDtypeStruct((db, QROWS, LANES), F32),
                   jax.ShapeDtypeStruct((db, QROWS, K_W), F32)],
        compiler_params=_params(("parallel", "arbitrary")),
        name="slc_stream",
    )(page_table, *([cache3] * SLC_PAGES), qbd, pen)


def _finish_sample_kernel(m_ref, l_ref, acc_ref, oc_ref, qbd_ref, pen_ref, slc_new_ref, win_new_ref,
                          win_ref, gates_ref, o_ref, *, past_len, new_blk):
    qbd = qbd_ref[...]
    tok = _iota((QROWS, LANES), 0) & 7
    col = _iota((QROWS, LANES), 1)
    pad = jnp.zeros((LANES - 8, K_W), F32)

    kn = jnp.concatenate([slc_new_ref[:, 0:K_W], pad], axis=0).astype(BF16)
    vn = jnp.concatenate([slc_new_ref[:, K_W:], pad], axis=0).astype(BF16)
    pen_col = pen_ref[new_blk // LANES][:, new_blk % LANES:new_blk % LANES + 1].astype(F32)
    sn = jnp.where(col <= tok, _dot_nt(qbd, kn) + pen_col, NEG)
    m_old = m_ref[:, 0:1]
    m_new = jnp.maximum(m_old, jnp.max(sn, axis=1, keepdims=True))
    alpha = jnp.exp(m_old - m_new)
    pn = jnp.exp(sn - m_new)
    l_new = alpha * l_ref[:, 0:1] + jnp.sum(pn, axis=1, keepdims=True)
    o_slc = (alpha * acc_ref[...] + _dot(pn.astype(BF16), vn)) / l_new

    wlen = win_ref.shape[0]
    kw = jnp.concatenate([win_ref[:, 0:K_W], win_new_ref[:, 0:K_W], pad], axis=0).astype(BF16)
    vw = jnp.concatenate([win_ref[:, K_W:], win_new_ref[:, K_W:], pad], axis=0).astype(BF16)
    nkw = wlen + LANES
    idx = _iota((QROWS, nkw), 1)
    t_row = past_len + (_iota((QROWS, nkw), 0) & 7)
    kpos = past_len - wlen + idx
    dist = t_row - kpos
    wmask = jnp.logical_and(jnp.logical_and(dist >= 0, dist <= WINDOW),
                            jnp.logical_and(kpos >= 0, idx < wlen + 8))
    sw = jnp.where(wmask, _dot_nt(qbd, kw), NEG)
    mw = jnp.max(sw, axis=1, keepdims=True)
    ew = jnp.where(wmask, jnp.exp(sw - mw), 0.0)
    pw = ew / jnp.maximum(jnp.sum(ew, axis=1, keepdims=True), 1e-30)
    o_win = _dot(pw.astype(BF16), vw)

    o_cmp = oc_ref[...]
    gates = gates_ref[...]
    lo = _iota((8, LANES), 1) < HEAD_DIM
    for c in range(D_MODEL // LANES):
        g = c // 2
        halves = []
        for e in range(2):
            r = 2 * (c % 2) + e
            rows = slice(r * 32 + g * 8, r * 32 + g * 8 + 8)
            lanes = slice((g // 2) * LANES, (g // 2 + 1) * LANES)
            base = (g * GROUP + r) * 3
            blk = (gates[:, base:base + 1] * o_cmp[rows, lanes] + gates[:, base + 1:base + 2] * o_slc[rows, lanes]
                   + gates[:, base + 2:base + 3] * o_win[rows, lanes])
            if g % 2 != e:
                blk = pltpu.roll(blk, HEAD_DIM, 1)
            halves.append(blk)
        o_ref[:, c * LANES:(c + 1) * LANES] = jnp.where(lo, halves[0], halves[1]).astype(BF16)


def _finish_sample(m, l, acc, oc, qbd, pen, slc_new, win_new, win3, gates, past_len, new_blk):
    db = m.shape[0]
    per_b = lambda shape: pl.BlockSpec((None,) + shape, lambda b: (b,) + (0,) * len(shape))
    ins = (m, l, acc, oc, qbd, pen, slc_new, win_new, win3, gates)
    return pl.pallas_call(
        functools.partial(_finish_sample_kernel, past_len=past_len, new_blk=new_blk),
        grid=(db,),
        in_specs=[per_b(a.shape[1:]) for a in ins],
        out_specs=per_b((8, D_MODEL)),
        out_shape=jax.ShapeDtypeStruct((db, 8, D_MODEL), BF16),
        compiler_params=_params(("parallel",)),
        name="finish_sample",
    )(*ins)


def _cmp_weights(w_c):
    eye = jnp.eye(N_KV, dtype=F32)
    bd = jnp.einsum("gh,lde->lgdhe", eye, w_c).reshape(CMP_LEN, K_W, K_W)
    return jnp.concatenate([bd[:CMP_STRIDE], bd[CMP_STRIDE:]], axis=2).astype(BF16)


def _pos_operands(pe, w_c):
    pe_rows = jnp.tile(pe.reshape(1, CMP_LEN * HEAD_DIM), (SUBLANES, 1)).astype(BF16)
    w4 = jnp.tile(w_c.reshape(CMP_LEN * HEAD_DIM, HEAD_DIM), (1, N_KV)).astype(BF16)
    return pe_rows, w4


def _tile_gain(g, width):
    return jnp.tile(g, width // g.shape[0]).reshape(1, width).astype(F32)


def kernel(x_prompt, x_sample, cache_cmp_kv, cache_slc_kv, cache_win_kv, state_conv, state_ffn_conv,
           page_table, g_norm_mix, w_in, w_dw, b_dw, g_ln_conv, b_ln_conv, w_conv_out, g_q, g_k_cmp,
           g_k_slc, g_k_win, w_cmp_k, w_cmp_v, pe_cmp_k, pe_cmp_v, w_nsa_out, w_out, g_norm_ffn, w_up,
           w_ffn_dw, b_ffn_dw, w_down):
    bp, seq, _ = x_prompt.shape
    db, dseq, _ = x_sample.shape
    n_pool, page = cache_cmp_kv.shape[0], cache_cmp_kv.shape[1]
    n_pages = page_table.shape[1]
    past_len = n_pages * page
    wlen = cache_win_kv.shape[1]
    assert dseq == 8 and seq % QBLK == 0 and seq // SEL_BLOCK <= LANES
    assert past_len % SEL_BLOCK == 0 and wlen == WINDOW and past_len >= WINDOW
    assert n_pages % CMP_PAGES == 0 and n_pages % SLC_PAGES == 0 and page % CMP_STRIDE == 0
    assert LANES % (SLC_PAGES * page // SEL_BLOCK) == 0

    offs = np.cumsum((2 * C_CONV, N_HEADS * HEAD_DIM, 3 * KV_W, 3 * N_HEADS))
    wu = w_in[:, :offs[0]].astype(BF16)
    wq = w_in[:, offs[0]:offs[1]].astype(BF16)
    wkv = w_in[:, offs[1]:offs[2]].astype(BF16)
    wgn = jnp.pad(w_in[:, offs[2]:offs[3]], ((0, 0), (0, LANES - 3 * N_HEADS))).astype(BF16)
    wgm = w_in[:, offs[3]:].astype(BF16)
    bd16, ex16 = _head_sum_mats(D_MODEL)
    bd4, ex4 = _head_sum_mats(K_W)
    in_wts = (g_norm_mix.reshape(1, D_MODEL), wu, wq, wkv, wgn, wgm, _tile_gain(g_q, D_MODEL),
              _tile_gain(g_k_slc, K_W), _tile_gain(g_k_win, K_W), bd16, ex16, bd4, ex4)
    wdw = jnp.pad(w_dw, ((0, 1), (0, 0)))
    conv_wts = (wdw, b_dw.reshape(1, C_CONV), g_ln_conv.reshape(1, C_CONV), b_ln_conv.reshape(1, C_CONV),
                w_conv_out.astype(BF16))
    wno = w_nsa_out.astype(BF16)
    wout = w_out.astype(BF16)
    gffn = g_norm_ffn.reshape(1, D_MODEL)
    wup = w_up.reshape(D_MODEL, 2 * FFN_NC, FFN_CW).transpose(1, 0, 2).astype(BF16)
    wdn = w_down.reshape(FFN_NC, FFN_CW, D_MODEL).astype(BF16)
    wfd = jnp.concatenate([w_ffn_dw, b_ffn_dw[None, :], jnp.zeros((4, 2 * D_FF), F32)], axis=0)
    wfd = wfd.reshape(SUBLANES, 2 * FFN_NC, FFN_CW).transpose(1, 0, 2)
    bdk = _cmp_weights(w_cmp_k)
    bdv = _cmp_weights(w_cmp_v)
    pek, wk4 = _pos_operands(pe_cmp_k, w_cmp_k)
    pev, wv4 = _pos_operands(pe_cmp_v, w_cmp_v)
    gkc = _tile_gain(g_k_cmp, K_W)
    cw_tail = (pek, wk4, pev, wv4, gkc, bd4, ex4)

    cos_p, sin_p = _rope_tables(jnp.arange(seq))
    (glu, q, cmp_rows, slc_rows, win_rows, gn, gm, k2s, vts, k2w, vtw) = _inproj(
        x_prompt, cos_p, sin_p, in_wts, True)
    zeros_halo = jnp.zeros((bp, CONV_HALO, C_CONV), F32)
    a_out, conv_tail = _conv_module(glu, glu, zeros_halo, *conv_wts, tt=128, rb=32)
    nch = seq // CMP_STRIDE
    n_cmp = nch - 1
    n_slc = seq // SEL_BLOCK
    cos_c, sin_c = _rope_tables(jnp.arange(nch) * CMP_STRIDE + (CMP_LEN - 1))
    kc2, vct = _cmp_prompt(cmp_rows.reshape(bp, nch, CMP_STRIDE * KV_W), (bdk, bdv) + cw_tail, cos_c, sin_c)
    e_tab = jnp.asarray((np.arange(seq)[:, None] // SEL_BLOCK == np.arange(LANES)[None, :]), BF16)
    ovt = jnp.asarray(np.tile(_overlap_t(n_slc, nch, 0, n_cmp), (1, GROUP)), BF16)
    o_p = _attn_prompt(q, gn, kc2, vct, k2s, vts, k2w, vtw, e_tab, ovt, n_cmp, min(N_SEL, n_slc))
    n_tok = bp * seq
    x1 = _merge(o_p.reshape(n_tok, D_MODEL), a_out.reshape(n_tok, D_MODEL), gm.reshape(n_tok, 2 * D_MODEL),
                x_prompt.reshape(n_tok, D_MODEL), wno, wout, 512)
    zeros_fix = jnp.zeros((bp, SUBLANES, 2 * D_FF), F32)
    y_p, ffn_tail = _ffn(x1.reshape(bp, seq, D_MODEL), gffn, wup, wfd, wdn, zeros_fix, zeros_fix, 512, 512)

    kv5 = lambda rows, b, t: rows.reshape(b, t, 2, N_KV, HEAD_DIM)
    out_p = (y_p, kv5(cmp_rows, bp, seq), kv5(slc_rows, bp, seq),
             kv5(win_rows[:, seq - min(WINDOW, seq):], bp, min(WINDOW, seq)),
             conv_tail[:, CONV_HALO - (CONV_K - 1):], ffn_tail[:, :FFN_CONV_K - 1])

    n_s = db * dseq
    pos_s = past_len + (jnp.arange(QBLK) % dseq)
    cos_s, sin_s = _rope_tables(pos_s)
    xs_pad = jnp.pad(x_sample.reshape(1, n_s, D_MODEL), ((0, 0), (0, QBLK - n_s), (0, 0))) if n_s < QBLK \
        else x_sample.reshape(1, n_s, D_MODEL)
    assert xs_pad.shape[1] == QBLK
    (glu_s, q_s, cmp_s, slc_s, win_s, gn_s, gm_s) = [a[0, :n_s] for a in _inproj(xs_pad, cos_s, sin_s, in_wts, False)]
    st32 = jnp.pad(state_conv, ((0, 0), (CONV_HALO - (CONV_K - 1), 0), (0, 0)))
    a_s, conv_tail_s = _conv_module(glu_s.reshape(db, dseq, C_CONV), st32, st32, *conv_wts, tt=dseq, rb=dseq)

    cache_cmp3 = cache_cmp_kv.reshape(n_pool, page // CMP_STRIDE, CMP_STRIDE * KV_W)
    cache_slc3 = cache_slc_kv.reshape(n_pool, page, KV_W)
    kcs, vcs = _cmp_stream(page_table, cache_cmp3, bdk, bdv)
    nch_s = past_len // CMP_STRIDE
    n_cmp_s = -(-(past_len + dseq) // CMP_STRIDE) - 1
    n_slc_s = -(-(past_len + dseq) // SEL_BLOCK)
    nsp = -(-n_slc_s // LANES) * LANES
    cos_cs, sin_cs = _rope_tables(jnp.arange(nch_s) * CMP_STRIDE + (CMP_STRIDE - 1))
    ovt_s = jnp.asarray(_overlap_t(nsp, nch_s, 1, n_cmp_s), BF16)
    qbd, oc_s, pen_s = _cmp_sample(kcs, vcs, q_s.reshape(db, dseq, D_MODEL), cw_tail, cos_cs, sin_cs, ovt_s,
                                   past_len, n_slc_s, min(N_SEL, n_slc_s))
    m_s, l_s, acc_s = _slc_stream(page_table, cache_slc3, qbd, pen_s)
    win3 = cache_win_kv.reshape(db, wlen, KV_W)
    o_s = _finish_sample(m_s, l_s, acc_s, oc_s, qbd, pen_s, slc_s.reshape(db, dseq, KV_W),
                         win_s.reshape(db, dseq, KV_W), win3, gn_s.reshape(db, dseq, LANES),
                         past_len, past_len // SEL_BLOCK)
    x1_s = _merge(o_s.reshape(n_s, D_MODEL), a_s.reshape(n_s, D_MODEL), gm_s, x_sample.reshape(n_s, D_MODEL),
                  wno, wout, n_s)
    z1 = jnp.zeros((db, dseq - 1, 2 * D_FF), F32)
    fix1 = jnp.concatenate([state_ffn_conv[:, 1:2], z1], axis=1).reshape(1, n_s, 2 * D_FF)
    fix2 = jnp.concatenate([state_ffn_conv, z1[:, 1:]], axis=1).reshape(1, n_s, 2 * D_FF)
    y_s, up_s = _ffn(x1_s.reshape(1, n_s, D_MODEL), gffn, wup, wfd, wdn, fix1, fix2, n_s, dseq)

    win_all = jnp.concatenate([win3, win_s.reshape(db, dseq, KV_W)], axis=1)
    keep = min(WINDOW, past_len + dseq)
    out_s = (y_s.reshape(db, dseq, D_MODEL), kv5(cmp_s, db, dseq), kv5(slc_s, db, dseq),
             kv5(win_all[:, wlen + dseq - keep:], db, keep),
             conv_tail_s[:, CONV_HALO - (CONV_K - 1):],
             up_s.reshape(db, dseq, 2 * D_FF)[:, dseq - (FFN_CONV_K - 1):])
    return (out_p[0], out_s[0]) + out_p[1:] + out_s[1:]
```

```python
import functools

import numpy as np
import jax
import jax.numpy as jnp
from jax import lax
from jax.experimental import pallas as pl
from jax.experimental.pallas import tpu as pltpu

F32 = jnp.float32
BF16 = jnp.bfloat16

D_MODEL = 1024
N_HEADS = 16
HEAD_DIM = 64
N_KV = 4
GROUP = N_HEADS // N_KV
CMP_STRIDE = 16
CMP_LEN = 2 * CMP_STRIDE
SEL_BLOCK = 64
N_SEL = 16
WINDOW = 512
C_CONV = D_MODEL // 2
CONV_K = 31
FFN_CONV_K = 3
D_FF = 2816
ROPE_THETA = 10000.0
EPS = 1e-6
KV_W = 2 * N_KV * HEAD_DIM
K_W = N_KV * HEAD_DIM

LANES = 128
SUBLANES = 8
QBLK = 256
CONV_HALO = 32
VMEM_LIMIT = 56 * 1024 * 1024
FFN_CW = 256
FFN_NC = D_FF // FFN_CW
CMP_PAGES = 16
SLC_PAGES = 8
QROWS = GROUP * N_KV * 8

NEG = -1e30
PEN = -1e9
RANK_FORCED = 1e30
RANK_INVALID = -1e30
RANK_REMOVED = -2e30


def _dot(a, b):
    return jnp.dot(a, b, preferred_element_type=F32)


def _dot_nt(a, b):
    return lax.dot_general(a, b, (((1,), (1,)), ((), ())), preferred_element_type=F32)


def _sigmoid(x):
    return 1.0 / (1.0 + jnp.exp(-x))


def _iota(shape, dim):
    return lax.broadcasted_iota(jnp.int32, shape, dim)


def _params(sem):
    return pltpu.CompilerParams(dimension_semantics=sem, vmem_limit_bytes=VMEM_LIMIT)


def _const_spec(a, ngrid):
    nd = a.ndim
    if ngrid == 1:
        return pl.BlockSpec(a.shape, lambda i: (0,) * nd)
    if ngrid == 2:
        return pl.BlockSpec(a.shape, lambda i, j: (0,) * nd)
    return pl.BlockSpec(a.shape, lambda i, j, k: (0,) * nd)


def _rope_tables(pos):
    half = HEAD_DIM // 2
    inv = ROPE_THETA ** (-jnp.arange(half, dtype=F32) * (2.0 / HEAD_DIM))
    ang = pos.astype(F32)[:, None] * inv[None, :]
    cos, sin = jnp.cos(ang), jnp.sin(ang)
    return jnp.tile(cos, (1, 4)), jnp.tile(jnp.concatenate([-sin, sin], axis=1), (1, 2))


def _head_sum_mats(width):
    heads = width // HEAD_DIM
    lane_head = np.arange(width) // HEAD_DIM
    bd = (lane_head[:, None] == np.arange(LANES)[None, :]).astype(np.float32)
    ex = np.zeros((2 * LANES, width), np.float32)
    ex[:heads] = (np.arange(heads)[:, None] == lane_head[None, :])
    ex[LANES:LANES + heads] = ex[:heads]
    return jnp.asarray(bd, BF16), jnp.asarray(ex, BF16)


def _overlap_t(n_slc_rows, n_cols, col_shift, n_cmp):
    n = np.arange(n_cols) - col_shift
    j = np.arange(n_slc_rows)
    cs = n * CMP_STRIDE
    ss = j * SEL_BLOCK
    ov = (cs[None, :] < ss[:, None] + SEL_BLOCK) & (cs[None, :] + CMP_LEN > ss[:, None])
    ov &= (n[None, :] >= 0) & (n[None, :] < n_cmp)
    return ov.astype(np.float32)


def _head_rmsnorm(x, gain, bd, ex):
    ssum = _dot((x * x).astype(BF16), bd)
    r = lax.rsqrt(ssum * (1.0 / HEAD_DIM) + EPS)
    r_hi = r.astype(BF16)
    r_lo = (r - r_hi.astype(F32)).astype(BF16)
    rfull = _dot(jnp.concatenate([r_hi, r_lo], axis=1), ex)
    return x * rfull * gain


def _rope128(y, cos, sin):
    lane = _iota(y.shape, 1)
    first = (lane & (HEAD_DIM - 1)) < (HEAD_DIM // 2)
    rot = jnp.where(first, pltpu.roll(y, LANES - HEAD_DIM // 2, 1), pltpu.roll(y, HEAD_DIM // 2, 1))
    return y * cos + rot * sin


def _dup_heads(chunk):
    lo = _iota(chunk.shape, 1) < HEAD_DIM
    swapped = pltpu.roll(chunk, HEAD_DIM, 1)
    return jnp.where(lo, chunk, swapped), jnp.where(lo, swapped, chunk)


def _rank_select(val, n_rounds):
    rows = val.shape[0]
    j = _iota(val.shape, 0).astype(F32)
    sel = jnp.zeros(val.shape, F32)
    for _ in range(n_rounds):
        mx = jnp.max(val, axis=0, keepdims=True)
        idx = jnp.min(jnp.where(val == mx, j, float(rows)), axis=0, keepdims=True)
        hit = j == idx
        sel = jnp.where(hit, 1.0, sel)
        val = jnp.where(hit, RANK_REMOVED, val)
    return sel > 0.5


def _inproj_kernel(x_ref, gmix_ref, wu_ref, wq_ref, wkv_ref, wgn_ref, wgm_ref, gq_ref, gks_ref,
                   gkw_ref, bd16_ref, ex16_ref, bd4_ref, ex4_ref, cos_ref, sin_ref,
                   glu_ref, q_ref, cmp_ref, slc_ref, win_ref, gn_ref, gm_ref, *attn_refs):
    x = x_ref[...]
    ms = jnp.mean(x * x, axis=-1, keepdims=True)
    h = (x * lax.rsqrt(ms + EPS) * gmix_ref[...]).astype(BF16)
    cos = cos_ref[...]
    sin = sin_ref[...]

    u = _dot(h, wu_ref[...])
    glu_ref[...] = u[:, :C_CONV] * _sigmoid(u[:, C_CONV:])

    yq = _head_rmsnorm(_dot(h, wq_ref[...]), gq_ref[...], bd16_ref[...], ex16_ref[...])
    scale = HEAD_DIM ** -0.5
    for c in range(D_MODEL // LANES):
        sl = slice(c * LANES, (c + 1) * LANES)
        q_ref[:, sl] = (_rope128(yq[:, sl], cos, sin) * scale).astype(BF16)

    zkv = _dot(h, wkv_ref[...])
    cmp_ref[...] = zkv[:, :KV_W]

    def kv_branch(z, gain_ref, rows_ref, k2_ref, v2_ref):
        yk = _head_rmsnorm(z[:, :K_W], gain_ref[...], bd4_ref[...], ex4_ref[...])
        v = z[:, K_W:]
        rows_ref[:, K_W:] = v
        for c in range(K_W // LANES):
            sl = slice(c * LANES, (c + 1) * LANES)
            kr = _rope128(yk[:, sl], cos, sin)
            rows_ref[:, sl] = kr
            if k2_ref is not None:
                for ref, chunk in ((k2_ref, kr), (v2_ref, v[:, sl])):
                    a2, b2 = _dup_heads(chunk)
                    ref[2 * c] = a2.astype(BF16)
                    ref[2 * c + 1] = b2.astype(BF16)

    if attn_refs:
        k2s_ref, v2s_ref, k2w_ref, v2w_ref = attn_refs
    else:
        k2s_ref = v2s_ref = k2w_ref = v2w_ref = None
    kv_branch(zkv[:, KV_W:2 * KV_W], gks_ref, slc_ref, k2s_ref, v2s_ref)
    kv_branch(zkv[:, 2 * KV_W:], gkw_ref, win_ref, k2w_ref, v2w_ref)

    gn_ref[...] = _sigmoid(_dot(h, wgn_ref[...]))
    gm_ref[...] = _sigmoid(_dot(h, wgm_ref[...]))


def _inproj(x, cos, sin, wts, attn_layouts):
    B, T, _ = x.shape
    tm = QBLK
    assert T % tm == 0
    nt = T // tm
    tok = lambda w: pl.BlockSpec((None, tm, w), lambda b, t: (b, t, 0))
    tab = pl.BlockSpec((tm, LANES), lambda b, t: (t, 0))
    in_specs = [tok(D_MODEL)] + [_const_spec(a, 2) for a in wts] + [tab, tab]
    out_shape = [
        jax.ShapeDtypeStruct((B, T, C_CONV), F32),
        jax.ShapeDtypeStruct((B, T, D_MODEL), BF16),
        jax.ShapeDtypeStruct((B, T, KV_W), F32),
        jax.ShapeDtypeStruct((B, T, KV_W), F32),
        jax.ShapeDtypeStruct((B, T, KV_W), F32),
        jax.ShapeDtypeStruct((B, T, N_KV * LANES), F32),
        jax.ShapeDtypeStruct((B, T, 2 * D_MODEL), F32),
    ]
    out_specs = [tok(C_CONV), tok(D_MODEL), tok(KV_W), tok(KV_W), tok(KV_W), tok(N_KV * LANES),
                 tok(2 * D_MODEL)]
    if attn_layouts:
        dup = jax.ShapeDtypeStruct((B, N_KV, T, LANES), BF16)
        dup_spec = pl.BlockSpec((None, N_KV, tm, LANES), lambda b, t: (b, 0, t, 0))
        out_shape += [dup] * 4
        out_specs += [dup_spec] * 4
    return pl.pallas_call(
        _inproj_kernel,
        grid=(B, nt),
        in_specs=in_specs,
        out_specs=out_specs,
        out_shape=out_shape,
        compiler_params=_params(("parallel", "parallel")),
        name="inproj",
    )(x, *wts, cos, sin)


def _conv_kernel(glu_ref, halo_ref, st_ref, wdw_ref, bdw_ref, gln_ref, bln_ref, wco_ref,
                 a_ref, cst_ref, xc_ref, s_ref, *, tt, rb):
    ti = pl.program_id(1)

    @pl.when(ti == 0)
    def _():
        xc_ref[0:CONV_HALO, :] = st_ref[...]

    @pl.when(ti > 0)
    def _():
        xc_ref[0:CONV_HALO, :] = halo_ref[...]

    xc_ref[CONV_HALO:CONV_HALO + tt, :] = glu_ref[...]
    first = CONV_HALO - (CONV_K - 1)
    for r0 in range(0, tt, rb):
        acc = jnp.broadcast_to(bdw_ref[...], (rb, C_CONV))
        for k in range(CONV_K):
            acc = acc + wdw_ref[k:k + 1, :] * xc_ref[r0 + first + k:r0 + first + k + rb, :]
        mu = jnp.mean(acc, axis=-1, keepdims=True)
        d = acc - mu
        var = jnp.mean(d * d, axis=-1, keepdims=True)
        y = d * lax.rsqrt(var + EPS) * gln_ref[...] + bln_ref[...]
        s_ref[r0:r0 + rb, :] = y * _sigmoid(y)
    a_ref[...] = _dot(s_ref[...].astype(BF16), wco_ref[...])
    cst_ref[...] = xc_ref[tt:tt + CONV_HALO, :]


def _conv_module(glu, halo_src, state32, wdw, bdw, gln, bln, wco, tt, rb):
    B, T, _ = glu.shape
    nt = T // tt
    hb = max(tt // CONV_HALO, 1)
    return pl.pallas_call(
        functools.partial(_conv_kernel, tt=tt, rb=rb),
        grid=(B, nt),
        in_specs=[
            pl.BlockSpec((None, tt, C_CONV), lambda b, t: (b, t, 0)),
            pl.BlockSpec((None, CONV_HALO, C_CONV), lambda b, t: (b, jnp.maximum(t * hb - 1, 0), 0)),
            pl.BlockSpec((None, CONV_HALO, C_CONV), lambda b, t: (b, 0, 0)),
            _const_spec(wdw, 2), _const_spec(bdw, 2), _const_spec(gln, 2), _const_spec(bln, 2),
            _const_spec(wco, 2),
        ],
        out_specs=[
            pl.BlockSpec((None, tt, D_MODEL), lambda b, t: (b, t, 0)),
            pl.BlockSpec((None, CONV_HALO, C_CONV), lambda b, t: (b, 0, 0)),
        ],
        out_shape=[
            jax.ShapeDtypeStruct((B, T, D_MODEL), F32),
            jax.ShapeDtypeStruct((B, CONV_HALO, C_CONV), F32),
        ],
        scratch_shapes=[pltpu.VMEM((CONV_HALO + tt, C_CONV), F32), pltpu.VMEM((tt, C_CONV), F32)],
        compiler_params=_params(("parallel", "arbitrary")),
        name="conv_module",
    )(glu, halo_src, state32, wdw, bdw, gln, bln, wco)


def _merge_kernel(o_ref, a_ref, gm_ref, x_ref, wno_ref, wout_ref, x1_ref):
    b_out = _dot(o_ref[...], wno_ref[...])
    gm = gm_ref[...]
    m = gm[:, :D_MODEL] * a_ref[...] + gm[:, D_MODEL:] * b_out
    x1_ref[...] = x_ref[...] + _dot(m.astype(BF16), wout_ref[...])


def _merge(o, a_out, gm, x, wno, wout, tm):
    N = x.shape[0]
    row = lambda w: pl.BlockSpec((tm, w), lambda i: (i, 0))
    return pl.pallas_call(
        _merge_kernel,
        grid=(N // tm,),
        in_specs=[row(D_MODEL), row(D_MODEL), row(2 * D_MODEL), row(D_MODEL),
                  _const_spec(wno, 1), _const_spec(wout, 1)],
        out_specs=row(D_MODEL),
        out_shape=jax.ShapeDtypeStruct((N, D_MODEL), F32),
        compiler_params=_params(("parallel",)),
        name="merge",
    )(o, a_out, gm, x, wno, wout)


def _ffn_kernel(x1_ref, gn_ref, wup_ref, wdw_ref, wdn_ref, fix1_ref, fix2_ref,
                y_ref, st_ref, prev_ref, acc_ref, *, tm, seg):
    streaming = seg == tm
    x1 = x1_ref[...]
    ms = jnp.mean(x1 * x1, axis=-1, keepdims=True)
    h = (x1 * lax.rsqrt(ms + EPS) * gn_ref[...]).astype(BF16)
    row = _iota((tm, FFN_CW), 0) & (seg - 1)

    if streaming:
        @pl.when(pl.program_id(1) == 0)
        def _():
            prev_ref[...] = fix2_ref[...]

    def conv3(c):
        cs = slice(c * FFN_CW, (c + 1) * FFN_CW)
        up = _dot(h, wup_ref[c])
        r1 = pltpu.roll(up, 1, 0)
        r2 = pltpu.roll(up, 2, 0)
        if streaming:
            p0 = prev_ref[0:1, cs]
            p1 = prev_ref[1:2, cs]
            s1 = jnp.where(row == 0, p1, r1)
            s2 = jnp.where(row == 0, p0, jnp.where(row == 1, p1, r2))
            prev_ref[0:2, cs] = up[tm - 2:tm, :]
        else:
            s1 = jnp.where(row == 0, fix1_ref[:, cs], r1)
            s2 = jnp.where(row < 2, fix2_ref[:, cs], r2)
            st_ref[:, cs] = up
        w = wdw_ref[c]
        return w[0:1] * s2 + w[1:2] * s1 + w[2:3] * up + w[3:4]

    acc_ref[...] = x1
    for c in range(FFN_NC):
        gate = conv3(c)
        val = conv3(FFN_NC + c)
        act = gate * _sigmoid(gate) * val
        acc_ref[...] += _dot(act.astype(BF16), wdn_ref[c])
    y_ref[...] = acc_ref[...]
    if streaming:
        st_ref[...] = prev_ref[...]


def _ffn(x1, gn, wup, wdw, wdn, fix1, fix2, tm, seg):
    B, T, _ = x1.shape
    nt = T // tm
    streaming = seg == tm
    tok = lambda w: pl.BlockSpec((None, tm, w), lambda b, t: (b, t, 0))
    if streaming:
        fix_specs = [pl.BlockSpec((None, SUBLANES, 2 * D_FF), lambda b, t: (b, 0, 0))] * 2
        st_shape = jax.ShapeDtypeStruct((B, SUBLANES, 2 * D_FF), F32)
        st_spec = pl.BlockSpec((None, SUBLANES, 2 * D_FF), lambda b, t: (b, 0, 0))
    else:
        fix_specs = [tok(2 * D_FF)] * 2
        st_shape = jax.ShapeDtypeStruct((B, T, 2 * D_FF), F32)
        st_spec = tok(2 * D_FF)
    return pl.pallas_call(
        functools.partial(_ffn_kernel, tm=tm, seg=seg),
        grid=(B, nt),
        in_specs=[tok(D_MODEL), _const_spec(gn, 2), _const_spec(wup, 2), _const_spec(wdw, 2),
                  _const_spec(wdn, 2)] + fix_specs,
        out_specs=[tok(D_MODEL), st_spec],
        out_shape=[jax.ShapeDtypeStruct((B, T, D_MODEL), F32), st_shape],
        scratch_shapes=[pltpu.VMEM((SUBLANES, 2 * D_FF), F32), pltpu.VMEM((tm, D_MODEL), F32)],
        compiler_params=_params(("parallel", "arbitrary")),
        name="conv_ffn",
    )(x1, gn, wup, wdw, wdn, fix1, fix2)


def _cmp_prompt_kernel(rows_ref, bdk_ref, bdv_ref, pek_ref, wk4_ref, pev_ref, wv4_ref, gk_ref,
                       bd4_ref, ex4_ref, cos_ref, sin_ref, kc2_ref, vc2_ref, *, nch):
    acc_k = jnp.zeros((nch, 2 * K_W), F32)
    acc_v = jnp.zeros((nch, 2 * K_W), F32)
    for l in range(CMP_STRIDE):
        acc_k += _dot(rows_ref[:, l * KV_W:l * KV_W + K_W].astype(BF16), bdk_ref[l])
        acc_v += _dot(rows_ref[:, l * KV_W + K_W:(l + 1) * KV_W].astype(BF16), bdv_ref[l])
    pos_k = _dot(pek_ref[...], wk4_ref[...])[0:1]
    pos_v = _dot(pev_ref[...], wv4_ref[...])[0:1]
    kc = acc_k[:, :K_W] + pltpu.roll(acc_k[:, K_W:], nch - 1, 0) + pos_k
    vc = acc_v[:, :K_W] + pltpu.roll(acc_v[:, K_W:], nch - 1, 0) + pos_v
    yk = _head_rmsnorm(kc, gk_ref[...], bd4_ref[...], ex4_ref[...])
    for c in range(K_W // LANES):
        sl = slice(c * LANES, (c + 1) * LANES)
        for ref, chunk in ((kc2_ref, _rope128(yk[:, sl], cos_ref[...], sin_ref[...])), (vc2_ref, vc[:, sl])):
            a2, b2 = _dup_heads(chunk)
            ref[2 * c] = a2.astype(BF16)
            ref[2 * c + 1] = b2.astype(BF16)


def _cmp_prompt(rows, cw, cos, sin):
    B, nch, _ = rows.shape
    return pl.pallas_call(
        functools.partial(_cmp_prompt_kernel, nch=nch),
        grid=(B,),
        in_specs=[pl.BlockSpec((None, nch, CMP_STRIDE * KV_W), lambda b: (b, 0, 0))] + [_const_spec(a, 1) for a in cw]
        + [_const_spec(cos, 1), _const_spec(sin, 1)],
        out_specs=[pl.BlockSpec((None, N_KV, nch, LANES), lambda b: (b, 0, 0, 0))] * 2,
        out_shape=[jax.ShapeDtypeStruct((B, N_KV, nch, LANES), BF16)] * 2,
        compiler_params=_params(("parallel",)),
        name="cmp_prompt",
    )(rows, *cw, cos, sin)


def _attn_prompt_kernel(q_ref, gates_ref, kc2_ref, vc2_ref, k2s_ref, v2s_ref, k2w_ref, v2w_ref,
                        e_ref, ov_ref, gex_ref, o_ref, m_ref, l_ref, acc_ref, *, n_cmp, n_slc, n_top):
    qi = pl.program_id(2)
    qs = qi * QBLK
    rows = GROUP * QBLK
    lo = _iota((QBLK, LANES), 1) < HEAD_DIM
    zero = jnp.zeros((QBLK, LANES), BF16)
    qm = []
    for p in range(GROUP // 2):
        qp = q_ref[:, p * LANES:(p + 1) * LANES]
        qm += [jnp.where(lo, qp, zero), jnp.where(lo, zero, qp)]
    q_all = jnp.concatenate(qm, axis=0)

    nch = kc2_ref.shape[0]
    n = _iota((rows, nch), 1)
    t_row = qs + (_iota((rows, nch), 0) & (QBLK - 1))
    cmask = jnp.logical_and(n * CMP_STRIDE + (CMP_LEN - 1) <= t_row, n < n_cmp)
    s = jnp.where(cmask, _dot_nt(q_all, kc2_ref[...]), NEG)
    mx = jnp.max(s, axis=1, keepdims=True)
    e = jnp.where(cmask, jnp.exp(s - mx), 0.0)
    p_cmp = (e / jnp.maximum(jnp.sum(e, axis=1, keepdims=True), 1e-30)).astype(BF16)
    o_cmp = _dot(p_cmp, vc2_ref[...])
    imp4 = _dot(p_cmp, ov_ref[...])
    imp = imp4[0:QBLK]
    for r in range(1, GROUP):
        imp = imp + imp4[r * QBLK:(r + 1) * QBLK]
    imp_t = imp.T[0:n_slc]
    j = _iota((n_slc, QBLK), 0)
    tq = qs + _iota((n_slc, QBLK), 1)
    cur = tq >> 6
    forced = jnp.logical_or(j == 0, jnp.logical_or(j == cur, j == cur - 1))
    rank = jnp.where(forced, RANK_FORCED, jnp.where(j * SEL_BLOCK <= tq, imp_t, RANK_INVALID))
    pen_t = jnp.where(_rank_select(rank, n_top), 0.0, PEN)
    if n_slc < LANES:
        pen_t = jnp.concatenate([pen_t, jnp.zeros((LANES - n_slc, QBLK), F32)], axis=0)
    pen = pen_t.T.astype(BF16)
    qa_all = jnp.concatenate([q_all, jnp.concatenate([pen] * GROUP, axis=0)], axis=1)

    kl = _iota((rows, QBLK), 1)
    tl = _iota((rows, QBLK), 0) & (QBLK - 1)

    def reset():
        m_ref[...] = jnp.full(m_ref.shape, NEG, F32)
        l_ref[...] = jnp.zeros(l_ref.shape, F32)
        acc_ref[...] = jnp.zeros(acc_ref.shape, F32)

    def attend(kblk, v2blk, q_rows, mode):
        sc = _dot_nt(q_rows, kblk)
        if mode == "causal":
            sc = jnp.where(kl <= tl, sc, NEG)
        elif mode == "band":
            sc = jnp.where(kl >= tl, sc, NEG)
        m_old = m_ref[...]
        m_new = jnp.maximum(m_old, jnp.max(sc, axis=1, keepdims=True))
        alpha = jnp.exp(m_old - m_new)
        p = jnp.exp(sc - jnp.concatenate([m_new] * (sc.shape[1] // LANES), axis=1))
        l_ref[...] = alpha * l_ref[...] + jnp.sum(p, axis=1, keepdims=True)
        acc_ref[...] = alpha * acc_ref[...] + _dot(p.astype(BF16), v2blk)
        m_ref[...] = m_new

    def blk(ref, kb, width=QBLK):
        return ref[pl.ds(pl.multiple_of(kb * QBLK, QBLK), width), :]

    def slc_keys(kb, width=QBLK):
        return jnp.concatenate([blk(k2s_ref, kb, width), blk(e_ref, kb, width)], axis=1)

    reset()

    def slc_body(kb2, carry):
        attend(slc_keys(2 * kb2, 2 * QBLK), blk(v2s_ref, 2 * kb2, 2 * QBLK), qa_all, "none")
        return carry

    lax.fori_loop(0, qi // 2, slc_body, 0)

    @pl.when(qi % 2 == 1)
    def _():
        attend(slc_keys(qi - 1), blk(v2s_ref, qi - 1), qa_all, "none")

    attend(slc_keys(qi), blk(v2s_ref, qi), qa_all, "causal")
    o_slc = acc_ref[...] / l_ref[...]

    reset()

    @pl.when(qi >= 2)
    def _():
        attend(blk(k2w_ref, qi - 2), blk(v2w_ref, qi - 2), q_all, "band")

    @pl.when(qi >= 1)
    def _():
        attend(blk(k2w_ref, qi - 1), blk(v2w_ref, qi - 1), q_all, "none")

    attend(blk(k2w_ref, qi), blk(v2w_ref, qi), q_all, "causal")
    o_win = acc_ref[...] / l_ref[...]

    gt = gates_ref[...]
    g_hi = gt.astype(BF16)
    g_lo = (gt - g_hi.astype(F32)).astype(BF16)
    gex = _dot(jnp.concatenate([g_hi, g_lo], axis=1), gex_ref[...])
    mixed = []
    for r in range(GROUP):
        rs = slice(r * QBLK, (r + 1) * QBLK)
        gl = lambda k: gex[:, (3 * r + k) * LANES:(3 * r + k + 1) * LANES]
        mixed.append(gl(0) * o_cmp[rs] + gl(1) * o_slc[rs] + gl(2) * o_win[rs])
    for p in range(GROUP // 2):
        o_ref[:, p * LANES:(p + 1) * LANES] = jnp.where(lo, mixed[2 * p], mixed[2 * p + 1]).astype(BF16)


def _attn_prompt(q, gates, kc2, vc2, k2s, v2s, k2w, v2w, e_tab, ov, gex, n_cmp, n_slc, n_top):
    B, T, _ = q.shape
    nq = T // QBLK
    nch = kc2.shape[2]
    rows = GROUP * QBLK
    per_bg = lambda n: pl.BlockSpec((None, None, n, LANES), lambda b, g, i: (b, g, 0, 0))
    return pl.pallas_call(
        functools.partial(_attn_prompt_kernel, n_cmp=n_cmp, n_slc=n_slc, n_top=n_top),
        grid=(B, N_KV, nq),
        in_specs=[
            pl.BlockSpec((None, QBLK, K_W), lambda b, g, i: (b, i, g)),
            pl.BlockSpec((None, QBLK, LANES), lambda b, g, i: (b, i, g)),
            per_bg(nch), per_bg(nch), per_bg(T), per_bg(T), per_bg(T), per_bg(T),
            _const_spec(e_tab, 3), _const_spec(ov, 3), _const_spec(gex, 3),
        ],
        out_specs=pl.BlockSpec((None, QBLK, K_W), lambda b, g, i: (b, i, g)),
        out_shape=jax.ShapeDtypeStruct((B, T, D_MODEL), BF16),
        scratch_shapes=[pltpu.VMEM((rows, LANES), F32), pltpu.VMEM((rows, LANES), F32),
                        pltpu.VMEM((rows, LANES), F32)],
        compiler_params=_params(("parallel", "parallel", "arbitrary")),
        name="attn_prompt",
    )(q, gates, kc2, vc2, k2s, v2s, k2w, v2w, e_tab, ov, gex)


def _cmp_stream_kernel(pt_ref, *refs, m):
    pages = refs[:CMP_PAGES]
    bdk_ref, bdv_ref, kc_ref, vc_ref, fs_ref, slab_ref = refs[CMP_PAGES:]
    s = pl.program_id(1)
    per_page = m // CMP_PAGES
    nslab = KV_W // LANES

    @pl.when(s == 0)
    def _():
        fs_ref[0:SUBLANES, :] = jnp.zeros((SUBLANES, 2 * K_W), F32)

    for k, pg in enumerate(pages):
        for c in range(nslab):
            slab_ref[k, c] = pg[c * LANES:(c + 1) * LANES, :].T
    acc_k = jnp.zeros((m, 2 * K_W), F32)
    acc_v = jnp.zeros((m, 2 * K_W), F32)
    for l in range(CMP_STRIDE):
        def rows_of(c):
            return jnp.concatenate([slab_ref[k, c, pl.ds(l, per_page, stride=CMP_STRIDE), :]
                                    for k in range(CMP_PAGES)], axis=0)
        xk = jnp.concatenate([rows_of(0), rows_of(1)], axis=1)
        xv = jnp.concatenate([rows_of(2), rows_of(3)], axis=1)
        acc_k += _dot(xk.astype(BF16), bdk_ref[l])
        acc_v += _dot(xv.astype(BF16), bdv_ref[l])
    fs_ref[SUBLANES:SUBLANES + m, 0:K_W] = acc_k[:, :K_W]
    fs_ref[SUBLANES:SUBLANES + m, K_W:] = acc_v[:, :K_W]
    shifted = fs_ref[SUBLANES - 1:SUBLANES - 1 + m, :]
    kc_ref[...] = shifted[:, :K_W] + acc_k[:, K_W:]
    vc_ref[...] = shifted[:, K_W:] + acc_v[:, K_W:]
    fs_ref[SUBLANES - 1:SUBLANES, :] = fs_ref[SUBLANES - 1 + m:SUBLANES + m, :]


def _cmp_stream(page_table, cache_t, bdk, bdv):
    db, n_pages = page_table.shape
    page = cache_t.shape[2]
    per_page = page // CMP_STRIDE
    m = CMP_PAGES * per_page
    nch = n_pages * per_page
    nsteps = n_pages // CMP_PAGES

    def page_spec(k):
        return pl.BlockSpec((None, KV_W, page), lambda b, s, pt: (pt[b, s * CMP_PAGES + k], 0, 0))

    grid_spec = pltpu.PrefetchScalarGridSpec(
        num_scalar_prefetch=1,
        grid=(db, nsteps),
        in_specs=[page_spec(k) for k in range(CMP_PAGES)]
        + [pl.BlockSpec(bdk.shape, lambda b, s, pt: (0, 0, 0)), pl.BlockSpec(bdv.shape, lambda b, s, pt: (0, 0, 0))],
        out_specs=[pl.BlockSpec((None, m, K_W), lambda b, s, pt: (b, s, 0))] * 2,
        scratch_shapes=[pltpu.VMEM((SUBLANES + m, 2 * K_W), F32),
                        pltpu.VMEM((CMP_PAGES, KV_W // LANES, page, LANES), F32)],
    )
    return pl.pallas_call(
        functools.partial(_cmp_stream_kernel, m=m),
        grid_spec=grid_spec,
        out_shape=[jax.ShapeDtypeStruct((db, nch, K_W), F32)] * 2,
        compiler_params=_params(("parallel", "arbitrary")),
        name="cmp_stream",
    )(page_table, *([cache_t] * CMP_PAGES), bdk, bdv)


def _block_diag_q(q):
    lo = _iota((8, LANES), 1) < HEAD_DIM
    zero = jnp.zeros((8, LANES), BF16)
    blocks = []
    for r in range(GROUP):
        for g in range(N_KV):
            hh = g * GROUP + r
            pair = q[:, (hh // 2) * LANES:(hh // 2 + 1) * LANES]
            if hh % 2 != g % 2:
                pair = pltpu.roll(pair.astype(F32), HEAD_DIM, 1).astype(BF16)
            keep = jnp.where(lo, pair, zero) if g % 2 == 0 else jnp.where(lo, zero, pair)
            blocks.append(jnp.concatenate([keep, zero] if g < 2 else [zero, keep], axis=1))
    return jnp.concatenate(blocks, axis=0)


def _cmp_sample_kernel(kcs_ref, vcs_ref, q_ref, pek_ref, wk4_ref, pev_ref, wv4_ref, gk_ref, bd4_ref,
                       ex4_ref, cos_ref, sin_ref, ovt_ref, qbd_ref, oc_ref, pen_ref,
                       *, past_len, n_slc, n_top):
    nch = kcs_ref.shape[0]
    pos_k = _dot(pek_ref[...], wk4_ref[...])[0:1]
    pos_v = _dot(pev_ref[...], wv4_ref[...])[0:1]
    yk = _head_rmsnorm(kcs_ref[...] + pos_k, gk_ref[...], bd4_ref[...], ex4_ref[...])
    kcn = jnp.concatenate([_rope128(yk[:, c * LANES:(c + 1) * LANES], cos_ref[...], sin_ref[...])
                           for c in range(K_W // LANES)], axis=1).astype(BF16)
    vc = (vcs_ref[...] + pos_v).astype(BF16)
    qbd = _block_diag_q(q_ref[...])
    qbd_ref[...] = qbd

    s = _dot_nt(qbd, kcn)
    mi = _iota((QROWS, nch), 1)
    t_row = past_len + (_iota((QROWS, nch), 0) & 7)
    mask = jnp.logical_and(mi * CMP_STRIDE + (CMP_STRIDE - 1) <= t_row, mi >= 1)
    s = jnp.where(mask, s, NEG)
    mx = jnp.max(s, axis=1, keepdims=True)
    e = jnp.where(mask, jnp.exp(s - mx), 0.0)
    p = (e / jnp.maximum(jnp.sum(e, axis=1, keepdims=True), 1e-30)).astype(BF16)
    oc_ref[...] = _dot(p, vc)

    imp = _dot_nt(ovt_ref[...], p)
    imp = imp + pltpu.roll(imp, 32, 1) + pltpu.roll(imp, 64, 1) + pltpu.roll(imp, 96, 1)
    nsp = imp.shape[0]
    j = _iota((nsp, QROWS), 0)
    t_lane = past_len + (_iota((nsp, QROWS), 1) & 7)
    cur = t_lane >> 6
    forced = jnp.logical_or(j == 0, jnp.logical_or(j == cur, j == cur - 1))
    rank = jnp.where(forced, RANK_FORCED, jnp.where(j * SEL_BLOCK <= t_lane, imp, RANK_INVALID))
    rank = jnp.where(j < n_slc, rank, RANK_REMOVED)
    sel = _rank_select(rank, n_top)
    pen_t = jnp.where(sel, 0.0, PEN)
    for blk in range(nsp // LANES):
        pen_ref[blk] = pen_t[blk * LANES:(blk + 1) * LANES, :].T.astype(BF16)


def _cmp_sample(kcs, vcs, q, cw_tail, cos, sin, ovt, past_len, n_slc, n_top):
    db, nch, _ = kcs.shape
    nsp = ovt.shape[0]
    per_b = lambda shape: pl.BlockSpec((None,) + shape, lambda b: (b,) + (0,) * len(shape))
    return pl.pallas_call(
        functools.partial(_cmp_sample_kernel, past_len=past_len, n_slc=n_slc, n_top=n_top),
        grid=(db,),
        in_specs=[per_b((nch, K_W)), per_b((nch, K_W)), per_b((8, D_MODEL))]
        + [_const_spec(a, 1) for a in cw_tail] + [_const_spec(cos, 1), _const_spec(sin, 1), _const_spec(ovt, 1)],
        out_specs=[per_b((QROWS, K_W)), per_b((QROWS, K_W)), per_b((nsp // LANES, QROWS, LANES))],
        out_shape=[jax.ShapeDtypeStruct((db, QROWS, K_W), BF16),
                   jax.ShapeDtypeStruct((db, QROWS, K_W), F32),
                   jax.ShapeDtypeStruct((db, nsp // LANES, QROWS, LANES), BF16)],
        compiler_params=_params(("parallel",)),
        name="cmp_sample",
    )(kcs, vcs, q, *cw_tail, cos, sin, ovt)


def _slc_stream_kernel(pt_ref, *refs):
    pages = refs[:SLC_PAGES]
    qbd_ref, pen_ref, m_out, l_out, acc_out, m_ref, l_ref, acc_ref = refs[SLC_PAGES:]
    s = pl.program_id(1)
    page = pages[0].shape[1]
    nk = SLC_PAGES * page
    blocks_per_step = nk // SEL_BLOCK

    @pl.when(s == 0)
    def _():
        m_ref[...] = jnp.full(m_ref.shape, NEG, F32)
        l_ref[...] = jnp.zeros(l_ref.shape, F32)
        acc_ref[...] = jnp.zeros(acc_ref.shape, F32)

    k_t = jnp.concatenate([pg[0:K_W, :] for pg in pages], axis=1).astype(BF16)
    v_t = jnp.concatenate([pg[K_W:, :] for pg in pages], axis=1).astype(BF16)
    steps_per_lane_block = LANES // blocks_per_step
    pen = pen_ref[s // steps_per_lane_block]
    jrow = _iota((LANES, nk), 0)
    blk = (s % steps_per_lane_block) * blocks_per_step + (_iota((LANES, nk), 1) >> 6)
    expand = jnp.where(jrow == blk, 1.0, 0.0).astype(BF16)
    sc = _dot(qbd_ref[...], k_t) + _dot(pen, expand)
    m_old = m_ref[...]
    m_new = jnp.maximum(m_old, jnp.max(sc, axis=1, keepdims=True))
    alpha = jnp.exp(m_old - m_new)
    p = jnp.exp(sc - m_new)
    l_ref[...] = alpha * l_ref[...] + jnp.sum(p, axis=1, keepdims=True)
    acc_ref[...] = alpha * acc_ref[...] + _dot_nt(p.astype(BF16), v_t)
    m_ref[...] = m_new

    @pl.when(s == pl.num_programs(1) - 1)
    def _():
        m_out[...] = jnp.broadcast_to(m_ref[...], m_out.shape)
        l_out[...] = jnp.broadcast_to(l_ref[...], l_out.shape)
        acc_out[...] = acc_ref[...]


def _slc_stream(page_table, cache_t, qbd, pen):
    db, n_pages = page_table.shape
    page = cache_t.shape[2]
    nsteps = n_pages // SLC_PAGES

    def page_spec(k):
        return pl.BlockSpec((None, KV_W, page), lambda b, s, pt: (pt[b, s * SLC_PAGES + k], 0, 0))

    per_b = lambda shape: pl.BlockSpec((None,) + shape, lambda b, s, pt: (b,) + (0,) * len(shape))
    grid_spec = pltpu.PrefetchScalarGridSpec(
        num_scalar_prefetch=1,
        grid=(db, nsteps),
        in_specs=[page_spec(k) for k in range(SLC_PAGES)] + [per_b((QROWS, K_W)), per_b(pen.shape[1:])],
        out_specs=[per_b((QROWS, LANES)), per_b((QROWS, LANES)), per_b((QROWS, K_W))],
        scratch_shapes=[pltpu.VMEM((QROWS, 1), F32), pltpu.VMEM((QROWS, 1), F32), pltpu.VMEM((QROWS, K_W), F32)],
    )
    return pl.pallas_call(
        _slc_stream_kernel,
        grid_spec=grid_spec,
        out_shape=[jax.ShapeDtypeStruct((db, QROWS, LANES), F32), jax.ShapeDtypeStruct((db, QROWS, LANES), F32),
                   jax.ShapeDtypeStruct((db, QROWS, K_W), F32)],
        compiler_params=_params(("parallel", "arbitrary")),
        name="slc_stream",
    )(page_table, *([cache_t] * SLC_PAGES), qbd, pen)


def _finish_sample_kernel(m_ref, l_ref, acc_ref, oc_ref, qbd_ref, pen_ref, slc_new_ref, win_new_ref,
                          win_ref, gates_ref, o_ref, *, past_len, new_blk):
    qbd = qbd_ref[...]
    tok = _iota((QROWS, LANES), 0) & 7
    col = _iota((QROWS, LANES), 1)
    pad = jnp.zeros((LANES - 8, K_W), F32)

    kn = jnp.concatenate([slc_new_ref[:, 0:K_W], pad], axis=0).astype(BF16)
    vn = jnp.concatenate([slc_new_ref[:, K_W:], pad], axis=0).astype(BF16)
    pen_col = pen_ref[new_blk // LANES][:, new_blk % LANES:new_blk % LANES + 1].astype(F32)
    sn = jnp.where(col <= tok, _dot_nt(qbd, kn) + pen_col, NEG)
    m_old = m_ref[:, 0:1]
    m_new = jnp.maximum(m_old, jnp.max(sn, axis=1, keepdims=True))
    alpha = jnp.exp(m_old - m_new)
    pn = jnp.exp(sn - m_new)
    l_new = alpha * l_ref[:, 0:1] + jnp.sum(pn, axis=1, keepdims=True)
    o_slc = (alpha * acc_ref[...] + _dot(pn.astype(BF16), vn)) / l_new

    wlen = win_ref.shape[1]
    new_t = jnp.concatenate([win_new_ref[...], jnp.zeros((LANES - 8, KV_W), F32)], axis=0).T
    kw_t = jnp.concatenate([win_ref[0:K_W, :], new_t[0:K_W]], axis=1).astype(BF16)
    vw_t = jnp.concatenate([win_ref[K_W:, :], new_t[K_W:]], axis=1).astype(BF16)
    nkw = wlen + LANES
    idx = _iota((QROWS, nkw), 1)
    t_row = past_len + (_iota((QROWS, nkw), 0) & 7)
    kpos = past_len - wlen + idx
    dist = t_row - kpos
    wmask = jnp.logical_and(jnp.logical_and(dist >= 0, dist <= WINDOW),
                            jnp.logical_and(kpos >= 0, idx < wlen + 8))
    sw = jnp.where(wmask, _dot(qbd, kw_t), NEG)
    mw = jnp.max(sw, axis=1, keepdims=True)
    ew = jnp.where(wmask, jnp.exp(sw - mw), 0.0)
    pw = ew / jnp.maximum(jnp.sum(ew, axis=1, keepdims=True), 1e-30)
    o_win = _dot_nt(pw.astype(BF16), vw_t)

    o_cmp = oc_ref[...]
    gates = gates_ref[...]
    lo = _iota((8, LANES), 1) < HEAD_DIM
    for c in range(D_MODEL // LANES):
        g = c // 2
        halves = []
        for e in range(2):
            r = 2 * (c % 2) + e
            rows = slice(r * 32 + g * 8, r * 32 + g * 8 + 8)
            lanes = slice((g // 2) * LANES, (g // 2 + 1) * LANES)
            base = g * LANES + 3 * r
            blk = (gates[:, base:base + 1] * o_cmp[rows, lanes] + gates[:, base + 1:base + 2] * o_slc[rows, lanes]
                   + gates[:, base + 2:base + 3] * o_win[rows, lanes])
            if g % 2 != e:
                blk = pltpu.roll(blk, HEAD_DIM, 1)
            halves.append(blk)
        o_ref[:, c * LANES:(c + 1) * LANES] = jnp.where(lo, halves[0], halves[1]).astype(BF16)


def _finish_sample(m, l, acc, oc, qbd, pen, slc_new, win_new, win3, gates, past_len, new_blk):
    db = m.shape[0]
    per_b = lambda shape: pl.BlockSpec((None,) + shape, lambda b: (b,) + (0,) * len(shape))
    ins = (m, l, acc, oc, qbd, pen, slc_new, win_new, win3, gates)
    return pl.pallas_call(
        functools.partial(_finish_sample_kernel, past_len=past_len, new_blk=new_blk),
        grid=(db,),
        in_specs=[per_b(a.shape[1:]) for a in ins],
        out_specs=per_b((8, D_MODEL)),
        out_shape=jax.ShapeDtypeStruct((db, 8, D_MODEL), BF16),
        compiler_params=_params(("parallel",)),
        name="finish_sample",
    )(*ins)


def _cmp_weights(w_c):
    eye = jnp.eye(N_KV, dtype=F32)
    bd = jnp.einsum("gh,lde->lgdhe", eye, w_c).reshape(CMP_LEN, K_W, K_W)
    return jnp.concatenate([bd[:CMP_STRIDE], bd[CMP_STRIDE:]], axis=2).astype(BF16)


def _pos_operands(pe, w_c):
    pe_rows = jnp.tile(pe.reshape(1, CMP_LEN * HEAD_DIM), (SUBLANES, 1)).astype(BF16)
    w4 = jnp.tile(w_c.reshape(CMP_LEN * HEAD_DIM, HEAD_DIM), (1, N_KV)).astype(BF16)
    return pe_rows, w4


def _tile_gain(g, width):
    return jnp.tile(g, width // g.shape[0]).reshape(1, width).astype(F32)


def kernel(x_prompt, x_sample, cache_cmp_kv, cache_slc_kv, cache_win_kv, state_conv, state_ffn_conv,
           page_table, g_norm_mix, w_in, w_dw, b_dw, g_ln_conv, b_ln_conv, w_conv_out, g_q, g_k_cmp,
           g_k_slc, g_k_win, w_cmp_k, w_cmp_v, pe_cmp_k, pe_cmp_v, w_nsa_out, w_out, g_norm_ffn, w_up,
           w_ffn_dw, b_ffn_dw, w_down):
    bp, seq, _ = x_prompt.shape
    db, dseq, _ = x_sample.shape
    n_pool, page = cache_cmp_kv.shape[0], cache_cmp_kv.shape[1]
    n_pages = page_table.shape[1]
    past_len = n_pages * page
    wlen = cache_win_kv.shape[1]
    assert dseq == 8 and seq % QBLK == 0 and seq // SEL_BLOCK <= LANES
    assert past_len % SEL_BLOCK == 0 and wlen == WINDOW and past_len >= WINDOW
    assert n_pages % CMP_PAGES == 0 and n_pages % SLC_PAGES == 0 and page % CMP_STRIDE == 0
    assert LANES % (SLC_PAGES * page // SEL_BLOCK) == 0

    offs = np.cumsum((2 * C_CONV, N_HEADS * HEAD_DIM, 3 * KV_W, 3 * N_HEADS))
    wu = w_in[:, :offs[0]].astype(BF16)
    wq = w_in[:, offs[0]:offs[1]].astype(BF16)
    wkv = w_in[:, offs[1]:offs[2]].astype(BF16)
    wgn = jnp.pad(w_in[:, offs[2]:offs[3]].reshape(D_MODEL, N_KV, 3 * GROUP),
                  ((0, 0), (0, 0), (0, LANES - 3 * GROUP))).reshape(D_MODEL, N_KV * LANES).astype(BF16)
    wgm = w_in[:, offs[3]:].astype(BF16)
    bd16, ex16 = _head_sum_mats(D_MODEL)
    bd4, ex4 = _head_sum_mats(K_W)
    in_wts = (g_norm_mix.reshape(1, D_MODEL), wu, wq, wkv, wgn, wgm, _tile_gain(g_q, D_MODEL),
              _tile_gain(g_k_slc, K_W), _tile_gain(g_k_win, K_W), bd16, ex16, bd4, ex4)
    wdw = jnp.pad(w_dw, ((0, 1), (0, 0)))
    conv_wts = (wdw, b_dw.reshape(1, C_CONV), g_ln_conv.reshape(1, C_CONV), b_ln_conv.reshape(1, C_CONV),
                w_conv_out.astype(BF16))
    wno = w_nsa_out.astype(BF16)
    wout = w_out.astype(BF16)
    gffn = g_norm_ffn.reshape(1, D_MODEL)
    wup = w_up.reshape(D_MODEL, 2 * FFN_NC, FFN_CW).transpose(1, 0, 2).astype(BF16)
    wdn = w_down.reshape(FFN_NC, FFN_CW, D_MODEL).astype(BF16)
    wfd = jnp.concatenate([w_ffn_dw, b_ffn_dw[None, :], jnp.zeros((4, 2 * D_FF), F32)], axis=0)
    wfd = wfd.reshape(SUBLANES, 2 * FFN_NC, FFN_CW).transpose(1, 0, 2)
    bdk = _cmp_weights(w_cmp_k)
    bdv = _cmp_weights(w_cmp_v)
    pek, wk4 = _pos_operands(pe_cmp_k, w_cmp_k)
    pev, wv4 = _pos_operands(pe_cmp_v, w_cmp_v)
    gkc = _tile_gain(g_k_cmp, K_W)
    cw_tail = (pek, wk4, pev, wv4, gkc, bd4, ex4)

    cos_p, sin_p = _rope_tables(jnp.arange(seq))
    (glu, q, cmp_rows, slc_rows, win_rows, gn, gm, k2s, v2s, k2w, v2w) = _inproj(
        x_prompt, cos_p, sin_p, in_wts, True)
    zeros_halo = jnp.zeros((bp, CONV_HALO, C_CONV), F32)
    a_out, conv_tail = _conv_module(glu, glu, zeros_halo, *conv_wts, tt=128, rb=32)
    nch = seq // CMP_STRIDE
    n_cmp = nch - 1
    n_slc = seq // SEL_BLOCK
    cos_c, sin_c = _rope_tables(jnp.arange(nch) * CMP_STRIDE + (CMP_LEN - 1))
    kc2, vc2 = _cmp_prompt(cmp_rows.reshape(bp, nch, CMP_STRIDE * KV_W), (bdk, bdv) + cw_tail, cos_c, sin_c)
    e_tab = jnp.asarray((np.arange(seq)[:, None] // SEL_BLOCK == np.arange(LANES)[None, :]), BF16)
    ov = jnp.asarray(_overlap_t(LANES, nch, 0, n_cmp).T * (np.arange(LANES) < n_slc), BF16)
    gex_np = np.zeros((2 * LANES, 3 * GROUP * LANES), np.float32)
    for k in range(3 * GROUP):
        gex_np[k, k * LANES:(k + 1) * LANES] = 1.0
        gex_np[LANES + k, k * LANES:(k + 1) * LANES] = 1.0
    o_p = _attn_prompt(q, gn, kc2, vc2, k2s, v2s, k2w, v2w, e_tab, ov, jnp.asarray(gex_np, BF16),
                       n_cmp, n_slc, min(N_SEL, n_slc))
    n_tok = bp * seq
    x1 = _merge(o_p.reshape(n_tok, D_MODEL), a_out.reshape(n_tok, D_MODEL), gm.reshape(n_tok, 2 * D_MODEL),
                x_prompt.reshape(n_tok, D_MODEL), wno, wout, 512)
    zeros_fix = jnp.zeros((bp, SUBLANES, 2 * D_FF), F32)
    y_p, ffn_tail = _ffn(x1.reshape(bp, seq, D_MODEL), gffn, wup, wfd, wdn, zeros_fix, zeros_fix, 512, 512)

    kv5 = lambda rows, b, t: rows.reshape(b, t, 2, N_KV, HEAD_DIM)
    out_p = (y_p, kv5(cmp_rows, bp, seq), kv5(slc_rows, bp, seq),
             kv5(win_rows[:, seq - min(WINDOW, seq):], bp, min(WINDOW, seq)),
             conv_tail[:, CONV_HALO - (CONV_K - 1):], ffn_tail[:, :FFN_CONV_K - 1])

    n_s = db * dseq
    pos_s = past_len + (jnp.arange(QBLK) % dseq)
    cos_s, sin_s = _rope_tables(pos_s)
    xs_pad = jnp.pad(x_sample.reshape(1, n_s, D_MODEL), ((0, 0), (0, QBLK - n_s), (0, 0))) if n_s < QBLK \
        else x_sample.reshape(1, n_s, D_MODEL)
    assert xs_pad.shape[1] == QBLK
    (glu_s, q_s, cmp_s, slc_s, win_s, gn_s, gm_s) = [a[0, :n_s] for a in _inproj(xs_pad, cos_s, sin_s, in_wts, False)]
    st32 = jnp.pad(state_conv, ((0, 0), (CONV_HALO - (CONV_K - 1), 0), (0, 0)))
    a_s, conv_tail_s = _conv_module(glu_s.reshape(db, dseq, C_CONV), st32, st32, *conv_wts, tt=dseq, rb=dseq)

    pos_minor = lambda c: c.transpose(0, 2, 3, 4, 1).reshape(c.shape[0], KV_W, c.shape[1])
    cache_cmp_t = pos_minor(cache_cmp_kv)
    cache_slc_t = pos_minor(cache_slc_kv)
    kcs, vcs = _cmp_stream(page_table, cache_cmp_t, bdk, bdv)
    nch_s = past_len // CMP_STRIDE
    n_cmp_s = -(-(past_len + dseq) // CMP_STRIDE) - 1
    n_slc_s = -(-(past_len + dseq) // SEL_BLOCK)
    nsp = -(-n_slc_s // LANES) * LANES
    cos_cs, sin_cs = _rope_tables(jnp.arange(nch_s) * CMP_STRIDE + (CMP_STRIDE - 1))
    ovt_s = jnp.asarray(_overlap_t(nsp, nch_s, 1, n_cmp_s), BF16)
    qbd, oc_s, pen_s = _cmp_sample(kcs, vcs, q_s.reshape(db, dseq, D_MODEL), cw_tail, cos_cs, sin_cs, ovt_s,
                                   past_len, n_slc_s, min(N_SEL, n_slc_s))
    m_s, l_s, acc_s = _slc_stream(page_table, cache_slc_t, qbd, pen_s)
    win3 = cache_win_kv.reshape(db, wlen, KV_W)
    o_s = _finish_sample(m_s, l_s, acc_s, oc_s, qbd, pen_s, slc_s.reshape(db, dseq, KV_W),
                         win_s.reshape(db, dseq, KV_W), pos_minor(cache_win_kv),
                         gn_s.reshape(db, dseq, N_KV * LANES), past_len, past_len // SEL_BLOCK)
    x1_s = _merge(o_s.reshape(n_s, D_MODEL), a_s.reshape(n_s, D_MODEL), gm_s, x_sample.reshape(n_s, D_MODEL),
                  wno, wout, n_s)
    z1 = jnp.zeros((db, dseq - 1, 2 * D_FF), F32)
    fix1 = jnp.concatenate([state_ffn_conv[:, 1:2], z1], axis=1).reshape(1, n_s, 2 * D_FF)
    fix2 = jnp.concatenate([state_ffn_conv, z1[:, 1:]], axis=1).reshape(1, n_s, 2 * D_FF)
    y_s, up_s = _ffn(x1_s.reshape(1, n_s, D_MODEL), gffn, wup, wfd, wdn, fix1, fix2, n_s, dseq)

    win_all = jnp.concatenate([win3, win_s.reshape(db, dseq, KV_W)], axis=1)
    keep = min(WINDOW, past_len + dseq)
    out_s = (y_s.reshape(db, dseq, D_MODEL), kv5(cmp_s, db, dseq), kv5(slc_s, db, dseq),
             kv5(win_all[:, wlen + dseq - keep:], db, keep),
             conv_tail_s[:, CONV_HALO - (CONV_K - 1):],
             up_s.reshape(db, dseq, 2 * D_FF)[:, dseq - (FFN_CONV_K - 1):])
    return (out_p[0], out_s[0]) + out_p[1:] + out_s[1:]
```

```python
import functools

import numpy as np
import jax
import jax.numpy as jnp
from jax import lax
from jax.experimental import pallas as pl
from jax.experimental.pallas import tpu as pltpu

F32 = jnp.float32
BF16 = jnp.bfloat16

D_MODEL = 1024
N_HEADS = 16
HEAD_DIM = 64
N_KV = 4
GROUP = N_HEADS // N_KV
CMP_STRIDE = 16
CMP_LEN = 2 * CMP_STRIDE
SEL_BLOCK = 64
N_SEL = 16
WINDOW = 512
C_CONV = D_MODEL // 2
CONV_K = 31
FFN_CONV_K = 3
D_FF = 2816
ROPE_THETA = 10000.0
EPS = 1e-6
KV_W = 2 * N_KV * HEAD_DIM
K_W = N_KV * HEAD_DIM

LANES = 128
SUBLANES = 8
QBLK = WINDOW
TOK_TILE = 256
CONV_HALO = 32
VMEM_LIMIT = 56 * 1024 * 1024
FFN_CW = 256
FFN_NC = D_FF // FFN_CW
CMP_PAGES = 16
SLC_PAGES = 8
QROWS = GROUP * N_KV * 8

NEG = -1e30
PEN = -1e9
RANK_FORCED = 1e30
RANK_INVALID = -1e30
RANK_REMOVED = -2e30


def _dot(a, b):
    return jnp.dot(a, b, preferred_element_type=F32)


def _dot_nt(a, b):
    return lax.dot_general(a, b, (((1,), (1,)), ((), ())), preferred_element_type=F32)


def _sigmoid(x):
    return 1.0 / (1.0 + jnp.exp(-x))


def _iota(shape, dim):
    return lax.broadcasted_iota(jnp.int32, shape, dim)


def _params(sem):
    return pltpu.CompilerParams(dimension_semantics=sem, vmem_limit_bytes=VMEM_LIMIT)


def _const_spec(a, ngrid):
    nd = a.ndim
    if ngrid == 1:
        return pl.BlockSpec(a.shape, lambda i: (0,) * nd)
    if ngrid == 2:
        return pl.BlockSpec(a.shape, lambda i, j: (0,) * nd)
    return pl.BlockSpec(a.shape, lambda i, j, k: (0,) * nd)


def _rope_tables(pos):
    half = HEAD_DIM // 2
    inv = ROPE_THETA ** (-jnp.arange(half, dtype=F32) * (2.0 / HEAD_DIM))
    ang = pos.astype(F32)[:, None] * inv[None, :]
    cos, sin = jnp.cos(ang), jnp.sin(ang)
    return jnp.tile(cos, (1, 4)), jnp.tile(jnp.concatenate([-sin, sin], axis=1), (1, 2))


def _head_sum_mats(width):
    heads = width // HEAD_DIM
    lane_head = np.arange(width) // HEAD_DIM
    bd = (lane_head[:, None] == np.arange(LANES)[None, :]).astype(np.float32)
    ex = np.zeros((2 * LANES, width), np.float32)
    ex[:heads] = (np.arange(heads)[:, None] == lane_head[None, :])
    ex[LANES:LANES + heads] = ex[:heads]
    return jnp.asarray(bd, BF16), jnp.asarray(ex, BF16)


def _overlap_t(n_slc_rows, n_cols, col_shift, n_cmp):
    n = np.arange(n_cols) - col_shift
    j = np.arange(n_slc_rows)
    cs = n * CMP_STRIDE
    ss = j * SEL_BLOCK
    ov = (cs[None, :] < ss[:, None] + SEL_BLOCK) & (cs[None, :] + CMP_LEN > ss[:, None])
    ov &= (n[None, :] >= 0) & (n[None, :] < n_cmp)
    return ov.astype(np.float32)


def _head_rmsnorm(x, gain, bd, ex):
    ssum = _dot((x * x).astype(BF16), bd)
    r = lax.rsqrt(ssum * (1.0 / HEAD_DIM) + EPS)
    r_hi = r.astype(BF16)
    r_lo = (r - r_hi.astype(F32)).astype(BF16)
    rfull = _dot(jnp.concatenate([r_hi, r_lo], axis=1), ex)
    return x * rfull * gain


def _rope128(y, cos, sin):
    lane = _iota(y.shape, 1)
    first = (lane & (HEAD_DIM - 1)) < (HEAD_DIM // 2)
    rot = jnp.where(first, pltpu.roll(y, LANES - HEAD_DIM // 2, 1), pltpu.roll(y, HEAD_DIM // 2, 1))
    return y * cos + rot * sin


def _dup_heads(chunk):
    lo = _iota(chunk.shape, 1) < HEAD_DIM
    swapped = pltpu.roll(chunk, HEAD_DIM, 1)
    return jnp.where(lo, chunk, swapped), jnp.where(lo, swapped, chunk)


def _rank_select(val, n_rounds):
    rows = val.shape[0]
    j = _iota(val.shape, 0).astype(F32)
    sel = jnp.zeros(val.shape, F32)
    for _ in range(n_rounds):
        mx = jnp.max(val, axis=0, keepdims=True)
        idx = jnp.min(jnp.where(val == mx, j, float(rows)), axis=0, keepdims=True)
        hit = j == idx
        sel = jnp.where(hit, 1.0, sel)
        val = jnp.where(hit, RANK_REMOVED, val)
    return sel > 0.5


def _inproj_kernel(x_ref, gmix_ref, wu_ref, wq_ref, wkv_ref, wgn_ref, wgm_ref, gq_ref, gks_ref,
                   gkw_ref, bd16_ref, ex16_ref, bd4_ref, ex4_ref, cos_ref, sin_ref,
                   glu_ref, q_ref, cmp_ref, slc_ref, win_ref, gn_ref, gm_ref, *attn_refs):
    x = x_ref[...]
    ms = jnp.mean(x * x, axis=-1, keepdims=True)
    h = (x * lax.rsqrt(ms + EPS) * gmix_ref[...]).astype(BF16)
    cos = cos_ref[...]
    sin = sin_ref[...]

    u = _dot(h, wu_ref[...])
    glu_ref[...] = u[:, :C_CONV] * _sigmoid(u[:, C_CONV:])

    yq = _head_rmsnorm(_dot(h, wq_ref[...]), gq_ref[...], bd16_ref[...], ex16_ref[...])
    scale = HEAD_DIM ** -0.5
    for c in range(D_MODEL // LANES):
        sl = slice(c * LANES, (c + 1) * LANES)
        q_ref[:, sl] = (_rope128(yq[:, sl], cos, sin) * scale).astype(BF16)

    zkv = _dot(h, wkv_ref[...])
    cmp_ref[...] = zkv[:, :KV_W]

    def kv_branch(z, gain_ref, rows_ref, k2_ref, v2_ref):
        yk = _head_rmsnorm(z[:, :K_W], gain_ref[...], bd4_ref[...], ex4_ref[...])
        v = z[:, K_W:]
        rows_ref[:, K_W:] = v
        for c in range(K_W // LANES):
            sl = slice(c * LANES, (c + 1) * LANES)
            kr = _rope128(yk[:, sl], cos, sin)
            rows_ref[:, sl] = kr
            if k2_ref is not None:
                for ref, chunk in ((k2_ref, kr), (v2_ref, v[:, sl])):
                    a2, b2 = _dup_heads(chunk)
                    ref[2 * c] = a2.astype(BF16)
                    ref[2 * c + 1] = b2.astype(BF16)

    if attn_refs:
        k2s_ref, v2s_ref, k2w_ref, v2w_ref = attn_refs
    else:
        k2s_ref = v2s_ref = k2w_ref = v2w_ref = None
    kv_branch(zkv[:, KV_W:2 * KV_W], gks_ref, slc_ref, k2s_ref, v2s_ref)
    kv_branch(zkv[:, 2 * KV_W:], gkw_ref, win_ref, k2w_ref, v2w_ref)

    gn_ref[...] = _sigmoid(_dot(h, wgn_ref[...]))
    gm_ref[...] = _sigmoid(_dot(h, wgm_ref[...]))


def _inproj(x, cos, sin, wts, attn_layouts):
    B, T, _ = x.shape
    tm = TOK_TILE
    assert T % tm == 0
    nt = T // tm
    tok = lambda w: pl.BlockSpec((None, tm, w), lambda b, t: (b, t, 0))
    tab = pl.BlockSpec((tm, LANES), lambda b, t: (t, 0))
    in_specs = [tok(D_MODEL)] + [_const_spec(a, 2) for a in wts] + [tab, tab]
    out_shape = [
        jax.ShapeDtypeStruct((B, T, C_CONV), F32),
        jax.ShapeDtypeStruct((B, T, D_MODEL), BF16),
        jax.ShapeDtypeStruct((B, T, KV_W), F32),
        jax.ShapeDtypeStruct((B, T, KV_W), F32),
        jax.ShapeDtypeStruct((B, T, KV_W), F32),
        jax.ShapeDtypeStruct((B, T, N_KV * LANES), F32),
        jax.ShapeDtypeStruct((B, T, 2 * D_MODEL), F32),
    ]
    out_specs = [tok(C_CONV), tok(D_MODEL), tok(KV_W), tok(KV_W), tok(KV_W), tok(N_KV * LANES),
                 tok(2 * D_MODEL)]
    if attn_layouts:
        dup = jax.ShapeDtypeStruct((B, N_KV, T, LANES), BF16)
        dup_spec = pl.BlockSpec((None, N_KV, tm, LANES), lambda b, t: (b, 0, t, 0))
        out_shape += [dup] * 4
        out_specs += [dup_spec] * 4
    return pl.pallas_call(
        _inproj_kernel,
        grid=(B, nt),
        in_specs=in_specs,
        out_specs=out_specs,
        out_shape=out_shape,
        compiler_params=_params(("parallel", "parallel")),
        name="inproj",
    )(x, *wts, cos, sin)


def _conv_kernel(glu_ref, halo_ref, st_ref, wdw_ref, bdw_ref, gln_ref, bln_ref, wco_ref,
                 a_ref, cst_ref, xc_ref, s_ref, *, tt, rb):
    ti = pl.program_id(1)

    @pl.when(ti == 0)
    def _():
        xc_ref[0:CONV_HALO, :] = st_ref[...]

    @pl.when(ti > 0)
    def _():
        xc_ref[0:CONV_HALO, :] = halo_ref[...]

    xc_ref[CONV_HALO:CONV_HALO + tt, :] = glu_ref[...]
    xc_ref[CONV_HALO + tt:CONV_HALO + tt + SUBLANES, :] = jnp.zeros((SUBLANES, C_CONV), F32)
    first = CONV_HALO - (CONV_K - 1)
    for r0 in range(0, tt, rb):
        acc = jnp.broadcast_to(bdw_ref[...], (rb, C_CONV))
        for phi in range(SUBLANES):
            y = None
            for u in range(first, first + CONV_K):
                if u % SUBLANES == phi:
                    term = wdw_ref[u - first:u - first + 1, :] * xc_ref[r0 + u - phi:r0 + u - phi + rb + SUBLANES, :]
                    y = term if y is None else y + term
            acc = acc + y[phi:phi + rb]
        mu = jnp.mean(acc, axis=-1, keepdims=True)
        d = acc - mu
        var = jnp.mean(d * d, axis=-1, keepdims=True)
        y = d * lax.rsqrt(var + EPS) * gln_ref[...] + bln_ref[...]
        s_ref[r0:r0 + rb, :] = y * _sigmoid(y)
    a_ref[...] = _dot(s_ref[...].astype(BF16), wco_ref[...])
    cst_ref[...] = xc_ref[tt:tt + CONV_HALO, :]


def _conv_module(glu, halo_src, state32, wdw, bdw, gln, bln, wco, tt, rb):
    B, T, _ = glu.shape
    nt = T // tt
    hb = max(tt // CONV_HALO, 1)
    return pl.pallas_call(
        functools.partial(_conv_kernel, tt=tt, rb=rb),
        grid=(B, nt),
        in_specs=[
            pl.BlockSpec((None, tt, C_CONV), lambda b, t: (b, t, 0)),
            pl.BlockSpec((None, CONV_HALO, C_CONV), lambda b, t: (b, jnp.maximum(t * hb - 1, 0), 0)),
            pl.BlockSpec((None, CONV_HALO, C_CONV), lambda b, t: (b, 0, 0)),
            _const_spec(wdw, 2), _const_spec(bdw, 2), _const_spec(gln, 2), _const_spec(bln, 2),
            _const_spec(wco, 2),
        ],
        out_specs=[
            pl.BlockSpec((None, tt, D_MODEL), lambda b, t: (b, t, 0)),
            pl.BlockSpec((None, CONV_HALO, C_CONV), lambda b, t: (b, 0, 0)),
        ],
        out_shape=[
            jax.ShapeDtypeStruct((B, T, D_MODEL), F32),
            jax.ShapeDtypeStruct((B, CONV_HALO, C_CONV), F32),
        ],
        scratch_shapes=[pltpu.VMEM((CONV_HALO + tt + SUBLANES, C_CONV), F32), pltpu.VMEM((tt, C_CONV), F32)],
        compiler_params=_params(("parallel", "arbitrary")),
        name="conv_module",
    )(glu, halo_src, state32, wdw, bdw, gln, bln, wco)


def _merge_kernel(o_ref, a_ref, gm_ref, x_ref, wno_ref, wout_ref, x1_ref):
    b_out = _dot(o_ref[...], wno_ref[...])
    gm = gm_ref[...]
    m = gm[:, :D_MODEL] * a_ref[...] + gm[:, D_MODEL:] * b_out
    x1_ref[...] = x_ref[...] + _dot(m.astype(BF16), wout_ref[...])


def _merge(o, a_out, gm, x, wno, wout, tm):
    N = x.shape[0]
    row = lambda w: pl.BlockSpec((tm, w), lambda i: (i, 0))
    return pl.pallas_call(
        _merge_kernel,
        grid=(N // tm,),
        in_specs=[row(D_MODEL), row(D_MODEL), row(2 * D_MODEL), row(D_MODEL),
                  _const_spec(wno, 1), _const_spec(wout, 1)],
        out_specs=row(D_MODEL),
        out_shape=jax.ShapeDtypeStruct((N, D_MODEL), F32),
        compiler_params=_params(("parallel",)),
        name="merge",
    )(o, a_out, gm, x, wno, wout)


def _ffn_kernel(x1_ref, gn_ref, wup_ref, wdw_ref, wdn_ref, fix1_ref, fix2_ref,
                y_ref, st_ref, prev_ref, acc_ref, *, tm, seg):
    streaming = seg == tm
    x1 = x1_ref[...]
    ms = jnp.mean(x1 * x1, axis=-1, keepdims=True)
    h = (x1 * lax.rsqrt(ms + EPS) * gn_ref[...]).astype(BF16)
    row = _iota((tm, FFN_CW), 0) & (seg - 1)

    if streaming:
        @pl.when(pl.program_id(1) == 0)
        def _():
            prev_ref[...] = fix2_ref[...]

    def conv3(c):
        cs = slice(c * FFN_CW, (c + 1) * FFN_CW)
        up = _dot(h, wup_ref[c])
        r1 = pltpu.roll(up, 1, 0)
        r2 = pltpu.roll(up, 2, 0)
        if streaming:
            p0 = prev_ref[0:1, cs]
            p1 = prev_ref[1:2, cs]
            s1 = jnp.where(row == 0, p1, r1)
            s2 = jnp.where(row == 0, p0, jnp.where(row == 1, p1, r2))
            prev_ref[0:2, cs] = up[tm - 2:tm, :]
        else:
            s1 = jnp.where(row == 0, fix1_ref[:, cs], r1)
            s2 = jnp.where(row < 2, fix2_ref[:, cs], r2)
            st_ref[:, cs] = up
        w = wdw_ref[c]
        return w[0:1] * s2 + w[1:2] * s1 + w[2:3] * up + w[3:4]

    acc_ref[...] = x1
    for c in range(FFN_NC):
        gate = conv3(c)
        val = conv3(FFN_NC + c)
        act = gate * _sigmoid(gate) * val
        acc_ref[...] += _dot(act.astype(BF16), wdn_ref[c])
    y_ref[...] = acc_ref[...]
    if streaming:
        st_ref[...] = prev_ref[...]


def _ffn(x1, gn, wup, wdw, wdn, fix1, fix2, tm, seg):
    B, T, _ = x1.shape
    nt = T // tm
    streaming = seg == tm
    tok = lambda w: pl.BlockSpec((None, tm, w), lambda b, t: (b, t, 0))
    if streaming:
        fix_specs = [pl.BlockSpec((None, SUBLANES, 2 * D_FF), lambda b, t: (b, 0, 0))] * 2
        st_shape = jax.ShapeDtypeStruct((B, SUBLANES, 2 * D_FF), F32)
        st_spec = pl.BlockSpec((None, SUBLANES, 2 * D_FF), lambda b, t: (b, 0, 0))
    else:
        fix_specs = [tok(2 * D_FF)] * 2
        st_shape = jax.ShapeDtypeStruct((B, T, 2 * D_FF), F32)
        st_spec = tok(2 * D_FF)
    return pl.pallas_call(
        functools.partial(_ffn_kernel, tm=tm, seg=seg),
        grid=(B, nt),
        in_specs=[tok(D_MODEL), _const_spec(gn, 2), _const_spec(wup, 2), _const_spec(wdw, 2),
                  _const_spec(wdn, 2)] + fix_specs,
        out_specs=[tok(D_MODEL), st_spec],
        out_shape=[jax.ShapeDtypeStruct((B, T, D_MODEL), F32), st_shape],
        scratch_shapes=[pltpu.VMEM((SUBLANES, 2 * D_FF), F32), pltpu.VMEM((tm, D_MODEL), F32)],
        compiler_params=_params(("parallel", "arbitrary")),
        name="conv_ffn",
    )(x1, gn, wup, wdw, wdn, fix1, fix2)


def _cmp_prompt_kernel(rows_ref, bdk_ref, bdv_ref, pek_ref, wk4_ref, pev_ref, wv4_ref, gk_ref,
                       bd4_ref, ex4_ref, cos_ref, sin_ref, kc2_ref, vc2_ref, *, nch):
    acc_k = jnp.zeros((nch, 2 * K_W), F32)
    acc_v = jnp.zeros((nch, 2 * K_W), F32)
    for l in range(CMP_STRIDE):
        acc_k += _dot(rows_ref[:, l * KV_W:l * KV_W + K_W].astype(BF16), bdk_ref[l])
        acc_v += _dot(rows_ref[:, l * KV_W + K_W:(l + 1) * KV_W].astype(BF16), bdv_ref[l])
    pos_k = _dot(pek_ref[...], wk4_ref[...])[0:1]
    pos_v = _dot(pev_ref[...], wv4_ref[...])[0:1]
    kc = acc_k[:, :K_W] + pltpu.roll(acc_k[:, K_W:], nch - 1, 0) + pos_k
    vc = acc_v[:, :K_W] + pltpu.roll(acc_v[:, K_W:], nch - 1, 0) + pos_v
    yk = _head_rmsnorm(kc, gk_ref[...], bd4_ref[...], ex4_ref[...])
    for c in range(K_W // LANES):
        sl = slice(c * LANES, (c + 1) * LANES)
        for ref, chunk in ((kc2_ref, _rope128(yk[:, sl], cos_ref[...], sin_ref[...])), (vc2_ref, vc[:, sl])):
            a2, b2 = _dup_heads(chunk)
            ref[2 * c] = a2.astype(BF16)
            ref[2 * c + 1] = b2.astype(BF16)


def _cmp_prompt(rows, cw, cos, sin):
    B, nch, _ = rows.shape
    return pl.pallas_call(
        functools.partial(_cmp_prompt_kernel, nch=nch),
        grid=(B,),
        in_specs=[pl.BlockSpec((None, nch, CMP_STRIDE * KV_W), lambda b: (b, 0, 0))] + [_const_spec(a, 1) for a in cw]
        + [_const_spec(cos, 1), _const_spec(sin, 1)],
        out_specs=[pl.BlockSpec((None, N_KV, nch, LANES), lambda b: (b, 0, 0, 0))] * 2,
        out_shape=[jax.ShapeDtypeStruct((B, N_KV, nch, LANES), BF16)] * 2,
        compiler_params=_params(("parallel",)),
        name="cmp_prompt",
    )(rows, *cw, cos, sin)


def _attn_prompt_kernel(q_ref, gates_ref, kc2_ref, vc2_ref, k2s_ref, v2s_ref, k2w_ref, v2w_ref,
                        e_ref, ov_ref, gex_ref, o_ref, m_ref, l_ref, acc_ref, *, n_cmp, n_slc, n_top):
    qi = pl.program_id(2)
    qs = qi * QBLK
    rows = GROUP * QBLK
    lo = _iota((QBLK, LANES), 1) < HEAD_DIM
    zero = jnp.zeros((QBLK, LANES), BF16)
    qm = []
    for p in range(GROUP // 2):
        qp = q_ref[:, p * LANES:(p + 1) * LANES]
        qm += [jnp.where(lo, qp, zero), jnp.where(lo, zero, qp)]
    q_all = jnp.concatenate(qm, axis=0)

    nch = kc2_ref.shape[0]
    n = _iota((rows, nch), 1)
    t_row = qs + (_iota((rows, nch), 0) & (QBLK - 1))
    cmask = jnp.logical_and(n * CMP_STRIDE + (CMP_LEN - 1) <= t_row, n < n_cmp)
    s = jnp.where(cmask, _dot_nt(q_all, kc2_ref[...]), NEG)
    mx = jnp.max(s, axis=1, keepdims=True)
    e = jnp.where(cmask, jnp.exp(s - mx), 0.0)
    p_cmp = (e / jnp.maximum(jnp.sum(e, axis=1, keepdims=True), 1e-30)).astype(BF16)
    o_cmp = _dot(p_cmp, vc2_ref[...])
    imp4 = _dot(p_cmp, ov_ref[...])
    imp = imp4[0:QBLK]
    for r in range(1, GROUP):
        imp = imp + imp4[r * QBLK:(r + 1) * QBLK]
    imp_t = imp.T[0:n_slc]
    j = _iota((n_slc, QBLK), 0)
    tq = qs + _iota((n_slc, QBLK), 1)
    cur = tq >> 6
    forced = jnp.logical_or(j == 0, jnp.logical_or(j == cur, j == cur - 1))
    rank = jnp.where(forced, RANK_FORCED, jnp.where(j * SEL_BLOCK <= tq, imp_t, RANK_INVALID))
    pen_t = jnp.where(_rank_select(rank, n_top), 0.0, PEN)
    if n_slc < LANES:
        pen_t = jnp.concatenate([pen_t, jnp.zeros((LANES - n_slc, QBLK), F32)], axis=0)
    pen = pen_t.T.astype(BF16)
    qa_all = jnp.concatenate([q_all, jnp.concatenate([pen] * GROUP, axis=0)], axis=1)

    kl = _iota((rows, QBLK), 1)
    tl = _iota((rows, QBLK), 0) & (QBLK - 1)

    def reset():
        m_ref[...] = jnp.full(m_ref.shape, NEG, F32)
        l_ref[...] = jnp.zeros(l_ref.shape, F32)
        acc_ref[...] = jnp.zeros(acc_ref.shape, F32)

    def attend(kblk, v2blk, q_rows, mode):
        sc = _dot_nt(q_rows, kblk)
        if mode == "causal":
            sc = jnp.where(kl <= tl, sc, NEG)
        elif mode == "band":
            sc = jnp.where(kl >= tl, sc, NEG)
        m_old = m_ref[...]
        m_new = jnp.maximum(m_old, jnp.max(sc, axis=1, keepdims=True))
        alpha = jnp.exp(m_old - m_new)
        p = jnp.exp(sc - jnp.concatenate([m_new] * (sc.shape[1] // LANES), axis=1))
        l_ref[...] = alpha * l_ref[...] + jnp.sum(p, axis=1, keepdims=True)
        acc_ref[...] = alpha * acc_ref[...] + _dot(p.astype(BF16), v2blk)
        m_ref[...] = m_new

    def blk(ref, kb):
        return ref[pl.ds(pl.multiple_of(kb * QBLK, QBLK), QBLK), :]

    def slc_keys(kb):
        return jnp.concatenate([blk(k2s_ref, kb), blk(e_ref, kb)], axis=1)

    reset()

    def slc_body(kb, carry):
        attend(slc_keys(kb), blk(v2s_ref, kb), qa_all, "none")
        return carry

    lax.fori_loop(0, qi, slc_body, 0)
    attend(slc_keys(qi), blk(v2s_ref, qi), qa_all, "causal")
    o_slc = acc_ref[...] / l_ref[...]

    reset()

    @pl.when(qi >= 1)
    def _():
        attend(blk(k2w_ref, qi - 1), blk(v2w_ref, qi - 1), q_all, "band")

    attend(blk(k2w_ref, qi), blk(v2w_ref, qi), q_all, "causal")
    o_win = acc_ref[...] / l_ref[...]

    gt = gates_ref[...]
    g_hi = gt.astype(BF16)
    g_lo = (gt - g_hi.astype(F32)).astype(BF16)
    gex = _dot(jnp.concatenate([g_hi, g_lo], axis=1), gex_ref[...])
    mixed = []
    for r in range(GROUP):
        rs = slice(r * QBLK, (r + 1) * QBLK)
        gl = lambda k: gex[:, (3 * r + k) * LANES:(3 * r + k + 1) * LANES]
        mixed.append(gl(0) * o_cmp[rs] + gl(1) * o_slc[rs] + gl(2) * o_win[rs])
    for p in range(GROUP // 2):
        o_ref[:, p * LANES:(p + 1) * LANES] = jnp.where(lo, mixed[2 * p], mixed[2 * p + 1]).astype(BF16)


def _attn_prompt(q, gates, kc2, vc2, k2s, v2s, k2w, v2w, e_tab, ov, gex, n_cmp, n_slc, n_top):
    B, T, _ = q.shape
    nq = T // QBLK
    nch = kc2.shape[2]
    rows = GROUP * QBLK
    per_bg = lambda n: pl.BlockSpec((None, None, n, LANES), lambda b, g, i: (b, g, 0, 0))
    return pl.pallas_call(
        functools.partial(_attn_prompt_kernel, n_cmp=n_cmp, n_slc=n_slc, n_top=n_top),
        grid=(B, N_KV, nq),
        in_specs=[
            pl.BlockSpec((None, QBLK, K_W), lambda b, g, i: (b, i, g)),
            pl.BlockSpec((None, QBLK, LANES), lambda b, g, i: (b, i, g)),
            per_bg(nch), per_bg(nch), per_bg(T), per_bg(T), per_bg(T), per_bg(T),
            _const_spec(e_tab, 3), _const_spec(ov, 3), _const_spec(gex, 3),
        ],
        out_specs=pl.BlockSpec((None, QBLK, K_W), lambda b, g, i: (b, i, g)),
        out_shape=jax.ShapeDtypeStruct((B, T, D_MODEL), BF16),
        scratch_shapes=[pltpu.VMEM((rows, LANES), F32), pltpu.VMEM((rows, LANES), F32),
                        pltpu.VMEM((rows, LANES), F32)],
        compiler_params=_params(("parallel", "parallel", "arbitrary")),
        name="attn_prompt",
    )(q, gates, kc2, vc2, k2s, v2s, k2w, v2w, e_tab, ov, gex)


def _cmp_stream_kernel(pt_ref, *refs, m):
    pages = refs[:CMP_PAGES]
    bdk_ref, bdv_ref, kc_ref, vc_ref, fs_ref, slab_ref = refs[CMP_PAGES:]
    s = pl.program_id(1)
    per_page = m // CMP_PAGES
    nslab = KV_W // LANES

    @pl.when(s == 0)
    def _():
        fs_ref[0:SUBLANES, :] = jnp.zeros((SUBLANES, 2 * K_W), F32)

    for k, pg in enumerate(pages):
        for c in range(nslab):
            slab_ref[k, c] = pg[c * LANES:(c + 1) * LANES, :].T
    def rows_of(c, l):
        return jnp.concatenate([slab_ref[k, c, pl.ds(l, per_page, stride=CMP_STRIDE), :]
                                for k in range(CMP_PAGES)], axis=0)

    def pair_rows(c, l):
        return jnp.concatenate([rows_of(c, l), rows_of(c, l + CMP_STRIDE // 2)], axis=1)

    acc_k = jnp.zeros((2 * m, K_W), F32)
    acc_v = jnp.zeros((2 * m, K_W), F32)
    for l in range(CMP_STRIDE // 2):
        xk = jnp.concatenate([pair_rows(0, l), pair_rows(1, l)], axis=0)
        xv = jnp.concatenate([pair_rows(2, l), pair_rows(3, l)], axis=0)
        acc_k += _dot(xk.astype(BF16), bdk_ref[l])
        acc_v += _dot(xv.astype(BF16), bdv_ref[l])
    half = K_W // 2
    first = jnp.concatenate([acc_k[:m, :half], acc_k[m:, :half], acc_v[:m, :half], acc_v[m:, :half]], axis=1)
    second_k = jnp.concatenate([acc_k[:m, half:], acc_k[m:, half:]], axis=1)
    second_v = jnp.concatenate([acc_v[:m, half:], acc_v[m:, half:]], axis=1)
    fs_ref[SUBLANES:SUBLANES + m, :] = first
    shifted = fs_ref[SUBLANES - 1:SUBLANES - 1 + m, :]
    kc_ref[...] = shifted[:, :K_W] + second_k
    vc_ref[...] = shifted[:, K_W:] + second_v
    fs_ref[SUBLANES - 1:SUBLANES, :] = fs_ref[SUBLANES - 1 + m:SUBLANES + m, :]


def _cmp_stream(page_table, cache_t, bdk, bdv):
    db, n_pages = page_table.shape
    page = cache_t.shape[2]
    per_page = page // CMP_STRIDE
    m = CMP_PAGES * per_page
    nch = n_pages * per_page
    nsteps = n_pages // CMP_PAGES

    def page_spec(k):
        return pl.BlockSpec((None, KV_W, page), lambda b, s, pt: (pt[b, s * CMP_PAGES + k], 0, 0))

    grid_spec = pltpu.PrefetchScalarGridSpec(
        num_scalar_prefetch=1,
        grid=(db, nsteps),
        in_specs=[page_spec(k) for k in range(CMP_PAGES)]
        + [pl.BlockSpec(bdk.shape, lambda b, s, pt: (0, 0, 0)), pl.BlockSpec(bdv.shape, lambda b, s, pt: (0, 0, 0))],
        out_specs=[pl.BlockSpec((None, m, K_W), lambda b, s, pt: (b, s, 0))] * 2,
        scratch_shapes=[pltpu.VMEM((SUBLANES + m, 2 * K_W), F32),
                        pltpu.VMEM((CMP_PAGES, KV_W // LANES, page, LANES), F32)],
    )
    return pl.pallas_call(
        functools.partial(_cmp_stream_kernel, m=m),
        grid_spec=grid_spec,
        out_shape=[jax.ShapeDtypeStruct((db, nch, K_W), F32)] * 2,
        compiler_params=_params(("parallel", "arbitrary")),
        name="cmp_stream",
    )(page_table, *([cache_t] * CMP_PAGES), bdk, bdv)


def _block_diag_q(q):
    lo = _iota((8, LANES), 1) < HEAD_DIM
    zero = jnp.zeros((8, LANES), BF16)
    blocks = []
    for r in range(GROUP):
        for g in range(N_KV):
            hh = g * GROUP + r
            pair = q[:, (hh // 2) * LANES:(hh // 2 + 1) * LANES]
            if hh % 2 != g % 2:
                pair = pltpu.roll(pair.astype(F32), HEAD_DIM, 1).astype(BF16)
            keep = jnp.where(lo, pair, zero) if g % 2 == 0 else jnp.where(lo, zero, pair)
            blocks.append(jnp.concatenate([keep, zero] if g < 2 else [zero, keep], axis=1))
    return jnp.concatenate(blocks, axis=0)


def _cmp_sample_kernel(kcs_ref, vcs_ref, q_ref, pek_ref, wk4_ref, pev_ref, wv4_ref, gk_ref, bd4_ref,
                       ex4_ref, cos_ref, sin_ref, ovt_ref, qbd_ref, oc_ref, pen_ref,
                       *, past_len, n_slc, n_top):
    nch = kcs_ref.shape[0]
    pos_k = _dot(pek_ref[...], wk4_ref[...])[0:1]
    pos_v = _dot(pev_ref[...], wv4_ref[...])[0:1]
    yk = _head_rmsnorm(kcs_ref[...] + pos_k, gk_ref[...], bd4_ref[...], ex4_ref[...])
    kcn = jnp.concatenate([_rope128(yk[:, c * LANES:(c + 1) * LANES], cos_ref[...], sin_ref[...])
                           for c in range(K_W // LANES)], axis=1).astype(BF16)
    vc = (vcs_ref[...] + pos_v).astype(BF16)
    qbd = _block_diag_q(q_ref[...])
    qbd_ref[...] = qbd

    s = _dot_nt(qbd, kcn)
    mi = _iota((QROWS, nch), 1)
    t_row = past_len + (_iota((QROWS, nch), 0) & 7)
    mask = jnp.logical_and(mi * CMP_STRIDE + (CMP_STRIDE - 1) <= t_row, mi >= 1)
    s = jnp.where(mask, s, NEG)
    mx = jnp.max(s, axis=1, keepdims=True)
    e = jnp.where(mask, jnp.exp(s - mx), 0.0)
    p = (e / jnp.maximum(jnp.sum(e, axis=1, keepdims=True), 1e-30)).astype(BF16)
    oc_ref[...] = _dot(p, vc)

    imp = _dot_nt(ovt_ref[...], p)
    imp = imp + pltpu.roll(imp, 32, 1) + pltpu.roll(imp, 64, 1) + pltpu.roll(imp, 96, 1)
    nsp = imp.shape[0]
    j = _iota((nsp, QROWS), 0)
    t_lane = past_len + (_iota((nsp, QROWS), 1) & 7)
    cur = t_lane >> 6
    forced = jnp.logical_or(j == 0, jnp.logical_or(j == cur, j == cur - 1))
    rank = jnp.where(forced, RANK_FORCED, jnp.where(j * SEL_BLOCK <= t_lane, imp, RANK_INVALID))
    rank = jnp.where(j < n_slc, rank, RANK_REMOVED)
    sel = _rank_select(rank, n_top)
    pen_t = jnp.where(sel, 0.0, PEN)
    for blk in range(nsp // LANES):
        pen_ref[blk] = pen_t[blk * LANES:(blk + 1) * LANES, :].T.astype(BF16)


def _cmp_sample(kcs, vcs, q, cw_tail, cos, sin, ovt, past_len, n_slc, n_top):
    db, nch, _ = kcs.shape
    nsp = ovt.shape[0]
    per_b = lambda shape: pl.BlockSpec((None,) + shape, lambda b: (b,) + (0,) * len(shape))
    return pl.pallas_call(
        functools.partial(_cmp_sample_kernel, past_len=past_len, n_slc=n_slc, n_top=n_top),
        grid=(db,),
        in_specs=[per_b((nch, K_W)), per_b((nch, K_W)), per_b((8, D_MODEL))]
        + [_const_spec(a, 1) for a in cw_tail] + [_const_spec(cos, 1), _const_spec(sin, 1), _const_spec(ovt, 1)],
        out_specs=[per_b((QROWS, K_W)), per_b((QROWS, K_W)), per_b((nsp // LANES, QROWS, LANES))],
        out_shape=[jax.ShapeDtypeStruct((db, QROWS, K_W), BF16),
                   jax.ShapeDtypeStruct((db, QROWS, K_W), F32),
                   jax.ShapeDtypeStruct((db, nsp // LANES, QROWS, LANES), BF16)],
        compiler_params=_params(("parallel",)),
        name="cmp_sample",
    )(kcs, vcs, q, *cw_tail, cos, sin, ovt)


def _slc_stream_kernel(pt_ref, *refs):
    pages = refs[:SLC_PAGES]
    qbd_ref, pen_ref, m_out, l_out, acc_out, m_ref, l_ref, acc_ref = refs[SLC_PAGES:]
    s = pl.program_id(1)
    page = pages[0].shape[1]
    nk = SLC_PAGES * page
    blocks_per_step = nk // SEL_BLOCK

    @pl.when(s == 0)
    def _():
        m_ref[...] = jnp.full(m_ref.shape, NEG, F32)
        l_ref[...] = jnp.zeros(l_ref.shape, F32)
        acc_ref[...] = jnp.zeros(acc_ref.shape, F32)

    k_t = jnp.concatenate([pg[0:K_W, :] for pg in pages], axis=1).astype(BF16)
    v_t = jnp.concatenate([pg[K_W:, :] for pg in pages], axis=1).astype(BF16)
    steps_per_lane_block = LANES // blocks_per_step
    pen = pen_ref[s // steps_per_lane_block]
    jrow = _iota((LANES, nk), 0)
    blk = (s % steps_per_lane_block) * blocks_per_step + (_iota((LANES, nk), 1) >> 6)
    expand = jnp.where(jrow == blk, 1.0, 0.0).astype(BF16)
    sc = _dot(qbd_ref[...], k_t) + _dot(pen, expand)
    m_old = m_ref[...]
    m_new = jnp.maximum(m_old, jnp.max(sc, axis=1, keepdims=True))
    alpha = jnp.exp(m_old - m_new)
    p = jnp.exp(sc - m_new)
    l_ref[...] = alpha * l_ref[...] + jnp.sum(p, axis=1, keepdims=True)
    acc_ref[...] = alpha * acc_ref[...] + _dot_nt(p.astype(BF16), v_t)
    m_ref[...] = m_new

    @pl.when(s == pl.num_programs(1) - 1)
    def _():
        m_out[...] = jnp.broadcast_to(m_ref[...], m_out.shape)
        l_out[...] = jnp.broadcast_to(l_ref[...], l_out.shape)
        acc_out[...] = acc_ref[...]


def _slc_stream(page_table, cache_t, qbd, pen):
    db, n_pages = page_table.shape
    page = cache_t.shape[2]
    nsteps = n_pages // SLC_PAGES

    def page_spec(k):
        return pl.BlockSpec((None, KV_W, page), lambda b, s, pt: (pt[b, s * SLC_PAGES + k], 0, 0))

    per_b = lambda shape: pl.BlockSpec((None,) + shape, lambda b, s, pt: (b,) + (0,) * len(shape))
    grid_spec = pltpu.PrefetchScalarGridSpec(
        num_scalar_prefetch=1,
        grid=(db, nsteps),
        in_specs=[page_spec(k) for k in range(SLC_PAGES)] + [per_b((QROWS, K_W)), per_b(pen.shape[1:])],
        out_specs=[per_b((QROWS, LANES)), per_b((QROWS, LANES)), per_b((QROWS, K_W))],
        scratch_shapes=[pltpu.VMEM((QROWS, 1), F32), pltpu.VMEM((QROWS, 1), F32), pltpu.VMEM((QROWS, K_W), F32)],
    )
    return pl.pallas_call(
        _slc_stream_kernel,
        grid_spec=grid_spec,
        out_shape=[jax.ShapeDtypeStruct((db, QROWS, LANES), F32), jax.ShapeDtypeStruct((db, QROWS, LANES), F32),
                   jax.ShapeDtypeStruct((db, QROWS, K_W), F32)],
        compiler_params=_params(("parallel", "arbitrary")),
        name="slc_stream",
    )(page_table, *([cache_t] * SLC_PAGES), qbd, pen)


def _finish_sample_kernel(m_ref, l_ref, acc_ref, oc_ref, qbd_ref, pen_ref, slc_new_ref, win_new_ref,
                          win_ref, gates_ref, o_ref, *, past_len, new_blk):
    qbd = qbd_ref[...]
    tok = _iota((QROWS, LANES), 0) & 7
    col = _iota((QROWS, LANES), 1)
    pad = jnp.zeros((LANES - 8, K_W), F32)

    kn = jnp.concatenate([slc_new_ref[:, 0:K_W], pad], axis=0).astype(BF16)
    vn = jnp.concatenate([slc_new_ref[:, K_W:], pad], axis=0).astype(BF16)
    pen_col = pen_ref[new_blk // LANES][:, new_blk % LANES:new_blk % LANES + 1].astype(F32)
    sn = jnp.where(col <= tok, _dot_nt(qbd, kn) + pen_col, NEG)
    m_old = m_ref[:, 0:1]
    m_new = jnp.maximum(m_old, jnp.max(sn, axis=1, keepdims=True))
    alpha = jnp.exp(m_old - m_new)
    pn = jnp.exp(sn - m_new)
    l_new = alpha * l_ref[:, 0:1] + jnp.sum(pn, axis=1, keepdims=True)
    o_slc = (alpha * acc_ref[...] + _dot(pn.astype(BF16), vn)) / l_new

    wlen = win_ref.shape[1]
    new_t = jnp.concatenate([win_new_ref[...], jnp.zeros((LANES - 8, KV_W), F32)], axis=0).T
    kw_t = jnp.concatenate([win_ref[0:K_W, :], new_t[0:K_W]], axis=1).astype(BF16)
    vw_t = jnp.concatenate([win_ref[K_W:, :], new_t[K_W:]], axis=1).astype(BF16)
    nkw = wlen + LANES
    idx = _iota((QROWS, nkw), 1)
    t_row = past_len + (_iota((QROWS, nkw), 0) & 7)
    kpos = past_len - wlen + idx
    dist = t_row - kpos
    wmask = jnp.logical_and(jnp.logical_and(dist >= 0, dist <= WINDOW),
                            jnp.logical_and(kpos >= 0, idx < wlen + 8))
    sw = jnp.where(wmask, _dot(qbd, kw_t), NEG)
    mw = jnp.max(sw, axis=1, keepdims=True)
    ew = jnp.where(wmask, jnp.exp(sw - mw), 0.0)
    pw = ew / jnp.maximum(jnp.sum(ew, axis=1, keepdims=True), 1e-30)
    o_win = _dot_nt(pw.astype(BF16), vw_t)

    o_cmp = oc_ref[...]
    gates = gates_ref[...]
    lo = _iota((8, LANES), 1) < HEAD_DIM
    for c in range(D_MODEL // LANES):
        g = c // 2
        halves = []
        for e in range(2):
            r = 2 * (c % 2) + e
            rows = slice(r * 32 + g * 8, r * 32 + g * 8 + 8)
            lanes = slice((g // 2) * LANES, (g // 2 + 1) * LANES)
            base = g * LANES + 3 * r
            blk = (gates[:, base:base + 1] * o_cmp[rows, lanes] + gates[:, base + 1:base + 2] * o_slc[rows, lanes]
                   + gates[:, base + 2:base + 3] * o_win[rows, lanes])
            if g % 2 != e:
                blk = pltpu.roll(blk, HEAD_DIM, 1)
            halves.append(blk)
        o_ref[:, c * LANES:(c + 1) * LANES] = jnp.where(lo, halves[0], halves[1]).astype(BF16)


def _finish_sample(m, l, acc, oc, qbd, pen, slc_new, win_new, win3, gates, past_len, new_blk):
    db = m.shape[0]
    per_b = lambda shape: pl.BlockSpec((None,) + shape, lambda b: (b,) + (0,) * len(shape))
    ins = (m, l, acc, oc, qbd, pen, slc_new, win_new, win3, gates)
    return pl.pallas_call(
        functools.partial(_finish_sample_kernel, past_len=past_len, new_blk=new_blk),
        grid=(db,),
        in_specs=[per_b(a.shape[1:]) for a in ins],
        out_specs=per_b((8, D_MODEL)),
        out_shape=jax.ShapeDtypeStruct((db, 8, D_MODEL), BF16),
        compiler_params=_params(("parallel",)),
        name="finish_sample",
    )(*ins)


def _cmp_weights(w_c):
    eye = jnp.eye(N_KV, dtype=F32)
    bd = jnp.einsum("gh,lde->lgdhe", eye, w_c).reshape(CMP_LEN, K_W, K_W)
    return jnp.concatenate([bd[:CMP_STRIDE], bd[CMP_STRIDE:]], axis=2).astype(BF16)


def _cmp_pair_weights(w_c):
    eye = jnp.eye(2, dtype=F32)
    bd = jnp.einsum("gh,lde->lgdhe", eye, w_c).reshape(CMP_LEN, LANES, LANES)
    fs = jnp.concatenate([bd[:CMP_STRIDE], bd[CMP_STRIDE:]], axis=2)
    return jnp.concatenate([fs[:CMP_STRIDE // 2], fs[CMP_STRIDE // 2:]], axis=1).astype(BF16)


def _pos_operands(pe, w_c):
    pe_rows = jnp.tile(pe.reshape(1, CMP_LEN * HEAD_DIM), (SUBLANES, 1)).astype(BF16)
    w4 = jnp.tile(w_c.reshape(CMP_LEN * HEAD_DIM, HEAD_DIM), (1, N_KV)).astype(BF16)
    return pe_rows, w4


def _tile_gain(g, width):
    return jnp.tile(g, width // g.shape[0]).reshape(1, width).astype(F32)


def kernel(x_prompt, x_sample, cache_cmp_kv, cache_slc_kv, cache_win_kv, state_conv, state_ffn_conv,
           page_table, g_norm_mix, w_in, w_dw, b_dw, g_ln_conv, b_ln_conv, w_conv_out, g_q, g_k_cmp,
           g_k_slc, g_k_win, w_cmp_k, w_cmp_v, pe_cmp_k, pe_cmp_v, w_nsa_out, w_out, g_norm_ffn, w_up,
           w_ffn_dw, b_ffn_dw, w_down):
    bp, seq, _ = x_prompt.shape
    db, dseq, _ = x_sample.shape
    n_pool, page = cache_cmp_kv.shape[0], cache_cmp_kv.shape[1]
    n_pages = page_table.shape[1]
    past_len = n_pages * page
    wlen = cache_win_kv.shape[1]
    assert dseq == 8 and seq % QBLK == 0 and seq // SEL_BLOCK <= LANES and QBLK == WINDOW
    assert past_len % SEL_BLOCK == 0 and wlen == WINDOW and past_len >= WINDOW
    assert n_pages % CMP_PAGES == 0 and n_pages % SLC_PAGES == 0 and page % CMP_STRIDE == 0
    assert LANES % (SLC_PAGES * page // SEL_BLOCK) == 0

    offs = np.cumsum((2 * C_CONV, N_HEADS * HEAD_DIM, 3 * KV_W, 3 * N_HEADS))
    wu = w_in[:, :offs[0]].astype(BF16)
    wq = w_in[:, offs[0]:offs[1]].astype(BF16)
    wkv = w_in[:, offs[1]:offs[2]].astype(BF16)
    wgn = jnp.pad(w_in[:, offs[2]:offs[3]].reshape(D_MODEL, N_KV, 3 * GROUP),
                  ((0, 0), (0, 0), (0, LANES - 3 * GROUP))).reshape(D_MODEL, N_KV * LANES).astype(BF16)
    wgm = w_in[:, offs[3]:].astype(BF16)
    bd16, ex16 = _head_sum_mats(D_MODEL)
    bd4, ex4 = _head_sum_mats(K_W)
    in_wts = (g_norm_mix.reshape(1, D_MODEL), wu, wq, wkv, wgn, wgm, _tile_gain(g_q, D_MODEL),
              _tile_gain(g_k_slc, K_W), _tile_gain(g_k_win, K_W), bd16, ex16, bd4, ex4)
    wdw = jnp.pad(w_dw, ((0, 1), (0, 0)))
    conv_wts = (wdw, b_dw.reshape(1, C_CONV), g_ln_conv.reshape(1, C_CONV), b_ln_conv.reshape(1, C_CONV),
                w_conv_out.astype(BF16))
    wno = w_nsa_out.astype(BF16)
    wout = w_out.astype(BF16)
    gffn = g_norm_ffn.reshape(1, D_MODEL)
    wup = w_up.reshape(D_MODEL, 2 * FFN_NC, FFN_CW).transpose(1, 0, 2).astype(BF16)
    wdn = w_down.reshape(FFN_NC, FFN_CW, D_MODEL).astype(BF16)
    wfd = jnp.concatenate([w_ffn_dw, b_ffn_dw[None, :], jnp.zeros((4, 2 * D_FF), F32)], axis=0)
    wfd = wfd.reshape(SUBLANES, 2 * FFN_NC, FFN_CW).transpose(1, 0, 2)
    bdk = _cmp_weights(w_cmp_k)
    bdv = _cmp_weights(w_cmp_v)
    pek, wk4 = _pos_operands(pe_cmp_k, w_cmp_k)
    pev, wv4 = _pos_operands(pe_cmp_v, w_cmp_v)
    gkc = _tile_gain(g_k_cmp, K_W)
    cw_tail = (pek, wk4, pev, wv4, gkc, bd4, ex4)

    cos_p, sin_p = _rope_tables(jnp.arange(seq))
    (glu, q, cmp_rows, slc_rows, win_rows, gn, gm, k2s, v2s, k2w, v2w) = _inproj(
        x_prompt, cos_p, sin_p, in_wts, True)
    zeros_halo = jnp.zeros((bp, CONV_HALO, C_CONV), F32)
    a_out, conv_tail = _conv_module(glu, glu, zeros_halo, *conv_wts, tt=128, rb=64)
    nch = seq // CMP_STRIDE
    n_cmp = nch - 1
    n_slc = seq // SEL_BLOCK
    cos_c, sin_c = _rope_tables(jnp.arange(nch) * CMP_STRIDE + (CMP_LEN - 1))
    kc2, vc2 = _cmp_prompt(cmp_rows.reshape(bp, nch, CMP_STRIDE * KV_W), (bdk, bdv) + cw_tail, cos_c, sin_c)
    e_tab = jnp.asarray((np.arange(seq)[:, None] // SEL_BLOCK == np.arange(LANES)[None, :]), BF16)
    ov = jnp.asarray(_overlap_t(LANES, nch, 0, n_cmp).T * (np.arange(LANES) < n_slc), BF16)
    gex_np = np.zeros((2 * LANES, 3 * GROUP * LANES), np.float32)
    for k in range(3 * GROUP):
        gex_np[k, k * LANES:(k + 1) * LANES] = 1.0
        gex_np[LANES + k, k * LANES:(k + 1) * LANES] = 1.0
    o_p = _attn_prompt(q, gn, kc2, vc2, k2s, v2s, k2w, v2w, e_tab, ov, jnp.asarray(gex_np, BF16),
                       n_cmp, n_slc, min(N_SEL, n_slc))
    n_tok = bp * seq
    x1 = _merge(o_p.reshape(n_tok, D_MODEL), a_out.reshape(n_tok, D_MODEL), gm.reshape(n_tok, 2 * D_MODEL),
                x_prompt.reshape(n_tok, D_MODEL), wno, wout, 512)
    zeros_fix = jnp.zeros((bp, SUBLANES, 2 * D_FF), F32)
    y_p, ffn_tail = _ffn(x1.reshape(bp, seq, D_MODEL), gffn, wup, wfd, wdn, zeros_fix, zeros_fix, 512, 512)

    kv5 = lambda rows, b, t: rows.reshape(b, t, 2, N_KV, HEAD_DIM)
    out_p = (y_p, kv5(cmp_rows, bp, seq), kv5(slc_rows, bp, seq),
             kv5(win_rows[:, seq - min(WINDOW, seq):], bp, min(WINDOW, seq)),
             conv_tail[:, CONV_HALO - (CONV_K - 1):], ffn_tail[:, :FFN_CONV_K - 1])

    n_s = db * dseq
    pos_s = past_len + (jnp.arange(TOK_TILE) % dseq)
    cos_s, sin_s = _rope_tables(pos_s)
    xs_pad = jnp.pad(x_sample.reshape(1, n_s, D_MODEL), ((0, 0), (0, TOK_TILE - n_s), (0, 0))) if n_s < TOK_TILE \
        else x_sample.reshape(1, n_s, D_MODEL)
    assert xs_pad.shape[1] == TOK_TILE
    (glu_s, q_s, cmp_s, slc_s, win_s, gn_s, gm_s) = [a[0, :n_s] for a in _inproj(xs_pad, cos_s, sin_s, in_wts, False)]
    st32 = jnp.pad(state_conv, ((0, 0), (CONV_HALO - (CONV_K - 1), 0), (0, 0)))
    a_s, conv_tail_s = _conv_module(glu_s.reshape(db, dseq, C_CONV), st32, st32, *conv_wts, tt=dseq, rb=dseq)

    pos_minor = lambda c: c.transpose(0, 2, 3, 4, 1).reshape(c.shape[0], KV_W, c.shape[1])
    cache_cmp_t = pos_minor(cache_cmp_kv)
    cache_slc_t = pos_minor(cache_slc_kv)
    kcs, vcs = _cmp_stream(page_table, cache_cmp_t, _cmp_pair_weights(w_cmp_k), _cmp_pair_weights(w_cmp_v))
    nch_s = past_len // CMP_STRIDE
    n_cmp_s = -(-(past_len + dseq) // CMP_STRIDE) - 1
    n_slc_s = -(-(past_len + dseq) // SEL_BLOCK)
    nsp = -(-n_slc_s // LANES) * LANES
    cos_cs, sin_cs = _rope_tables(jnp.arange(nch_s) * CMP_STRIDE + (CMP_STRIDE - 1))
    ovt_s = jnp.asarray(_overlap_t(nsp, nch_s, 1, n_cmp_s), BF16)
    qbd, oc_s, pen_s = _cmp_sample(kcs, vcs, q_s.reshape(db, dseq, D_MODEL), cw_tail, cos_cs, sin_cs, ovt_s,
                                   past_len, n_slc_s, min(N_SEL, n_slc_s))
    m_s, l_s, acc_s = _slc_stream(page_table, cache_slc_t, qbd, pen_s)
    win3 = cache_win_kv.reshape(db, wlen, KV_W)
    o_s = _finish_sample(m_s, l_s, acc_s, oc_s, qbd, pen_s, slc_s.reshape(db, dseq, KV_W),
                         win_s.reshape(db, dseq, KV_W), pos_minor(cache_win_kv),
                         gn_s.reshape(db, dseq, N_KV * LANES), past_len, past_len // SEL_BLOCK)
    x1_s = _merge(o_s.reshape(n_s, D_MODEL), a_s.reshape(n_s, D_MODEL), gm_s, x_sample.reshape(n_s, D_MODEL),
                  wno, wout, n_s)
    z1 = jnp.zeros((db, dseq - 1, 2 * D_FF), F32)
    fix1 = jnp.concatenate([state_ffn_conv[:, 1:2], z1], axis=1).reshape(1, n_s, 2 * D_FF)
    fix2 = jnp.concatenate([state_ffn_conv, z1[:, 1:]], axis=1).reshape(1, n_s, 2 * D_FF)
    y_s, up_s = _ffn(x1_s.reshape(1, n_s, D_MODEL), gffn, wup, wfd, wdn, fix1, fix2, n_s, dseq)

    win_all = jnp.concatenate([win3, win_s.reshape(db, dseq, KV_W)], axis=1)
    keep = min(WINDOW, past_len + dseq)
    out_s = (y_s.reshape(db, dseq, D_MODEL), kv5(cmp_s, db, dseq), kv5(slc_s, db, dseq),
             kv5(win_all[:, wlen + dseq - keep:], db, keep),
             conv_tail_s[:, CONV_HALO - (CONV_K - 1):],
             up_s.reshape(db, dseq, 2 * D_FF)[:, dseq - (FFN_CONV_K - 1):])
    return (out_p[0], out_s[0]) + out_p[1:] + out_s[1:]
```

```python
import functools

import numpy as np
import jax
import jax.numpy as jnp
from jax import lax
from jax.experimental import pallas as pl
from jax.experimental.pallas import tpu as pltpu

F32 = jnp.float32
BF16 = jnp.bfloat16

D_MODEL = 1024
N_HEADS = 16
HEAD_DIM = 64
N_KV = 4
GROUP = N_HEADS // N_KV
CMP_STRIDE = 16
CMP_LEN = 2 * CMP_STRIDE
SEL_BLOCK = 64
N_SEL = 16
WINDOW = 512
C_CONV = D_MODEL // 2
CONV_K = 31
FFN_CONV_K = 3
D_FF = 2816
ROPE_THETA = 10000.0
EPS = 1e-6
KV_W = 2 * N_KV * HEAD_DIM
K_W = N_KV * HEAD_DIM

LANES = 128
SUBLANES = 8
QBLK = WINDOW
TOK_TILE = 256
CONV_HALO = 32
VMEM_LIMIT = 56 * 1024 * 1024
FFN_CW = 256
FFN_NC = D_FF // FFN_CW
CMP_PAGES = 32
CMP_GROUP = 8
SLC_PAGES = 32
QROWS = GROUP * N_KV * 8

NEG = -1e30
PEN = -1e9
RANK_FORCED = 1e30
RANK_INVALID = -1e30
RANK_REMOVED = -2e30


def _dot(a, b):
    return jnp.dot(a, b, preferred_element_type=F32)


def _dot_nt(a, b):
    return lax.dot_general(a, b, (((1,), (1,)), ((), ())), preferred_element_type=F32)


def _sigmoid(x):
    return 1.0 / (1.0 + jnp.exp(-x))


def _iota(shape, dim):
    return lax.broadcasted_iota(jnp.int32, shape, dim)


def _params(sem):
    return pltpu.CompilerParams(dimension_semantics=sem, vmem_limit_bytes=VMEM_LIMIT)


def _const_spec(a, ngrid):
    nd = a.ndim
    if ngrid == 1:
        return pl.BlockSpec(a.shape, lambda i: (0,) * nd)
    if ngrid == 2:
        return pl.BlockSpec(a.shape, lambda i, j: (0,) * nd)
    return pl.BlockSpec(a.shape, lambda i, j, k: (0,) * nd)


def _rope_tables(pos):
    half = HEAD_DIM // 2
    inv = ROPE_THETA ** (-jnp.arange(half, dtype=F32) * (2.0 / HEAD_DIM))
    ang = pos.astype(F32)[:, None] * inv[None, :]
    cos, sin = jnp.cos(ang), jnp.sin(ang)
    return jnp.tile(cos, (1, 4)), jnp.tile(jnp.concatenate([-sin, sin], axis=1), (1, 2))


def _head_sum_mats(width):
    heads = width // HEAD_DIM
    lane_head = np.arange(width) // HEAD_DIM
    bd = (lane_head[:, None] == np.arange(LANES)[None, :]).astype(np.float32)
    ex = np.zeros((2 * LANES, width), np.float32)
    ex[:heads] = (np.arange(heads)[:, None] == lane_head[None, :])
    ex[LANES:LANES + heads] = ex[:heads]
    return jnp.asarray(bd, BF16), jnp.asarray(ex, BF16)


def _overlap_t(n_slc_rows, n_cols, col_shift, n_cmp):
    n = np.arange(n_cols) - col_shift
    j = np.arange(n_slc_rows)
    cs = n * CMP_STRIDE
    ss = j * SEL_BLOCK
    ov = (cs[None, :] < ss[:, None] + SEL_BLOCK) & (cs[None, :] + CMP_LEN > ss[:, None])
    ov &= (n[None, :] >= 0) & (n[None, :] < n_cmp)
    return ov.astype(np.float32)


def _head_rmsnorm(x, gain, bd, ex):
    ssum = _dot((x * x).astype(BF16), bd)
    r = lax.rsqrt(ssum * (1.0 / HEAD_DIM) + EPS)
    r_hi = r.astype(BF16)
    r_lo = (r - r_hi.astype(F32)).astype(BF16)
    rfull = _dot(jnp.concatenate([r_hi, r_lo], axis=1), ex)
    return x * rfull * gain


def _rope128(y, cos, sin):
    lane = _iota(y.shape, 1)
    first = (lane & (HEAD_DIM - 1)) < (HEAD_DIM // 2)
    rot = jnp.where(first, pltpu.roll(y, LANES - HEAD_DIM // 2, 1), pltpu.roll(y, HEAD_DIM // 2, 1))
    return y * cos + rot * sin


def _dup_heads(chunk):
    lo = _iota(chunk.shape, 1) < HEAD_DIM
    swapped = pltpu.roll(chunk, HEAD_DIM, 1)
    return jnp.where(lo, chunk, swapped), jnp.where(lo, swapped, chunk)


def _rank_select(val, n_rounds):
    rows = val.shape[0]
    j = _iota(val.shape, 0).astype(F32)
    sel = jnp.zeros(val.shape, F32)
    for _ in range(n_rounds):
        mx = jnp.max(val, axis=0, keepdims=True)
        idx = jnp.min(jnp.where(val == mx, j, float(rows)), axis=0, keepdims=True)
        hit = j == idx
        sel = jnp.where(hit, 1.0, sel)
        val = jnp.where(hit, RANK_REMOVED, val)
    return sel > 0.5


def _inproj_kernel(x_ref, gmix_ref, wu_ref, wq_ref, wkv_ref, wgn_ref, wgm_ref, gq_ref, gks_ref,
                   gkw_ref, bd16_ref, ex16_ref, bd4_ref, ex4_ref, cos_ref, sin_ref,
                   glu_ref, q_ref, cmp_ref, slc_ref, win_ref, gn_ref, gm_ref, *attn_refs):
    x = x_ref[...]
    ms = jnp.mean(x * x, axis=-1, keepdims=True)
    h = (x * lax.rsqrt(ms + EPS) * gmix_ref[...]).astype(BF16)
    cos = cos_ref[...]
    sin = sin_ref[...]

    u = _dot(h, wu_ref[...])
    glu_ref[...] = u[:, :C_CONV] * _sigmoid(u[:, C_CONV:])

    yq = _head_rmsnorm(_dot(h, wq_ref[...]), gq_ref[...], bd16_ref[...], ex16_ref[...])
    scale = HEAD_DIM ** -0.5
    for c in range(D_MODEL // LANES):
        sl = slice(c * LANES, (c + 1) * LANES)
        q_ref[:, sl] = (_rope128(yq[:, sl], cos, sin) * scale).astype(BF16)

    zkv = _dot(h, wkv_ref[...])
    cmp_ref[...] = zkv[:, :KV_W]

    def kv_branch(z, gain_ref, rows_ref, k2_ref, v2_ref):
        yk = _head_rmsnorm(z[:, :K_W], gain_ref[...], bd4_ref[...], ex4_ref[...])
        v = z[:, K_W:]
        rows_ref[:, K_W:] = v
        for c in range(K_W // LANES):
            sl = slice(c * LANES, (c + 1) * LANES)
            kr = _rope128(yk[:, sl], cos, sin)
            rows_ref[:, sl] = kr
            if k2_ref is not None:
                for ref, chunk in ((k2_ref, kr), (v2_ref, v[:, sl])):
                    a2, b2 = _dup_heads(chunk)
                    ref[2 * c] = a2.astype(BF16)
                    ref[2 * c + 1] = b2.astype(BF16)

    if attn_refs:
        k2s_ref, v2s_ref, k2w_ref, v2w_ref = attn_refs
    else:
        k2s_ref = v2s_ref = k2w_ref = v2w_ref = None
    kv_branch(zkv[:, KV_W:2 * KV_W], gks_ref, slc_ref, k2s_ref, v2s_ref)
    kv_branch(zkv[:, 2 * KV_W:], gkw_ref, win_ref, k2w_ref, v2w_ref)

    gn_ref[...] = _sigmoid(_dot(h, wgn_ref[...]))
    gm_ref[...] = _sigmoid(_dot(h, wgm_ref[...]))


def _inproj(x, cos, sin, wts, attn_layouts):
    B, T, _ = x.shape
    tm = TOK_TILE
    assert T % tm == 0
    nt = T // tm
    tok = lambda w: pl.BlockSpec((None, tm, w), lambda b, t: (b, t, 0))
    tab = pl.BlockSpec((tm, LANES), lambda b, t: (t, 0))
    in_specs = [tok(D_MODEL)] + [_const_spec(a, 2) for a in wts] + [tab, tab]
    out_shape = [
        jax.ShapeDtypeStruct((B, T, C_CONV), F32),
        jax.ShapeDtypeStruct((B, T, D_MODEL), BF16),
        jax.ShapeDtypeStruct((B, T, KV_W), F32),
        jax.ShapeDtypeStruct((B, T, KV_W), F32),
        jax.ShapeDtypeStruct((B, T, KV_W), F32),
        jax.ShapeDtypeStruct((B, T, N_KV * LANES), F32),
        jax.ShapeDtypeStruct((B, T, 2 * D_MODEL), F32),
    ]
    out_specs = [tok(C_CONV), tok(D_MODEL), tok(KV_W), tok(KV_W), tok(KV_W), tok(N_KV * LANES),
                 tok(2 * D_MODEL)]
    if attn_layouts:
        dup = jax.ShapeDtypeStruct((B, N_KV, T, LANES), BF16)
        dup_spec = pl.BlockSpec((None, N_KV, tm, LANES), lambda b, t: (b, 0, t, 0))
        out_shape += [dup] * 4
        out_specs += [dup_spec] * 4
    return pl.pallas_call(
        _inproj_kernel,
        grid=(B, nt),
        in_specs=in_specs,
        out_specs=out_specs,
        out_shape=out_shape,
        compiler_params=_params(("parallel", "parallel")),
        name="inproj",
    )(x, *wts, cos, sin)


def _conv_kernel(glu_ref, halo_ref, st_ref, wdw_ref, bdw_ref, gln_ref, bln_ref, wco_ref,
                 a_ref, cst_ref, xc_ref, s_ref, *, tt, rb):
    ti = pl.program_id(1)

    @pl.when(ti == 0)
    def _():
        xc_ref[0:CONV_HALO, :] = st_ref[...]

    @pl.when(ti > 0)
    def _():
        xc_ref[0:CONV_HALO, :] = halo_ref[...]

    xc_ref[CONV_HALO:CONV_HALO + tt, :] = glu_ref[...]
    xc_ref[CONV_HALO + tt:CONV_HALO + tt + SUBLANES, :] = jnp.zeros((SUBLANES, C_CONV), F32)
    first = CONV_HALO - (CONV_K - 1)
    for r0 in range(0, tt, rb):
        acc = jnp.broadcast_to(bdw_ref[...], (rb, C_CONV))
        for phi in range(SUBLANES):
            y = None
            for u in range(first, first + CONV_K):
                if u % SUBLANES == phi:
                    term = wdw_ref[u - first:u - first + 1, :] * xc_ref[r0 + u - phi:r0 + u - phi + rb + SUBLANES, :]
                    y = term if y is None else y + term
            acc = acc + y[phi:phi + rb]
        mu = jnp.mean(acc, axis=-1, keepdims=True)
        d = acc - mu
        var = jnp.mean(d * d, axis=-1, keepdims=True)
        y = d * lax.rsqrt(var + EPS) * gln_ref[...] + bln_ref[...]
        s_ref[r0:r0 + rb, :] = y * _sigmoid(y)
    a_ref[...] = _dot(s_ref[...].astype(BF16), wco_ref[...])
    cst_ref[...] = xc_ref[tt:tt + CONV_HALO, :]


def _conv_module(glu, halo_src, state32, wdw, bdw, gln, bln, wco, tt, rb):
    B, T, _ = glu.shape
    nt = T // tt
    hb = max(tt // CONV_HALO, 1)
    return pl.pallas_call(
        functools.partial(_conv_kernel, tt=tt, rb=rb),
        grid=(B, nt),
        in_specs=[
            pl.BlockSpec((None, tt, C_CONV), lambda b, t: (b, t, 0)),
            pl.BlockSpec((None, CONV_HALO, C_CONV), lambda b, t: (b, jnp.maximum(t * hb - 1, 0), 0)),
            pl.BlockSpec((None, CONV_HALO, C_CONV), lambda b, t: (b, 0, 0)),
            _const_spec(wdw, 2), _const_spec(bdw, 2), _const_spec(gln, 2), _const_spec(bln, 2),
            _const_spec(wco, 2),
        ],
        out_specs=[
            pl.BlockSpec((None, tt, D_MODEL), lambda b, t: (b, t, 0)),
            pl.BlockSpec((None, CONV_HALO, C_CONV), lambda b, t: (b, 0, 0)),
        ],
        out_shape=[
            jax.ShapeDtypeStruct((B, T, D_MODEL), F32),
            jax.ShapeDtypeStruct((B, CONV_HALO, C_CONV), F32),
        ],
        scratch_shapes=[pltpu.VMEM((CONV_HALO + tt + SUBLANES, C_CONV), F32), pltpu.VMEM((tt, C_CONV), F32)],
        compiler_params=_params(("parallel", "arbitrary")),
        name="conv_module",
    )(glu, halo_src, state32, wdw, bdw, gln, bln, wco)


def _merge_kernel(o_ref, a_ref, gm_ref, x_ref, wno_ref, wout_ref, x1_ref):
    b_out = _dot(o_ref[...], wno_ref[...])
    gm = gm_ref[...]
    m = gm[:, :D_MODEL] * a_ref[...] + gm[:, D_MODEL:] * b_out
    x1_ref[...] = x_ref[...] + _dot(m.astype(BF16), wout_ref[...])


def _merge(o, a_out, gm, x, wno, wout, tm):
    N = x.shape[0]
    row = lambda w: pl.BlockSpec((tm, w), lambda i: (i, 0))
    return pl.pallas_call(
        _merge_kernel,
        grid=(N // tm,),
        in_specs=[row(D_MODEL), row(D_MODEL), row(2 * D_MODEL), row(D_MODEL),
                  _const_spec(wno, 1), _const_spec(wout, 1)],
        out_specs=row(D_MODEL),
        out_shape=jax.ShapeDtypeStruct((N, D_MODEL), F32),
        compiler_params=_params(("parallel",)),
        name="merge",
    )(o, a_out, gm, x, wno, wout)


def _ffn_kernel(x1_ref, gn_ref, wup_ref, wdw_ref, wdn_ref, fix1_ref, fix2_ref,
                y_ref, st_ref, prev_ref, acc_ref, *, tm, seg):
    streaming = seg == tm
    x1 = x1_ref[...]
    ms = jnp.mean(x1 * x1, axis=-1, keepdims=True)
    h = (x1 * lax.rsqrt(ms + EPS) * gn_ref[...]).astype(BF16)
    row = _iota((tm, FFN_CW), 0) & (seg - 1)

    if streaming:
        @pl.when(pl.program_id(1) == 0)
        def _():
            prev_ref[...] = fix2_ref[...]

    def conv3(c):
        cs = slice(c * FFN_CW, (c + 1) * FFN_CW)
        up = _dot(h, wup_ref[c])
        r1 = pltpu.roll(up, 1, 0)
        r2 = pltpu.roll(up, 2, 0)
        if streaming:
            p0 = prev_ref[0:1, cs]
            p1 = prev_ref[1:2, cs]
            s1 = jnp.where(row == 0, p1, r1)
            s2 = jnp.where(row == 0, p0, jnp.where(row == 1, p1, r2))
            prev_ref[0:2, cs] = up[tm - 2:tm, :]
        else:
            s1 = jnp.where(row == 0, fix1_ref[:, cs], r1)
            s2 = jnp.where(row < 2, fix2_ref[:, cs], r2)
            st_ref[:, cs] = up
        w = wdw_ref[c]
        return w[0:1] * s2 + w[1:2] * s1 + w[2:3] * up + w[3:4]

    acc_ref[...] = x1
    for c in range(FFN_NC):
        gate = conv3(c)
        val = conv3(FFN_NC + c)
        act = gate * _sigmoid(gate) * val
        acc_ref[...] += _dot(act.astype(BF16), wdn_ref[c])
    y_ref[...] = acc_ref[...]
    if streaming:
        st_ref[...] = prev_ref[...]


def _ffn(x1, gn, wup, wdw, wdn, fix1, fix2, tm, seg):
    B, T, _ = x1.shape
    nt = T // tm
    streaming = seg == tm
    tok = lambda w: pl.BlockSpec((None, tm, w), lambda b, t: (b, t, 0))
    if streaming:
        fix_specs = [pl.BlockSpec((None, SUBLANES, 2 * D_FF), lambda b, t: (b, 0, 0))] * 2
        st_shape = jax.ShapeDtypeStruct((B, SUBLANES, 2 * D_FF), F32)
        st_spec = pl.BlockSpec((None, SUBLANES, 2 * D_FF), lambda b, t: (b, 0, 0))
    else:
        fix_specs = [tok(2 * D_FF)] * 2
        st_shape = jax.ShapeDtypeStruct((B, T, 2 * D_FF), F32)
        st_spec = tok(2 * D_FF)
    return pl.pallas_call(
        functools.partial(_ffn_kernel, tm=tm, seg=seg),
        grid=(B, nt),
        in_specs=[tok(D_MODEL), _const_spec(gn, 2), _const_spec(wup, 2), _const_spec(wdw, 2),
                  _const_spec(wdn, 2)] + fix_specs,
        out_specs=[tok(D_MODEL), st_spec],
        out_shape=[jax.ShapeDtypeStruct((B, T, D_MODEL), F32), st_shape],
        scratch_shapes=[pltpu.VMEM((SUBLANES, 2 * D_FF), F32), pltpu.VMEM((tm, D_MODEL), F32)],
        compiler_params=_params(("parallel", "arbitrary")),
        name="conv_ffn",
    )(x1, gn, wup, wdw, wdn, fix1, fix2)


def _cmp_prompt_kernel(rows_ref, bdk_ref, bdv_ref, pek_ref, wk4_ref, pev_ref, wv4_ref, gk_ref,
                       bd4_ref, ex4_ref, cos_ref, sin_ref, kc2_ref, vc2_ref, *, nch):
    acc_k = jnp.zeros((nch, 2 * K_W), F32)
    acc_v = jnp.zeros((nch, 2 * K_W), F32)
    for l in range(CMP_STRIDE):
        acc_k += _dot(rows_ref[:, l * KV_W:l * KV_W + K_W].astype(BF16), bdk_ref[l])
        acc_v += _dot(rows_ref[:, l * KV_W + K_W:(l + 1) * KV_W].astype(BF16), bdv_ref[l])
    pos_k = _dot(pek_ref[...], wk4_ref[...])[0:1]
    pos_v = _dot(pev_ref[...], wv4_ref[...])[0:1]
    kc = acc_k[:, :K_W] + pltpu.roll(acc_k[:, K_W:], nch - 1, 0) + pos_k
    vc = acc_v[:, :K_W] + pltpu.roll(acc_v[:, K_W:], nch - 1, 0) + pos_v
    yk = _head_rmsnorm(kc, gk_ref[...], bd4_ref[...], ex4_ref[...])
    for c in range(K_W // LANES):
        sl = slice(c * LANES, (c + 1) * LANES)
        for ref, chunk in ((kc2_ref, _rope128(yk[:, sl], cos_ref[...], sin_ref[...])), (vc2_ref, vc[:, sl])):
            a2, b2 = _dup_heads(chunk)
            ref[2 * c] = a2.astype(BF16)
            ref[2 * c + 1] = b2.astype(BF16)


def _cmp_prompt(rows, cw, cos, sin):
    B, nch, _ = rows.shape
    return pl.pallas_call(
        functools.partial(_cmp_prompt_kernel, nch=nch),
        grid=(B,),
        in_specs=[pl.BlockSpec((None, nch, CMP_STRIDE * KV_W), lambda b: (b, 0, 0))] + [_const_spec(a, 1) for a in cw]
        + [_const_spec(cos, 1), _const_spec(sin, 1)],
        out_specs=[pl.BlockSpec((None, N_KV, nch, LANES), lambda b: (b, 0, 0, 0))] * 2,
        out_shape=[jax.ShapeDtypeStruct((B, N_KV, nch, LANES), BF16)] * 2,
        compiler_params=_params(("parallel",)),
        name="cmp_prompt",
    )(rows, *cw, cos, sin)


def _attn_prompt_kernel(q_ref, gates_ref, kc2_ref, vc2_ref, k2s_ref, v2s_ref, k2w_ref, v2w_ref,
                        e_ref, ov_ref, gex_ref, o_ref, m_ref, l_ref, acc_ref, mw_ref, lw_ref, accw_ref,
                        *, n_cmp, n_slc, n_top):
    qi = pl.program_id(2)
    qs = qi * QBLK
    rows = GROUP * QBLK
    lo = _iota((QBLK, LANES), 1) < HEAD_DIM
    zero = jnp.zeros((QBLK, LANES), BF16)
    qm = []
    for p in range(GROUP // 2):
        qp = q_ref[:, p * LANES:(p + 1) * LANES]
        qm += [jnp.where(lo, qp, zero), jnp.where(lo, zero, qp)]
    q_all = jnp.concatenate(qm, axis=0)

    nch = kc2_ref.shape[0]
    last = jnp.minimum((qs + _iota((QBLK, nch), 0) - (CMP_LEN - 1)) >> 4, n_cmp - 1)
    cmask = _iota((QBLK, nch), 1) <= last

    def masked(x, fill):
        return jnp.concatenate([jnp.where(cmask, x[r * QBLK:(r + 1) * QBLK], fill) for r in range(GROUP)], axis=0)

    s = masked(_dot_nt(q_all, kc2_ref[...]), NEG)
    mx = jnp.max(s, axis=1, keepdims=True)
    e = masked(jnp.exp(s - mx), 0.0)
    p_cmp = (e * (1.0 / jnp.maximum(jnp.sum(e, axis=1, keepdims=True), 1e-30))).astype(BF16)
    o_cmp = _dot(p_cmp, vc2_ref[...])
    imp4 = _dot(p_cmp, ov_ref[...])
    imp = imp4[0:QBLK]
    for r in range(1, GROUP):
        imp = imp + imp4[r * QBLK:(r + 1) * QBLK]

    kl = _iota((QBLK, QBLK), 1)
    tl = _iota((QBLK, QBLK), 0)
    tri = {"causal": kl <= tl, "band": kl >= tl}
    slc_state = (m_ref, l_ref, acc_ref)
    win_state = (mw_ref, lw_ref, accw_ref)

    def reset(state):
        m_st, l_st, acc_st = state
        m_st[...] = jnp.full(m_st.shape, NEG, F32)
        l_st[...] = jnp.zeros(l_st.shape, F32)
        acc_st[...] = jnp.zeros(acc_st.shape, F32)

    def attend(state, kblk, v2blk, q_rows, mode):
        m_st, l_st, acc_st = state
        sc = _dot_nt(q_rows, kblk)
        if mode != "none":
            sc = jnp.concatenate([jnp.where(tri[mode], sc[r * QBLK:(r + 1) * QBLK], NEG)
                                  for r in range(GROUP)], axis=0)
        m_old = m_st[...]
        m_new = jnp.maximum(m_old, jnp.max(sc, axis=1, keepdims=True))
        alpha = jnp.exp(m_old - m_new)
        p = jnp.exp(sc - jnp.concatenate([m_new] * (sc.shape[1] // LANES), axis=1))
        l_st[...] = alpha * l_st[...] + jnp.sum(p, axis=1, keepdims=True)
        acc_st[...] = alpha * acc_st[...] + _dot(p.astype(BF16), v2blk)
        m_st[...] = m_new

    def blk(ref, kb):
        return ref[pl.ds(pl.multiple_of(kb * QBLK, QBLK), QBLK), :]

    reset(win_state)
    attend(win_state, blk(k2w_ref, qi), blk(v2w_ref, qi), q_all, "causal")

    imp_t = imp.T[0:n_slc]
    j = _iota((n_slc, QBLK), 0)
    tq = qs + _iota((n_slc, QBLK), 1)
    cur = tq >> 6
    forced = jnp.logical_or(j == 0, jnp.logical_or(j == cur, j == cur - 1))
    rank = jnp.where(forced, RANK_FORCED, jnp.where(j * SEL_BLOCK <= tq, imp_t, RANK_INVALID))
    pen_t = jnp.where(_rank_select(rank, n_top), 0.0, PEN)
    if n_slc < LANES:
        pen_t = jnp.concatenate([pen_t, jnp.zeros((LANES - n_slc, QBLK), F32)], axis=0)
    pen = pen_t.T.astype(BF16)
    qa_all = jnp.concatenate([q_all, jnp.concatenate([pen] * GROUP, axis=0)], axis=1)

    def slc_keys(kb):
        return jnp.concatenate([blk(k2s_ref, kb), blk(e_ref, kb)], axis=1)

    reset(slc_state)

    def slc_body(kb, carry):
        attend(slc_state, slc_keys(kb), blk(v2s_ref, kb), qa_all, "none")
        return carry

    lax.fori_loop(0, qi, slc_body, 0)
    attend(slc_state, slc_keys(qi), blk(v2s_ref, qi), qa_all, "causal")
    o_slc = acc_ref[...] / l_ref[...]

    @pl.when(qi >= 1)
    def _():
        attend(win_state, blk(k2w_ref, qi - 1), blk(v2w_ref, qi - 1), q_all, "band")

    o_win = accw_ref[...] / lw_ref[...]

    gt = gates_ref[...]
    g_hi = gt.astype(BF16)
    g_lo = (gt - g_hi.astype(F32)).astype(BF16)
    gex = _dot(jnp.concatenate([g_hi, g_lo], axis=1), gex_ref[...])
    mixed = []
    for r in range(GROUP):
        rs = slice(r * QBLK, (r + 1) * QBLK)
        gl = lambda k: gex[:, (3 * r + k) * LANES:(3 * r + k + 1) * LANES]
        mixed.append(gl(0) * o_cmp[rs] + gl(1) * o_slc[rs] + gl(2) * o_win[rs])
    for p in range(GROUP // 2):
        o_ref[:, p * LANES:(p + 1) * LANES] = jnp.where(lo, mixed[2 * p], mixed[2 * p + 1]).astype(BF16)


def _attn_prompt(q, gates, kc2, vc2, k2s, v2s, k2w, v2w, e_tab, ov, gex, n_cmp, n_slc, n_top):
    B, T, _ = q.shape
    nq = T // QBLK
    nch = kc2.shape[2]
    rows = GROUP * QBLK
    per_bg = lambda n: pl.BlockSpec((None, None, n, LANES), lambda b, g, i: (b, g, 0, 0))
    return pl.pallas_call(
        functools.partial(_attn_prompt_kernel, n_cmp=n_cmp, n_slc=n_slc, n_top=n_top),
        grid=(B, N_KV, nq),
        in_specs=[
            pl.BlockSpec((None, QBLK, K_W), lambda b, g, i: (b, i, g)),
            pl.BlockSpec((None, QBLK, LANES), lambda b, g, i: (b, i, g)),
            per_bg(nch), per_bg(nch), per_bg(T), per_bg(T), per_bg(T), per_bg(T),
            _const_spec(e_tab, 3), _const_spec(ov, 3), _const_spec(gex, 3),
        ],
        out_specs=pl.BlockSpec((None, QBLK, K_W), lambda b, g, i: (b, i, g)),
        out_shape=jax.ShapeDtypeStruct((B, T, D_MODEL), BF16),
        scratch_shapes=[pltpu.VMEM((rows, LANES), F32)] * 6,
        compiler_params=_params(("parallel", "parallel", "arbitrary")),
        name="attn_prompt",
    )(q, gates, kc2, vc2, k2s, v2s, k2w, v2w, e_tab, ov, gex)


def _cmp_stream_kernel(pt_ref, *refs, m):
    pages = refs[:CMP_PAGES]
    bdk_ref, bdv_ref, kc_ref, vc_ref, fs_ref, slab_ref = refs[CMP_PAGES:]
    s = pl.program_id(1)
    per_page = m // CMP_PAGES
    nslab = KV_W // LANES

    @pl.when(s == 0)
    def _():
        fs_ref[0:SUBLANES, :] = jnp.zeros((SUBLANES, 2 * K_W), F32)

    half = K_W // 2

    def page_group(k0, k1):
        mg = (k1 - k0) * per_page
        for k in range(k0, k1):
            for c in range(nslab):
                slab_ref[k, c] = pages[k][c * LANES:(c + 1) * LANES, :].T

        def rows_of(c, l):
            return jnp.concatenate([slab_ref[k, c, pl.ds(l, per_page, stride=CMP_STRIDE), :]
                                    for k in range(k0, k1)], axis=0)

        def pair_rows(c, l):
            return jnp.concatenate([rows_of(c, l), rows_of(c, l + CMP_STRIDE // 2)], axis=1)

        acc_k = jnp.zeros((2 * mg, K_W), F32)
        acc_v = jnp.zeros((2 * mg, K_W), F32)
        for l in range(CMP_STRIDE // 2):
            xk = jnp.concatenate([pair_rows(0, l), pair_rows(1, l)], axis=0)
            xv = jnp.concatenate([pair_rows(2, l), pair_rows(3, l)], axis=0)
            acc_k += _dot(xk.astype(BF16), bdk_ref[l])
            acc_v += _dot(xv.astype(BF16), bdv_ref[l])
        return (jnp.concatenate([acc_k[:mg, :half], acc_k[mg:, :half], acc_v[:mg, :half], acc_v[mg:, :half]], axis=1),
                jnp.concatenate([acc_k[:mg, half:], acc_k[mg:, half:]], axis=1),
                jnp.concatenate([acc_v[:mg, half:], acc_v[mg:, half:]], axis=1))

    groups = [page_group(k0, k0 + CMP_GROUP) for k0 in range(0, CMP_PAGES, CMP_GROUP)]
    first, second_k, second_v = [jnp.concatenate([g[i] for g in groups], axis=0) for i in range(3)]
    fs_ref[SUBLANES:SUBLANES + m, :] = first
    shifted = fs_ref[SUBLANES - 1:SUBLANES - 1 + m, :]
    kc_ref[...] = shifted[:, :K_W] + second_k
    vc_ref[...] = shifted[:, K_W:] + second_v
    fs_ref[SUBLANES - 1:SUBLANES, :] = fs_ref[SUBLANES - 1 + m:SUBLANES + m, :]


def _cmp_stream(page_table, cache_t, bdk, bdv):
    db, n_pages = page_table.shape
    page = cache_t.shape[2]
    per_page = page // CMP_STRIDE
    m = CMP_PAGES * per_page
    nch = n_pages * per_page
    nsteps = n_pages // CMP_PAGES

    def page_spec(k):
        return pl.BlockSpec((None, KV_W, page), lambda b, s, pt: (pt[b, s * CMP_PAGES + k], 0, 0))

    grid_spec = pltpu.PrefetchScalarGridSpec(
        num_scalar_prefetch=1,
        grid=(db, nsteps),
        in_specs=[page_spec(k) for k in range(CMP_PAGES)]
        + [pl.BlockSpec(bdk.shape, lambda b, s, pt: (0, 0, 0)), pl.BlockSpec(bdv.shape, lambda b, s, pt: (0, 0, 0))],
        out_specs=[pl.BlockSpec((None, m, K_W), lambda b, s, pt: (b, s, 0))] * 2,
        scratch_shapes=[pltpu.VMEM((SUBLANES + m, 2 * K_W), F32),
                        pltpu.VMEM((CMP_PAGES, KV_W // LANES, page, LANES), F32)],
    )
    return pl.pallas_call(
        functools.partial(_cmp_stream_kernel, m=m),
        grid_spec=grid_spec,
        out_shape=[jax.ShapeDtypeStruct((db, nch, K_W), F32)] * 2,
        compiler_params=_params(("parallel", "arbitrary")),
        name="cmp_stream",
    )(page_table, *([cache_t] * CMP_PAGES), bdk, bdv)


def _block_diag_q(q):
    lo = _iota((8, LANES), 1) < HEAD_DIM
    zero = jnp.zeros((8, LANES), BF16)
    blocks = []
    for r in range(GROUP):
        for g in range(N_KV):
            hh = g * GROUP + r
            pair = q[:, (hh // 2) * LANES:(hh // 2 + 1) * LANES]
            if hh % 2 != g % 2:
                pair = pltpu.roll(pair.astype(F32), HEAD_DIM, 1).astype(BF16)
            keep = jnp.where(lo, pair, zero) if g % 2 == 0 else jnp.where(lo, zero, pair)
            blocks.append(jnp.concatenate([keep, zero] if g < 2 else [zero, keep], axis=1))
    return jnp.concatenate(blocks, axis=0)


def _cmp_sample_kernel(kcs_ref, vcs_ref, q_ref, pek_ref, wk4_ref, pev_ref, wv4_ref, gk_ref, bd4_ref,
                       ex4_ref, cos_ref, sin_ref, ovt_ref, qbd_ref, oc_ref, pen_ref,
                       *, past_len, n_slc, n_top):
    nch = kcs_ref.shape[0]
    pos_k = _dot(pek_ref[...], wk4_ref[...])[0:1]
    pos_v = _dot(pev_ref[...], wv4_ref[...])[0:1]
    yk = _head_rmsnorm(kcs_ref[...] + pos_k, gk_ref[...], bd4_ref[...], ex4_ref[...])
    kcn = jnp.concatenate([_rope128(yk[:, c * LANES:(c + 1) * LANES], cos_ref[...], sin_ref[...])
                           for c in range(K_W // LANES)], axis=1).astype(BF16)
    vc = (vcs_ref[...] + pos_v).astype(BF16)
    qbd = _block_diag_q(q_ref[...])
    qbd_ref[...] = qbd

    s = _dot_nt(qbd, kcn)
    mi = _iota((QROWS, nch), 1)
    t_row = past_len + (_iota((QROWS, nch), 0) & 7)
    mask = jnp.logical_and(mi * CMP_STRIDE + (CMP_STRIDE - 1) <= t_row, mi >= 1)
    s = jnp.where(mask, s, NEG)
    mx = jnp.max(s, axis=1, keepdims=True)
    e = jnp.where(mask, jnp.exp(s - mx), 0.0)
    p = (e / jnp.maximum(jnp.sum(e, axis=1, keepdims=True), 1e-30)).astype(BF16)
    oc_ref[...] = _dot(p, vc)

    imp = _dot_nt(ovt_ref[...], p)
    imp = imp + pltpu.roll(imp, 32, 1) + pltpu.roll(imp, 64, 1) + pltpu.roll(imp, 96, 1)
    nsp = imp.shape[0]
    j = _iota((nsp, QROWS), 0)
    t_lane = past_len + (_iota((nsp, QROWS), 1) & 7)
    cur = t_lane >> 6
    forced = jnp.logical_or(j == 0, jnp.logical_or(j == cur, j == cur - 1))
    rank = jnp.where(forced, RANK_FORCED, jnp.where(j * SEL_BLOCK <= t_lane, imp, RANK_INVALID))
    rank = jnp.where(j < n_slc, rank, RANK_REMOVED)
    sel = _rank_select(rank, n_top)
    pen_t = jnp.where(sel, 0.0, PEN)
    for blk in range(nsp // LANES):
        pen_ref[blk] = pen_t[blk * LANES:(blk + 1) * LANES, :].T.astype(BF16)


def _cmp_sample(kcs, vcs, q, cw_tail, cos, sin, ovt, past_len, n_slc, n_top):
    db, nch, _ = kcs.shape
    nsp = ovt.shape[0]
    per_b = lambda shape: pl.BlockSpec((None,) + shape, lambda b: (b,) + (0,) * len(shape))
    return pl.pallas_call(
        functools.partial(_cmp_sample_kernel, past_len=past_len, n_slc=n_slc, n_top=n_top),
        grid=(db,),
        in_specs=[per_b((nch, K_W)), per_b((nch, K_W)), per_b((8, D_MODEL))]
        + [_const_spec(a, 1) for a in cw_tail] + [_const_spec(cos, 1), _const_spec(sin, 1), _const_spec(ovt, 1)],
        out_specs=[per_b((QROWS, K_W)), per_b((QROWS, K_W)), per_b((nsp // LANES, QROWS, LANES))],
        out_shape=[jax.ShapeDtypeStruct((db, QROWS, K_W), BF16),
                   jax.ShapeDtypeStruct((db, QROWS, K_W), F32),
                   jax.ShapeDtypeStruct((db, nsp // LANES, QROWS, LANES), BF16)],
        compiler_params=_params(("parallel",)),
        name="cmp_sample",
    )(kcs, vcs, q, *cw_tail, cos, sin, ovt)


def _slc_stream_kernel(pt_ref, *refs):
    pages = refs[:SLC_PAGES]
    qbd_ref, pen_ref, m_out, l_out, acc_out, m_ref, l_ref, acc_ref = refs[SLC_PAGES:]
    s = pl.program_id(1)
    page = pages[0].shape[1]
    nk = SLC_PAGES * page
    blocks_per_step = nk // SEL_BLOCK

    @pl.when(s == 0)
    def _():
        m_ref[...] = jnp.full(m_ref.shape, NEG, F32)
        l_ref[...] = jnp.zeros(l_ref.shape, F32)
        acc_ref[...] = jnp.zeros(acc_ref.shape, F32)

    k_t = jnp.concatenate([pg[0:K_W, :] for pg in pages], axis=1).astype(BF16)
    v_t = jnp.concatenate([pg[K_W:, :] for pg in pages], axis=1).astype(BF16)
    steps_per_lane_block = LANES // blocks_per_step
    pen = pen_ref[s // steps_per_lane_block]
    jrow = _iota((LANES, nk), 0)
    blk = (s % steps_per_lane_block) * blocks_per_step + (_iota((LANES, nk), 1) >> 6)
    expand = jnp.where(jrow == blk, 1.0, 0.0).astype(BF16)
    sc = _dot(qbd_ref[...], k_t) + _dot(pen, expand)
    m_old = m_ref[...]
    m_new = jnp.maximum(m_old, jnp.max(sc, axis=1, keepdims=True))
    alpha = jnp.exp(m_old - m_new)
    p = jnp.exp(sc - m_new)
    l_ref[...] = alpha * l_ref[...] + jnp.sum(p, axis=1, keepdims=True)
    acc_ref[...] = alpha * acc_ref[...] + _dot_nt(p.astype(BF16), v_t)
    m_ref[...] = m_new

    @pl.when(s == pl.num_programs(1) - 1)
    def _():
        m_out[...] = jnp.broadcast_to(m_ref[...], m_out.shape)
        l_out[...] = jnp.broadcast_to(l_ref[...], l_out.shape)
        acc_out[...] = acc_ref[...]


def _slc_stream(page_table, cache_t, qbd, pen):
    db, n_pages = page_table.shape
    page = cache_t.shape[2]
    nsteps = n_pages // SLC_PAGES

    def page_spec(k):
        return pl.BlockSpec((None, KV_W, page), lambda b, s, pt: (pt[b, s * SLC_PAGES + k], 0, 0))

    per_b = lambda shape: pl.BlockSpec((None,) + shape, lambda b, s, pt: (b,) + (0,) * len(shape))
    grid_spec = pltpu.PrefetchScalarGridSpec(
        num_scalar_prefetch=1,
        grid=(db, nsteps),
        in_specs=[page_spec(k) for k in range(SLC_PAGES)] + [per_b((QROWS, K_W)), per_b(pen.shape[1:])],
        out_specs=[per_b((QROWS, LANES)), per_b((QROWS, LANES)), per_b((QROWS, K_W))],
        scratch_shapes=[pltpu.VMEM((QROWS, 1), F32), pltpu.VMEM((QROWS, 1), F32), pltpu.VMEM((QROWS, K_W), F32)],
    )
    return pl.pallas_call(
        _slc_stream_kernel,
        grid_spec=grid_spec,
        out_shape=[jax.ShapeDtypeStruct((db, QROWS, LANES), F32), jax.ShapeDtypeStruct((db, QROWS, LANES), F32),
                   jax.ShapeDtypeStruct((db, QROWS, K_W), F32)],
        compiler_params=_params(("parallel", "arbitrary")),
        name="slc_stream",
    )(page_table, *([cache_t] * SLC_PAGES), qbd, pen)


def _finish_sample_kernel(m_ref, l_ref, acc_ref, oc_ref, qbd_ref, pen_ref, slc_new_ref, win_new_ref,
                          win_ref, gates_ref, o_ref, *, past_len, new_blk):
    qbd = qbd_ref[...]
    tok = _iota((QROWS, LANES), 0) & 7
    col = _iota((QROWS, LANES), 1)
    pad = jnp.zeros((LANES - 8, K_W), F32)

    kn = jnp.concatenate([slc_new_ref[:, 0:K_W], pad], axis=0).astype(BF16)
    vn = jnp.concatenate([slc_new_ref[:, K_W:], pad], axis=0).astype(BF16)
    pen_col = pen_ref[new_blk // LANES][:, new_blk % LANES:new_blk % LANES + 1].astype(F32)
    sn = jnp.where(col <= tok, _dot_nt(qbd, kn) + pen_col, NEG)
    m_old = m_ref[:, 0:1]
    m_new = jnp.maximum(m_old, jnp.max(sn, axis=1, keepdims=True))
    alpha = jnp.exp(m_old - m_new)
    pn = jnp.exp(sn - m_new)
    l_new = alpha * l_ref[:, 0:1] + jnp.sum(pn, axis=1, keepdims=True)
    o_slc = (alpha * acc_ref[...] + _dot(pn.astype(BF16), vn)) / l_new

    wlen = win_ref.shape[1]
    new_t = jnp.concatenate([win_new_ref[...], jnp.zeros((LANES - 8, KV_W), F32)], axis=0).T
    kw_t = jnp.concatenate([win_ref[0:K_W, :], new_t[0:K_W]], axis=1).astype(BF16)
    vw_t = jnp.concatenate([win_ref[K_W:, :], new_t[K_W:]], axis=1).astype(BF16)
    nkw = wlen + LANES
    idx = _iota((QROWS, nkw), 1)
    t_row = past_len + (_iota((QROWS, nkw), 0) & 7)
    kpos = past_len - wlen + idx
    dist = t_row - kpos
    wmask = jnp.logical_and(jnp.logical_and(dist >= 0, dist <= WINDOW),
                            jnp.logical_and(kpos >= 0, idx < wlen + 8))
    sw = jnp.where(wmask, _dot(qbd, kw_t), NEG)
    mw = jnp.max(sw, axis=1, keepdims=True)
    ew = jnp.where(wmask, jnp.exp(sw - mw), 0.0)
    pw = ew / jnp.maximum(jnp.sum(ew, axis=1, keepdims=True), 1e-30)
    o_win = _dot_nt(pw.astype(BF16), vw_t)

    o_cmp = oc_ref[...]
    gates = gates_ref[...]
    lo = _iota((8, LANES), 1) < HEAD_DIM
    for c in range(D_MODEL // LANES):
        g = c // 2
        halves = []
        for e in range(2):
            r = 2 * (c % 2) + e
            rows = slice(r * 32 + g * 8, r * 32 + g * 8 + 8)
            lanes = slice((g // 2) * LANES, (g // 2 + 1) * LANES)
            base = g * LANES + 3 * r
            blk = (gates[:, base:base + 1] * o_cmp[rows, lanes] + gates[:, base + 1:base + 2] * o_slc[rows, lanes]
                   + gates[:, base + 2:base + 3] * o_win[rows, lanes])
            if g % 2 != e:
                blk = pltpu.roll(blk, HEAD_DIM, 1)
            halves.append(blk)
        o_ref[:, c * LANES:(c + 1) * LANES] = jnp.where(lo, halves[0], halves[1]).astype(BF16)


def _finish_sample(m, l, acc, oc, qbd, pen, slc_new, win_new, win3, gates, past_len, new_blk):
    db = m.shape[0]
    per_b = lambda shape: pl.BlockSpec((None,) + shape, lambda b: (b,) + (0,) * len(shape))
    ins = (m, l, acc, oc, qbd, pen, slc_new, win_new, win3, gates)
    return pl.pallas_call(
        functools.partial(_finish_sample_kernel, past_len=past_len, new_blk=new_blk),
        grid=(db,),
        in_specs=[per_b(a.shape[1:]) for a in ins],
        out_specs=per_b((8, D_MODEL)),
        out_shape=jax.ShapeDtypeStruct((db, 8, D_MODEL), BF16),
        compiler_params=_params(("parallel",)),
        name="finish_sample",
    )(*ins)


def _cmp_weights(w_c):
    eye = jnp.eye(N_KV, dtype=F32)
    bd = jnp.einsum("gh,lde->lgdhe", eye, w_c).reshape(CMP_LEN, K_W, K_W)
    return jnp.concatenate([bd[:CMP_STRIDE], bd[CMP_STRIDE:]], axis=2).astype(BF16)


def _cmp_pair_weights(w_c):
    eye = jnp.eye(2, dtype=F32)
    bd = jnp.einsum("gh,lde->lgdhe", eye, w_c).reshape(CMP_LEN, LANES, LANES)
    fs = jnp.concatenate([bd[:CMP_STRIDE], bd[CMP_STRIDE:]], axis=2)
    return jnp.concatenate([fs[:CMP_STRIDE // 2], fs[CMP_STRIDE // 2:]], axis=1).astype(BF16)


def _pos_operands(pe, w_c):
    pe_rows = jnp.tile(pe.reshape(1, CMP_LEN * HEAD_DIM), (SUBLANES, 1)).astype(BF16)
    w4 = jnp.tile(w_c.reshape(CMP_LEN * HEAD_DIM, HEAD_DIM), (1, N_KV)).astype(BF16)
    return pe_rows, w4


def _tile_gain(g, width):
    return jnp.tile(g, width // g.shape[0]).reshape(1, width).astype(F32)


def kernel(x_prompt, x_sample, cache_cmp_kv, cache_slc_kv, cache_win_kv, state_conv, state_ffn_conv,
           page_table, g_norm_mix, w_in, w_dw, b_dw, g_ln_conv, b_ln_conv, w_conv_out, g_q, g_k_cmp,
           g_k_slc, g_k_win, w_cmp_k, w_cmp_v, pe_cmp_k, pe_cmp_v, w_nsa_out, w_out, g_norm_ffn, w_up,
           w_ffn_dw, b_ffn_dw, w_down):
    bp, seq, _ = x_prompt.shape
    db, dseq, _ = x_sample.shape
    n_pool, page = cache_cmp_kv.shape[0], cache_cmp_kv.shape[1]
    n_pages = page_table.shape[1]
    past_len = n_pages * page
    wlen = cache_win_kv.shape[1]
    assert dseq == 8 and seq % QBLK == 0 and seq // SEL_BLOCK <= LANES and QBLK == WINDOW
    assert past_len % SEL_BLOCK == 0 and wlen == WINDOW and past_len >= WINDOW
    assert n_pages % CMP_PAGES == 0 and n_pages % SLC_PAGES == 0 and page % CMP_STRIDE == 0
    assert LANES % (SLC_PAGES * page // SEL_BLOCK) == 0

    offs = np.cumsum((2 * C_CONV, N_HEADS * HEAD_DIM, 3 * KV_W, 3 * N_HEADS))
    wu = w_in[:, :offs[0]].astype(BF16)
    wq = w_in[:, offs[0]:offs[1]].astype(BF16)
    wkv = w_in[:, offs[1]:offs[2]].astype(BF16)
    wgn = jnp.pad(w_in[:, offs[2]:offs[3]].reshape(D_MODEL, N_KV, 3 * GROUP),
                  ((0, 0), (0, 0), (0, LANES - 3 * GROUP))).reshape(D_MODEL, N_KV * LANES).astype(BF16)
    wgm = w_in[:, offs[3]:].astype(BF16)
    bd16, ex16 = _head_sum_mats(D_MODEL)
    bd4, ex4 = _head_sum_mats(K_W)
    in_wts = (g_norm_mix.reshape(1, D_MODEL), wu, wq, wkv, wgn, wgm, _tile_gain(g_q, D_MODEL),
              _tile_gain(g_k_slc, K_W), _tile_gain(g_k_win, K_W), bd16, ex16, bd4, ex4)
    wdw = jnp.pad(w_dw, ((0, 1), (0, 0)))
    conv_wts = (wdw, b_dw.reshape(1, C_CONV), g_ln_conv.reshape(1, C_CONV), b_ln_conv.reshape(1, C_CONV),
                w_conv_out.astype(BF16))
    wno = w_nsa_out.astype(BF16)
    wout = w_out.astype(BF16)
    gffn = g_norm_ffn.reshape(1, D_MODEL)
    wup = w_up.reshape(D_MODEL, 2 * FFN_NC, FFN_CW).transpose(1, 0, 2).astype(BF16)
    wdn = w_down.reshape(FFN_NC, FFN_CW, D_MODEL).astype(BF16)
    wfd = jnp.concatenate([w_ffn_dw, b_ffn_dw[None, :], jnp.zeros((4, 2 * D_FF), F32)], axis=0)
    wfd = wfd.reshape(SUBLANES, 2 * FFN_NC, FFN_CW).transpose(1, 0, 2)
    bdk = _cmp_weights(w_cmp_k)
    bdv = _cmp_weights(w_cmp_v)
    pek, wk4 = _pos_operands(pe_cmp_k, w_cmp_k)
    pev, wv4 = _pos_operands(pe_cmp_v, w_cmp_v)
    gkc = _tile_gain(g_k_cmp, K_W)
    cw_tail = (pek, wk4, pev, wv4, gkc, bd4, ex4)

    cos_p, sin_p = _rope_tables(jnp.arange(seq))
    (glu, q, cmp_rows, slc_rows, win_rows, gn, gm, k2s, v2s, k2w, v2w) = _inproj(
        x_prompt, cos_p, sin_p, in_wts, True)
    zeros_halo = jnp.zeros((bp, CONV_HALO, C_CONV), F32)
    a_out, conv_tail = _conv_module(glu, glu, zeros_halo, *conv_wts, tt=128, rb=64)
    nch = seq // CMP_STRIDE
    n_cmp = nch - 1
    n_slc = seq // SEL_BLOCK
    cos_c, sin_c = _rope_tables(jnp.arange(nch) * CMP_STRIDE + (CMP_LEN - 1))
    kc2, vc2 = _cmp_prompt(cmp_rows.reshape(bp, nch, CMP_STRIDE * KV_W), (bdk, bdv) + cw_tail, cos_c, sin_c)
    e_tab = jnp.asarray((np.arange(seq)[:, None] // SEL_BLOCK == np.arange(LANES)[None, :]), BF16)
    ov = jnp.asarray(_overlap_t(LANES, nch, 0, n_cmp).T * (np.arange(LANES) < n_slc), BF16)
    gex_np = np.zeros((2 * LANES, 3 * GROUP * LANES), np.float32)
    for k in range(3 * GROUP):
        gex_np[k, k * LANES:(k + 1) * LANES] = 1.0
        gex_np[LANES + k, k * LANES:(k + 1) * LANES] = 1.0
    o_p = _attn_prompt(q, gn, kc2, vc2, k2s, v2s, k2w, v2w, e_tab, ov, jnp.asarray(gex_np, BF16),
                       n_cmp, n_slc, min(N_SEL, n_slc))
    n_tok = bp * seq
    x1 = _merge(o_p.reshape(n_tok, D_MODEL), a_out.reshape(n_tok, D_MODEL), gm.reshape(n_tok, 2 * D_MODEL),
                x_prompt.reshape(n_tok, D_MODEL), wno, wout, 512)
    zeros_fix = jnp.zeros((bp, SUBLANES, 2 * D_FF), F32)
    y_p, ffn_tail = _ffn(x1.reshape(bp, seq, D_MODEL), gffn, wup, wfd, wdn, zeros_fix, zeros_fix, 512, 512)

    kv5 = lambda rows, b, t: rows.reshape(b, t, 2, N_KV, HEAD_DIM)
    out_p = (y_p, kv5(cmp_rows, bp, seq), kv5(slc_rows, bp, seq),
             kv5(win_rows[:, seq - min(WINDOW, seq):], bp, min(WINDOW, seq)),
             conv_tail[:, CONV_HALO - (CONV_K - 1):], ffn_tail[:, :FFN_CONV_K - 1])

    n_s = db * dseq
    pos_s = past_len + (jnp.arange(TOK_TILE) % dseq)
    cos_s, sin_s = _rope_tables(pos_s)
    xs_pad = jnp.pad(x_sample.reshape(1, n_s, D_MODEL), ((0, 0), (0, TOK_TILE - n_s), (0, 0))) if n_s < TOK_TILE \
        else x_sample.reshape(1, n_s, D_MODEL)
    assert xs_pad.shape[1] == TOK_TILE
    (glu_s, q_s, cmp_s, slc_s, win_s, gn_s, gm_s) = [a[0, :n_s] for a in _inproj(xs_pad, cos_s, sin_s, in_wts, False)]
    st32 = jnp.pad(state_conv, ((0, 0), (CONV_HALO - (CONV_K - 1), 0), (0, 0)))
    a_s, conv_tail_s = _conv_module(glu_s.reshape(db, dseq, C_CONV), st32, st32, *conv_wts, tt=dseq, rb=dseq)

    pos_minor = lambda c: c.transpose(0, 2, 3, 4, 1).reshape(c.shape[0], KV_W, c.shape[1])
    cache_cmp_t = pos_minor(cache_cmp_kv)
    cache_slc_t = pos_minor(cache_slc_kv)
    kcs, vcs = _cmp_stream(page_table, cache_cmp_t, _cmp_pair_weights(w_cmp_k), _cmp_pair_weights(w_cmp_v))
    nch_s = past_len // CMP_STRIDE
    n_cmp_s = -(-(past_len + dseq) // CMP_STRIDE) - 1
    n_slc_s = -(-(past_len + dseq) // SEL_BLOCK)
    nsp = -(-n_slc_s // LANES) * LANES
    cos_cs, sin_cs = _rope_tables(jnp.arange(nch_s) * CMP_STRIDE + (CMP_STRIDE - 1))
    ovt_s = jnp.asarray(_overlap_t(nsp, nch_s, 1, n_cmp_s), BF16)
    qbd, oc_s, pen_s = _cmp_sample(kcs, vcs, q_s.reshape(db, dseq, D_MODEL), cw_tail, cos_cs, sin_cs, ovt_s,
                                   past_len, n_slc_s, min(N_SEL, n_slc_s))
    m_s, l_s, acc_s = _slc_stream(page_table, cache_slc_t, qbd, pen_s)
    win3 = cache_win_kv.reshape(db, wlen, KV_W)
    o_s = _finish_sample(m_s, l_s, acc_s, oc_s, qbd, pen_s, slc_s.reshape(db, dseq, KV_W),
                         win_s.reshape(db, dseq, KV_W), pos_minor(cache_win_kv),
                         gn_s.reshape(db, dseq, N_KV * LANES), past_len, past_len // SEL_BLOCK)
    x1_s = _merge(o_s.reshape(n_s, D_MODEL), a_s.reshape(n_s, D_MODEL), gm_s, x_sample.reshape(n_s, D_MODEL),
                  wno, wout, n_s)
    z1 = jnp.zeros((db, dseq - 1, 2 * D_FF), F32)
    fix1 = jnp.concatenate([state_ffn_conv[:, 1:2], z1], axis=1).reshape(1, n_s, 2 * D_FF)
    fix2 = jnp.concatenate([state_ffn_conv, z1[:, 1:]], axis=1).reshape(1, n_s, 2 * D_FF)
    y_s, up_s = _ffn(x1_s.reshape(1, n_s, D_MODEL), gffn, wup, wfd, wdn, fix1, fix2, n_s, dseq)

    win_all = jnp.concatenate([win3, win_s.reshape(db, dseq, KV_W)], axis=1)
    keep = min(WINDOW, past_len + dseq)
    out_s = (y_s.reshape(db, dseq, D_MODEL), kv5(cmp_s, db, dseq), kv5(slc_s, db, dseq),
             kv5(win_all[:, wlen + dseq - keep:], db, keep),
             conv_tail_s[:, CONV_HALO - (CONV_K - 1):],
             up_s.reshape(db, dseq, 2 * D_FF)[:, dseq - (FFN_CONV_K - 1):])
    return (out_p[0], out_s[0]) + out_p[1:] + out_s[1:]
```

```python
import functools

import numpy as np
import jax
import jax.numpy as jnp
from jax import lax
from jax.experimental import pallas as pl
from jax.experimental.pallas import tpu as pltpu

F32 = jnp.float32
BF16 = jnp.bfloat16

D_MODEL = 1024
N_HEADS = 16
HEAD_DIM = 64
N_KV = 4
GROUP = N_HEADS // N_KV
CMP_STRIDE = 16
CMP_LEN = 2 * CMP_STRIDE
SEL_BLOCK = 64
N_SEL = 16
WINDOW = 512
C_CONV = D_MODEL // 2
CONV_K = 31
FFN_CONV_K = 3
D_FF = 2816
ROPE_THETA = 10000.0
EPS = 1e-6
KV_W = 2 * N_KV * HEAD_DIM
K_W = N_KV * HEAD_DIM

LANES = 128
SUBLANES = 8
QBLK = WINDOW
TOK_TILE = 256
ATT_GROUPS = 2
CONV_HALO = 32
VMEM_LIMIT = 56 * 1024 * 1024
FFN_CW = 256
FFN_NC = D_FF // FFN_CW
CMP_PAGES = 32
CMP_GROUP = 8
SLC_PAGES = 32
QROWS = GROUP * N_KV * 8

NEG = -1e30
PEN = -1e9
RANK_FORCED = 1e30
RANK_INVALID = -1e30
RANK_REMOVED = -2e30


def _dot(a, b):
    return jnp.dot(a, b, preferred_element_type=F32)


def _dot_nt(a, b):
    return lax.dot_general(a, b, (((1,), (1,)), ((), ())), preferred_element_type=F32)


def _sigmoid(x):
    return 1.0 / (1.0 + jnp.exp(-x))


def _iota(shape, dim):
    return lax.broadcasted_iota(jnp.int32, shape, dim)


def _params(sem):
    return pltpu.CompilerParams(dimension_semantics=sem, vmem_limit_bytes=VMEM_LIMIT)


def _const_spec(a, ngrid):
    nd = a.ndim
    if ngrid == 1:
        return pl.BlockSpec(a.shape, lambda i: (0,) * nd)
    if ngrid == 2:
        return pl.BlockSpec(a.shape, lambda i, j: (0,) * nd)
    return pl.BlockSpec(a.shape, lambda i, j, k: (0,) * nd)


def _rope_tables(pos):
    half = HEAD_DIM // 2
    inv = ROPE_THETA ** (-jnp.arange(half, dtype=F32) * (2.0 / HEAD_DIM))
    ang = pos.astype(F32)[:, None] * inv[None, :]
    cos, sin = jnp.cos(ang), jnp.sin(ang)
    return jnp.tile(cos, (1, 4)), jnp.tile(jnp.concatenate([-sin, sin], axis=1), (1, 2))


def _head_sum_mats(width):
    heads = width // HEAD_DIM
    lane_head = np.arange(width) // HEAD_DIM
    bd = (lane_head[:, None] == np.arange(LANES)[None, :]).astype(np.float32)
    ex = np.zeros((2 * LANES, width), np.float32)
    ex[:heads] = (np.arange(heads)[:, None] == lane_head[None, :])
    ex[LANES:LANES + heads] = ex[:heads]
    return jnp.asarray(bd, BF16), jnp.asarray(ex, BF16)


def _overlap_t(n_slc_rows, n_cols, col_shift, n_cmp):
    n = np.arange(n_cols) - col_shift
    j = np.arange(n_slc_rows)
    cs = n * CMP_STRIDE
    ss = j * SEL_BLOCK
    ov = (cs[None, :] < ss[:, None] + SEL_BLOCK) & (cs[None, :] + CMP_LEN > ss[:, None])
    ov &= (n[None, :] >= 0) & (n[None, :] < n_cmp)
    return ov.astype(np.float32)


def _head_rmsnorm(x, gain, bd, ex):
    ssum = _dot((x * x).astype(BF16), bd)
    r = lax.rsqrt(ssum * (1.0 / HEAD_DIM) + EPS)
    r_hi = r.astype(BF16)
    r_lo = (r - r_hi.astype(F32)).astype(BF16)
    rfull = _dot(jnp.concatenate([r_hi, r_lo], axis=1), ex)
    return x * rfull * gain


def _rope128(y, cos, sin):
    lane = _iota(y.shape, 1)
    first = (lane & (HEAD_DIM - 1)) < (HEAD_DIM // 2)
    rot = jnp.where(first, pltpu.roll(y, LANES - HEAD_DIM // 2, 1), pltpu.roll(y, HEAD_DIM // 2, 1))
    return y * cos + rot * sin


def _dup_heads(chunk):
    lo = _iota(chunk.shape, 1) < HEAD_DIM
    swapped = pltpu.roll(chunk, HEAD_DIM, 1)
    return jnp.where(lo, chunk, swapped), jnp.where(lo, swapped, chunk)


def _rank_select(val, n_rounds):
    rows = val.shape[0]
    j = _iota(val.shape, 0).astype(F32)
    sel = jnp.zeros(val.shape, F32)
    for _ in range(n_rounds):
        mx = jnp.max(val, axis=0, keepdims=True)
        idx = jnp.min(jnp.where(val == mx, j, float(rows)), axis=0, keepdims=True)
        hit = j == idx
        sel = jnp.where(hit, 1.0, sel)
        val = jnp.where(hit, RANK_REMOVED, val)
    return sel > 0.5


def _inproj_kernel(x_ref, gmix_ref, wu_ref, wq_ref, wkv_ref, wgn_ref, wgm_ref, gq_ref, gks_ref,
                   gkw_ref, bd16_ref, ex16_ref, bd4_ref, ex4_ref, cos_ref, sin_ref,
                   glu_ref, q_ref, cmp_ref, slc_ref, win_ref, gn_ref, gm_ref, *attn_refs):
    x = x_ref[...]
    ms = jnp.mean(x * x, axis=-1, keepdims=True)
    h = (x * lax.rsqrt(ms + EPS) * gmix_ref[...]).astype(BF16)
    cos = cos_ref[...]
    sin = sin_ref[...]

    u = _dot(h, wu_ref[...])
    glu_ref[...] = u[:, :C_CONV] * _sigmoid(u[:, C_CONV:])

    yq = _head_rmsnorm(_dot(h, wq_ref[...]), gq_ref[...], bd16_ref[...], ex16_ref[...])
    scale = HEAD_DIM ** -0.5
    for c in range(D_MODEL // LANES):
        sl = slice(c * LANES, (c + 1) * LANES)
        q_ref[:, sl] = (_rope128(yq[:, sl], cos, sin) * scale).astype(BF16)

    zkv = _dot(h, wkv_ref[...])
    cmp_ref[...] = zkv[:, :KV_W]

    def kv_branch(z, gain_ref, rows_ref, k2_ref, v2_ref):
        yk = _head_rmsnorm(z[:, :K_W], gain_ref[...], bd4_ref[...], ex4_ref[...])
        v = z[:, K_W:]
        rows_ref[:, K_W:] = v
        for c in range(K_W // LANES):
            sl = slice(c * LANES, (c + 1) * LANES)
            kr = _rope128(yk[:, sl], cos, sin)
            rows_ref[:, sl] = kr
            if k2_ref is not None:
                for ref, chunk in ((k2_ref, kr), (v2_ref, v[:, sl])):
                    a2, b2 = _dup_heads(chunk)
                    ref[2 * c] = a2.astype(BF16)
                    ref[2 * c + 1] = b2.astype(BF16)

    if attn_refs:
        k2s_ref, v2s_ref, k2w_ref, v2w_ref = attn_refs
    else:
        k2s_ref = v2s_ref = k2w_ref = v2w_ref = None
    kv_branch(zkv[:, KV_W:2 * KV_W], gks_ref, slc_ref, k2s_ref, v2s_ref)
    kv_branch(zkv[:, 2 * KV_W:], gkw_ref, win_ref, k2w_ref, v2w_ref)

    gn_ref[...] = _sigmoid(_dot(h, wgn_ref[...]))
    gm_ref[...] = _sigmoid(_dot(h, wgm_ref[...]))


def _inproj(x, cos, sin, wts, attn_layouts):
    B, T, _ = x.shape
    tm = TOK_TILE
    assert T % tm == 0
    nt = T // tm
    tok = lambda w: pl.BlockSpec((None, tm, w), lambda b, t: (b, t, 0))
    tab = pl.BlockSpec((tm, LANES), lambda b, t: (t, 0))
    in_specs = [tok(D_MODEL)] + [_const_spec(a, 2) for a in wts] + [tab, tab]
    out_shape = [
        jax.ShapeDtypeStruct((B, T, C_CONV), F32),
        jax.ShapeDtypeStruct((B, T, D_MODEL), BF16),
        jax.ShapeDtypeStruct((B, T, KV_W), F32),
        jax.ShapeDtypeStruct((B, T, KV_W), F32),
        jax.ShapeDtypeStruct((B, T, KV_W), F32),
        jax.ShapeDtypeStruct((B, T, N_KV * LANES), F32),
        jax.ShapeDtypeStruct((B, T, 2 * D_MODEL), F32),
    ]
    out_specs = [tok(C_CONV), tok(D_MODEL), tok(KV_W), tok(KV_W), tok(KV_W), tok(N_KV * LANES),
                 tok(2 * D_MODEL)]
    if attn_layouts:
        dup = jax.ShapeDtypeStruct((B, N_KV, T, LANES), BF16)
        dup_spec = pl.BlockSpec((None, N_KV, tm, LANES), lambda b, t: (b, 0, t, 0))
        out_shape += [dup] * 4
        out_specs += [dup_spec] * 4
    return pl.pallas_call(
        _inproj_kernel,
        grid=(B, nt),
        in_specs=in_specs,
        out_specs=out_specs,
        out_shape=out_shape,
        compiler_params=_params(("parallel", "parallel")),
        name="inproj",
    )(x, *wts, cos, sin)


def _conv_kernel(glu_ref, halo_ref, st_ref, wdw_ref, bdw_ref, gln_ref, bln_ref, wco_ref,
                 a_ref, cst_ref, xc_ref, s_ref, *, tt, rb):
    ti = pl.program_id(1)

    @pl.when(ti == 0)
    def _():
        xc_ref[0:CONV_HALO, :] = st_ref[...]

    @pl.when(ti > 0)
    def _():
        xc_ref[0:CONV_HALO, :] = halo_ref[...]

    xc_ref[CONV_HALO:CONV_HALO + tt, :] = glu_ref[...]
    xc_ref[CONV_HALO + tt:CONV_HALO + tt + SUBLANES, :] = jnp.zeros((SUBLANES, C_CONV), F32)
    first = CONV_HALO - (CONV_K - 1)
    for r0 in range(0, tt, rb):
        acc = jnp.broadcast_to(bdw_ref[...], (rb, C_CONV))
        for phi in range(SUBLANES):
            y = None
            for u in range(first, first + CONV_K):
                if u % SUBLANES == phi:
                    term = wdw_ref[u - first:u - first + 1, :] * xc_ref[r0 + u - phi:r0 + u - phi + rb + SUBLANES, :]
                    y = term if y is None else y + term
            acc = acc + y[phi:phi + rb]
        mu = jnp.mean(acc, axis=-1, keepdims=True)
        d = acc - mu
        var = jnp.mean(d * d, axis=-1, keepdims=True)
        y = d * lax.rsqrt(var + EPS) * gln_ref[...] + bln_ref[...]
        s_ref[r0:r0 + rb, :] = y * _sigmoid(y)
    a_ref[...] = _dot(s_ref[...].astype(BF16), wco_ref[...])
    cst_ref[...] = xc_ref[tt:tt + CONV_HALO, :]


def _conv_module(glu, halo_src, state32, wdw, bdw, gln, bln, wco, tt, rb):
    B, T, _ = glu.shape
    nt = T // tt
    hb = max(tt // CONV_HALO, 1)
    return pl.pallas_call(
        functools.partial(_conv_kernel, tt=tt, rb=rb),
        grid=(B, nt),
        in_specs=[
            pl.BlockSpec((None, tt, C_CONV), lambda b, t: (b, t, 0)),
            pl.BlockSpec((None, CONV_HALO, C_CONV), lambda b, t: (b, jnp.maximum(t * hb - 1, 0), 0)),
            pl.BlockSpec((None, CONV_HALO, C_CONV), lambda b, t: (b, 0, 0)),
            _const_spec(wdw, 2), _const_spec(bdw, 2), _const_spec(gln, 2), _const_spec(bln, 2),
            _const_spec(wco, 2),
        ],
        out_specs=[
            pl.BlockSpec((None, tt, D_MODEL), lambda b, t: (b, t, 0)),
            pl.BlockSpec((None, CONV_HALO, C_CONV), lambda b, t: (b, 0, 0)),
        ],
        out_shape=[
            jax.ShapeDtypeStruct((B, T, D_MODEL), F32),
            jax.ShapeDtypeStruct((B, CONV_HALO, C_CONV), F32),
        ],
        scratch_shapes=[pltpu.VMEM((CONV_HALO + tt + SUBLANES, C_CONV), F32), pltpu.VMEM((tt, C_CONV), F32)],
        compiler_params=_params(("parallel", "arbitrary")),
        name="conv_module",
    )(glu, halo_src, state32, wdw, bdw, gln, bln, wco)


def _merge_kernel(o_ref, a_ref, gm_ref, x_ref, wno_ref, wout_ref, x1_ref):
    b_out = _dot(o_ref[...], wno_ref[...])
    gm = gm_ref[...]
    m = gm[:, :D_MODEL] * a_ref[...] + gm[:, D_MODEL:] * b_out
    x1_ref[...] = x_ref[...] + _dot(m.astype(BF16), wout_ref[...])


def _merge(o, a_out, gm, x, wno, wout, tm):
    N = x.shape[0]
    row = lambda w: pl.BlockSpec((tm, w), lambda i: (i, 0))
    return pl.pallas_call(
        _merge_kernel,
        grid=(N // tm,),
        in_specs=[row(D_MODEL), row(D_MODEL), row(2 * D_MODEL), row(D_MODEL),
                  _const_spec(wno, 1), _const_spec(wout, 1)],
        out_specs=row(D_MODEL),
        out_shape=jax.ShapeDtypeStruct((N, D_MODEL), F32),
        compiler_params=_params(("parallel",)),
        name="merge",
    )(o, a_out, gm, x, wno, wout)


def _ffn_kernel(x1_ref, gn_ref, wup_ref, wdw_ref, wdn_ref, fix1_ref, fix2_ref,
                y_ref, st_ref, prev_ref, acc_ref, *, tm, seg):
    streaming = seg == tm
    x1 = x1_ref[...]
    ms = jnp.mean(x1 * x1, axis=-1, keepdims=True)
    h = (x1 * lax.rsqrt(ms + EPS) * gn_ref[...]).astype(BF16)
    row = _iota((tm, FFN_CW), 0) & (seg - 1)

    if streaming:
        @pl.when(pl.program_id(1) == 0)
        def _():
            prev_ref[...] = fix2_ref[...]

    def conv3(c):
        cs = slice(c * FFN_CW, (c + 1) * FFN_CW)
        up = _dot(h, wup_ref[c])
        r1 = pltpu.roll(up, 1, 0)
        r2 = pltpu.roll(up, 2, 0)
        if streaming:
            p0 = prev_ref[0:1, cs]
            p1 = prev_ref[1:2, cs]
            s1 = jnp.where(row == 0, p1, r1)
            s2 = jnp.where(row == 0, p0, jnp.where(row == 1, p1, r2))
            prev_ref[0:2, cs] = up[tm - 2:tm, :]
        else:
            s1 = jnp.where(row == 0, fix1_ref[:, cs], r1)
            s2 = jnp.where(row < 2, fix2_ref[:, cs], r2)
            st_ref[:, cs] = up
        w = wdw_ref[c]
        return w[0:1] * s2 + w[1:2] * s1 + w[2:3] * up + w[3:4]

    acc_ref[...] = x1
    for c in range(FFN_NC):
        gate = conv3(c)
        val = conv3(FFN_NC + c)
        act = gate * _sigmoid(gate) * val
        acc_ref[...] += _dot(act.astype(BF16), wdn_ref[c])
    y_ref[...] = acc_ref[...]
    if streaming:
        st_ref[...] = prev_ref[...]


def _ffn(x1, gn, wup, wdw, wdn, fix1, fix2, tm, seg):
    B, T, _ = x1.shape
    nt = T // tm
    streaming = seg == tm
    tok = lambda w: pl.BlockSpec((None, tm, w), lambda b, t: (b, t, 0))
    if streaming:
        fix_specs = [pl.BlockSpec((None, SUBLANES, 2 * D_FF), lambda b, t: (b, 0, 0))] * 2
        st_shape = jax.ShapeDtypeStruct((B, SUBLANES, 2 * D_FF), F32)
        st_spec = pl.BlockSpec((None, SUBLANES, 2 * D_FF), lambda b, t: (b, 0, 0))
    else:
        fix_specs = [tok(2 * D_FF)] * 2
        st_shape = jax.ShapeDtypeStruct((B, T, 2 * D_FF), F32)
        st_spec = tok(2 * D_FF)
    return pl.pallas_call(
        functools.partial(_ffn_kernel, tm=tm, seg=seg),
        grid=(B, nt),
        in_specs=[tok(D_MODEL), _const_spec(gn, 2), _const_spec(wup, 2), _const_spec(wdw, 2),
                  _const_spec(wdn, 2)] + fix_specs,
        out_specs=[tok(D_MODEL), st_spec],
        out_shape=[jax.ShapeDtypeStruct((B, T, D_MODEL), F32), st_shape],
        scratch_shapes=[pltpu.VMEM((SUBLANES, 2 * D_FF), F32), pltpu.VMEM((tm, D_MODEL), F32)],
        compiler_params=_params(("parallel", "arbitrary")),
        name="conv_ffn",
    )(x1, gn, wup, wdw, wdn, fix1, fix2)


def _cmp_prompt_kernel(rows_ref, bdk_ref, bdv_ref, pek_ref, wk4_ref, pev_ref, wv4_ref, gk_ref,
                       bd4_ref, ex4_ref, cos_ref, sin_ref, kc2_ref, vc2_ref, *, nch):
    acc_k = jnp.zeros((nch, 2 * K_W), F32)
    acc_v = jnp.zeros((nch, 2 * K_W), F32)
    for l in range(CMP_STRIDE):
        acc_k += _dot(rows_ref[:, l * KV_W:l * KV_W + K_W].astype(BF16), bdk_ref[l])
        acc_v += _dot(rows_ref[:, l * KV_W + K_W:(l + 1) * KV_W].astype(BF16), bdv_ref[l])
    pos_k = _dot(pek_ref[...], wk4_ref[...])[0:1]
    pos_v = _dot(pev_ref[...], wv4_ref[...])[0:1]
    kc = acc_k[:, :K_W] + pltpu.roll(acc_k[:, K_W:], nch - 1, 0) + pos_k
    vc = acc_v[:, :K_W] + pltpu.roll(acc_v[:, K_W:], nch - 1, 0) + pos_v
    yk = _head_rmsnorm(kc, gk_ref[...], bd4_ref[...], ex4_ref[...])
    for c in range(K_W // LANES):
        sl = slice(c * LANES, (c + 1) * LANES)
        for ref, chunk in ((kc2_ref, _rope128(yk[:, sl], cos_ref[...], sin_ref[...])), (vc2_ref, vc[:, sl])):
            a2, b2 = _dup_heads(chunk)
            ref[2 * c] = a2.astype(BF16)
            ref[2 * c + 1] = b2.astype(BF16)


def _cmp_prompt(rows, cw, cos, sin):
    B, nch, _ = rows.shape
    return pl.pallas_call(
        functools.partial(_cmp_prompt_kernel, nch=nch),
        grid=(B,),
        in_specs=[pl.BlockSpec((None, nch, CMP_STRIDE * KV_W), lambda b: (b, 0, 0))] + [_const_spec(a, 1) for a in cw]
        + [_const_spec(cos, 1), _const_spec(sin, 1)],
        out_specs=[pl.BlockSpec((None, N_KV, nch, LANES), lambda b: (b, 0, 0, 0))] * 2,
        out_shape=[jax.ShapeDtypeStruct((B, N_KV, nch, LANES), BF16)] * 2,
        compiler_params=_params(("parallel",)),
        name="cmp_prompt",
    )(rows, *cw, cos, sin)


def _attn_prompt_kernel(q_ref, gates_ref, kc2_ref, vc2_ref, k2s_ref, v2s_ref, k2w_ref, v2w_ref,
                        e_ref, ov_ref, gex_ref, o_ref, m_ref, l_ref, acc_ref, mw_ref, lw_ref, accw_ref,
                        *, n_cmp, n_slc, n_top):
    qi = pl.program_id(2)
    qs = qi * QBLK
    ng, nch = kc2_ref.shape[0], kc2_ref.shape[1]
    lo = _iota((QBLK, LANES), 1) < HEAD_DIM
    zero = jnp.zeros((QBLK, LANES), BF16)
    last = jnp.minimum((qs + _iota((QBLK, nch), 0) - (CMP_LEN - 1)) >> 4, n_cmp - 1)
    cmask = _iota((QBLK, nch), 1) <= last

    def masked(x, fill):
        return jnp.concatenate([jnp.where(cmask, x[r * QBLK:(r + 1) * QBLK], fill) for r in range(GROUP)], axis=0)

    kl = _iota((QBLK, QBLK), 1)
    tl = _iota((QBLK, QBLK), 0)
    tri = {"causal": kl <= tl, "band": kl >= tl}
    slc_state = lambda gi: (m_ref.at[gi], l_ref.at[gi], acc_ref.at[gi])
    win_state = lambda gi: (mw_ref.at[gi], lw_ref.at[gi], accw_ref.at[gi])

    def reset(state):
        m_st, l_st, acc_st = state
        m_st[...] = jnp.full(m_st.shape, NEG, F32)
        l_st[...] = jnp.zeros(l_st.shape, F32)
        acc_st[...] = jnp.zeros(acc_st.shape, F32)

    def attend(state, kblk, v2blk, q_rows, mode):
        m_st, l_st, acc_st = state
        sc = _dot_nt(q_rows, kblk)
        if mode != "none":
            sc = jnp.concatenate([jnp.where(tri[mode], sc[r * QBLK:(r + 1) * QBLK], NEG)
                                  for r in range(GROUP)], axis=0)
        m_old = m_st[...]
        m_new = jnp.maximum(m_old, jnp.max(sc, axis=1, keepdims=True))
        alpha = jnp.exp(m_old - m_new)
        p = jnp.exp(sc - jnp.concatenate([m_new] * (sc.shape[1] // LANES), axis=1))
        l_st[...] = alpha * l_st[...] + jnp.sum(p, axis=1, keepdims=True)
        acc_st[...] = alpha * acc_st[...] + _dot(p.astype(BF16), v2blk)
        m_st[...] = m_new

    def blk(ref, kb):
        return ref[pl.ds(pl.multiple_of(kb * QBLK, QBLK), QBLK), :]

    def front(gi):
        qm = []
        for p in range(GROUP // 2):
            qp = q_ref[:, gi * K_W + p * LANES:gi * K_W + (p + 1) * LANES]
            qm += [jnp.where(lo, qp, zero), jnp.where(lo, zero, qp)]
        q_all = jnp.concatenate(qm, axis=0)

        s = masked(_dot_nt(q_all, kc2_ref[gi]), NEG)
        mx = jnp.max(s, axis=1, keepdims=True)
        e = masked(jnp.exp(s - mx), 0.0)
        p_cmp = (e * (1.0 / jnp.maximum(jnp.sum(e, axis=1, keepdims=True), 1e-30))).astype(BF16)
        o_cmp = _dot(p_cmp, vc2_ref[gi])
        imp4 = _dot(p_cmp, ov_ref[...])
        imp = imp4[0:QBLK]
        for r in range(1, GROUP):
            imp = imp + imp4[r * QBLK:(r + 1) * QBLK]

        reset(win_state(gi))
        attend(win_state(gi), blk(k2w_ref.at[gi], qi), blk(v2w_ref.at[gi], qi), q_all, "causal")

        imp_t = imp.T[0:n_slc]
        j = _iota((n_slc, QBLK), 0)
        tq = qs + _iota((n_slc, QBLK), 1)
        cur = tq >> 6
        forced = jnp.logical_or(j == 0, jnp.logical_or(j == cur, j == cur - 1))
        rank = jnp.where(forced, RANK_FORCED, jnp.where(j * SEL_BLOCK <= tq, imp_t, RANK_INVALID))
        pen_t = jnp.where(_rank_select(rank, n_top), 0.0, PEN)
        if n_slc < LANES:
            pen_t = jnp.concatenate([pen_t, jnp.zeros((LANES - n_slc, QBLK), F32)], axis=0)
        pen = pen_t.T.astype(BF16)
        qa_all = jnp.concatenate([q_all, jnp.concatenate([pen] * GROUP, axis=0)], axis=1)
        reset(slc_state(gi))
        return q_all, qa_all, o_cmp

    parts = [front(gi) for gi in range(ng)]

    def slc_step(kb, mode):
        e_blk = blk(e_ref, kb)
        for gi in range(ng):
            keys = jnp.concatenate([blk(k2s_ref.at[gi], kb), e_blk], axis=1)
            attend(slc_state(gi), keys, blk(v2s_ref.at[gi], kb), parts[gi][1], mode)

    def slc_body(kb, carry):
        slc_step(kb, "none")
        return carry

    lax.fori_loop(0, qi, slc_body, 0)
    slc_step(qi, "causal")

    @pl.when(qi >= 1)
    def _():
        for gi in range(ng):
            attend(win_state(gi), blk(k2w_ref.at[gi], qi - 1), blk(v2w_ref.at[gi], qi - 1), parts[gi][0], "band")

    for gi in range(ng):
        o_cmp = parts[gi][2]
        o_slc = acc_ref[gi] / l_ref[gi]
        o_win = accw_ref[gi] / lw_ref[gi]
        gt = gates_ref[:, gi * LANES:(gi + 1) * LANES]
        g_hi = gt.astype(BF16)
        g_lo = (gt - g_hi.astype(F32)).astype(BF16)
        gex = _dot(jnp.concatenate([g_hi, g_lo], axis=1), gex_ref[...])
        mixed = []
        for r in range(GROUP):
            rs = slice(r * QBLK, (r + 1) * QBLK)
            gl = lambda k: gex[:, (3 * r + k) * LANES:(3 * r + k + 1) * LANES]
            mixed.append(gl(0) * o_cmp[rs] + gl(1) * o_slc[rs] + gl(2) * o_win[rs])
        for p in range(GROUP // 2):
            o_ref[:, gi * K_W + p * LANES:gi * K_W + (p + 1) * LANES] = jnp.where(
                lo, mixed[2 * p], mixed[2 * p + 1]).astype(BF16)


def _attn_prompt(q, gates, kc2, vc2, k2s, v2s, k2w, v2w, e_tab, ov, gex, n_cmp, n_slc, n_top):
    B, T, _ = q.shape
    nq = T // QBLK
    nch = kc2.shape[2]
    rows = GROUP * QBLK
    ng = ATT_GROUPS
    per_bg = lambda n: pl.BlockSpec((None, ng, n, LANES), lambda b, g, i: (b, g, 0, 0))
    return pl.pallas_call(
        functools.partial(_attn_prompt_kernel, n_cmp=n_cmp, n_slc=n_slc, n_top=n_top),
        grid=(B, N_KV // ng, nq),
        in_specs=[
            pl.BlockSpec((None, QBLK, ng * K_W), lambda b, g, i: (b, i, g)),
            pl.BlockSpec((None, QBLK, ng * LANES), lambda b, g, i: (b, i, g)),
            per_bg(nch), per_bg(nch), per_bg(T), per_bg(T), per_bg(T), per_bg(T),
            _const_spec(e_tab, 3), _const_spec(ov, 3), _const_spec(gex, 3),
        ],
        out_specs=pl.BlockSpec((None, QBLK, ng * K_W), lambda b, g, i: (b, i, g)),
        out_shape=jax.ShapeDtypeStruct((B, T, D_MODEL), BF16),
        scratch_shapes=[pltpu.VMEM((ng, rows, LANES), F32)] * 6,
        compiler_params=_params(("parallel", "parallel", "arbitrary")),
        name="attn_prompt",
    )(q, gates, kc2, vc2, k2s, v2s, k2w, v2w, e_tab, ov, gex)


def _cmp_stream_kernel(pt_ref, *refs, m):
    pages = refs[:CMP_PAGES]
    bdk_ref, bdv_ref, kc_ref, vc_ref, fs_ref, slab_ref = refs[CMP_PAGES:]
    s = pl.program_id(1)
    per_page = m // CMP_PAGES
    nslab = KV_W // LANES

    @pl.when(s == 0)
    def _():
        fs_ref[0:SUBLANES, :] = jnp.zeros((SUBLANES, 2 * K_W), F32)

    half = K_W // 2

    def page_group(k0, k1):
        mg = (k1 - k0) * per_page
        for k in range(k0, k1):
            for c in range(nslab):
                slab_ref[k, c] = pages[k][c * LANES:(c + 1) * LANES, :].T

        def rows_of(c, l):
            return jnp.concatenate([slab_ref[k, c, pl.ds(l, per_page, stride=CMP_STRIDE), :]
                                    for k in range(k0, k1)], axis=0)

        def pair_rows(c, l):
            return jnp.concatenate([rows_of(c, l), rows_of(c, l + CMP_STRIDE // 2)], axis=1)

        acc_k = jnp.zeros((2 * mg, K_W), F32)
        acc_v = jnp.zeros((2 * mg, K_W), F32)
        for l in range(CMP_STRIDE // 2):
            xk = jnp.concatenate([pair_rows(0, l), pair_rows(1, l)], axis=0)
            xv = jnp.concatenate([pair_rows(2, l), pair_rows(3, l)], axis=0)
            acc_k += _dot(xk.astype(BF16), bdk_ref[l])
            acc_v += _dot(xv.astype(BF16), bdv_ref[l])
        return (jnp.concatenate([acc_k[:mg, :half], acc_k[mg:, :half], acc_v[:mg, :half], acc_v[mg:, :half]], axis=1),
                jnp.concatenate([acc_k[:mg, half:], acc_k[mg:, half:]], axis=1),
                jnp.concatenate([acc_v[:mg, half:], acc_v[mg:, half:]], axis=1))

    groups = [page_group(k0, k0 + CMP_GROUP) for k0 in range(0, CMP_PAGES, CMP_GROUP)]
    first, second_k, second_v = [jnp.concatenate([g[i] for g in groups], axis=0) for i in range(3)]
    fs_ref[SUBLANES:SUBLANES + m, :] = first
    shifted = fs_ref[SUBLANES - 1:SUBLANES - 1 + m, :]
    kc_ref[...] = shifted[:, :K_W] + second_k
    vc_ref[...] = shifted[:, K_W:] + second_v
    fs_ref[SUBLANES - 1:SUBLANES, :] = fs_ref[SUBLANES - 1 + m:SUBLANES + m, :]


def _cmp_stream(page_table, cache_t, bdk, bdv):
    db, n_pages = page_table.shape
    page = cache_t.shape[2]
    per_page = page // CMP_STRIDE
    m = CMP_PAGES * per_page
    nch = n_pages * per_page
    nsteps = n_pages // CMP_PAGES

    def page_spec(k):
        return pl.BlockSpec((None, KV_W, page), lambda b, s, pt: (pt[b, s * CMP_PAGES + k], 0, 0))

    grid_spec = pltpu.PrefetchScalarGridSpec(
        num_scalar_prefetch=1,
        grid=(db, nsteps),
        in_specs=[page_spec(k) for k in range(CMP_PAGES)]
        + [pl.BlockSpec(bdk.shape, lambda b, s, pt: (0, 0, 0)), pl.BlockSpec(bdv.shape, lambda b, s, pt: (0, 0, 0))],
        out_specs=[pl.BlockSpec((None, m, K_W), lambda b, s, pt: (b, s, 0))] * 2,
        scratch_shapes=[pltpu.VMEM((SUBLANES + m, 2 * K_W), F32),
                        pltpu.VMEM((CMP_PAGES, KV_W // LANES, page, LANES), F32)],
    )
    return pl.pallas_call(
        functools.partial(_cmp_stream_kernel, m=m),
        grid_spec=grid_spec,
        out_shape=[jax.ShapeDtypeStruct((db, nch, K_W), F32)] * 2,
        compiler_params=_params(("parallel", "arbitrary")),
        name="cmp_stream",
    )(page_table, *([cache_t] * CMP_PAGES), bdk, bdv)


def _block_diag_q(q):
    lo = _iota((8, LANES), 1) < HEAD_DIM
    zero = jnp.zeros((8, LANES), BF16)
    blocks = []
    for r in range(GROUP):
        for g in range(N_KV):
            hh = g * GROUP + r
            pair = q[:, (hh // 2) * LANES:(hh // 2 + 1) * LANES]
            if hh % 2 != g % 2:
                pair = pltpu.roll(pair.astype(F32), HEAD_DIM, 1).astype(BF16)
            keep = jnp.where(lo, pair, zero) if g % 2 == 0 else jnp.where(lo, zero, pair)
            blocks.append(jnp.concatenate([keep, zero] if g < 2 else [zero, keep], axis=1))
    return jnp.concatenate(blocks, axis=0)


def _cmp_sample_kernel(kcs_ref, vcs_ref, q_ref, pek_ref, wk4_ref, pev_ref, wv4_ref, gk_ref, bd4_ref,
                       ex4_ref, cos_ref, sin_ref, ovt_ref, qbd_ref, oc_ref, pen_ref,
                       *, past_len, n_slc, n_top):
    nch = kcs_ref.shape[0]
    pos_k = _dot(pek_ref[...], wk4_ref[...])[0:1]
    pos_v = _dot(pev_ref[...], wv4_ref[...])[0:1]
    yk = _head_rmsnorm(kcs_ref[...] + pos_k, gk_ref[...], bd4_ref[...], ex4_ref[...])
    kcn = jnp.concatenate([_rope128(yk[:, c * LANES:(c + 1) * LANES], cos_ref[...], sin_ref[...])
                           for c in range(K_W // LANES)], axis=1).astype(BF16)
    vc = (vcs_ref[...] + pos_v).astype(BF16)
    qbd = _block_diag_q(q_ref[...])
    qbd_ref[...] = qbd

    s = _dot_nt(qbd, kcn)
    mi = _iota((QROWS, nch), 1)
    t_row = past_len + (_iota((QROWS, nch), 0) & 7)
    mask = jnp.logical_and(mi * CMP_STRIDE + (CMP_STRIDE - 1) <= t_row, mi >= 1)
    s = jnp.where(mask, s, NEG)
    mx = jnp.max(s, axis=1, keepdims=True)
    e = jnp.where(mask, jnp.exp(s - mx), 0.0)
    p = (e / jnp.maximum(jnp.sum(e, axis=1, keepdims=True), 1e-30)).astype(BF16)
    oc_ref[...] = _dot(p, vc)

    imp = _dot_nt(ovt_ref[...], p)
    imp = imp + pltpu.roll(imp, 32, 1) + pltpu.roll(imp, 64, 1) + pltpu.roll(imp, 96, 1)
    nsp = imp.shape[0]
    j = _iota((nsp, QROWS), 0)
    t_lane = past_len + (_iota((nsp, QROWS), 1) & 7)
    cur = t_lane >> 6
    forced = jnp.logical_or(j == 0, jnp.logical_or(j == cur, j == cur - 1))
    rank = jnp.where(forced, RANK_FORCED, jnp.where(j * SEL_BLOCK <= t_lane, imp, RANK_INVALID))
    rank = jnp.where(j < n_slc, rank, RANK_REMOVED)
    sel = _rank_select(rank, n_top)
    pen_t = jnp.where(sel, 0.0, PEN)
    for blk in range(nsp // LANES):
        pen_ref[blk] = pen_t[blk * LANES:(blk + 1) * LANES, :].T.astype(BF16)


def _cmp_sample(kcs, vcs, q, cw_tail, cos, sin, ovt, past_len, n_slc, n_top):
    db, nch, _ = kcs.shape
    nsp = ovt.shape[0]
    per_b = lambda shape: pl.BlockSpec((None,) + shape, lambda b: (b,) + (0,) * len(shape))
    return pl.pallas_call(
        functools.partial(_cmp_sample_kernel, past_len=past_len, n_slc=n_slc, n_top=n_top),
        grid=(db,),
        in_specs=[per_b((nch, K_W)), per_b((nch, K_W)), per_b((8, D_MODEL))]
        + [_const_spec(a, 1) for a in cw_tail] + [_const_spec(cos, 1), _const_spec(sin, 1), _const_spec(ovt, 1)],
        out_specs=[per_b((QROWS, K_W)), per_b((QROWS, K_W)), per_b((nsp // LANES, QROWS, LANES))],
        out_shape=[jax.ShapeDtypeStruct((db, QROWS, K_W), BF16),
                   jax.ShapeDtypeStruct((db, QROWS, K_W), F32),
                   jax.ShapeDtypeStruct((db, nsp // LANES, QROWS, LANES), BF16)],
        compiler_params=_params(("parallel",)),
        name="cmp_sample",
    )(kcs, vcs, q, *cw_tail, cos, sin, ovt)


def _slc_stream_kernel(pt_ref, *refs):
    pages = refs[:SLC_PAGES]
    qbd_ref, pen_ref, m_out, l_out, acc_out, m_ref, l_ref, acc_ref = refs[SLC_PAGES:]
    s = pl.program_id(1)
    page = pages[0].shape[1]
    nk = SLC_PAGES * page
    blocks_per_step = nk // SEL_BLOCK

    @pl.when(s == 0)
    def _():
        m_ref[...] = jnp.full(m_ref.shape, NEG, F32)
        l_ref[...] = jnp.zeros(l_ref.shape, F32)
        acc_ref[...] = jnp.zeros(acc_ref.shape, F32)

    k_t = jnp.concatenate([pg[0:K_W, :] for pg in pages], axis=1).astype(BF16)
    v_t = jnp.concatenate([pg[K_W:, :] for pg in pages], axis=1).astype(BF16)
    steps_per_lane_block = LANES // blocks_per_step
    pen = pen_ref[s // steps_per_lane_block]
    jrow = _iota((LANES, nk), 0)
    blk = (s % steps_per_lane_block) * blocks_per_step + (_iota((LANES, nk), 1) >> 6)
    expand = jnp.where(jrow == blk, 1.0, 0.0).astype(BF16)
    sc = _dot(qbd_ref[...], k_t) + _dot(pen, expand)
    m_old = m_ref[...]
    m_new = jnp.maximum(m_old, jnp.max(sc, axis=1, keepdims=True))
    alpha = jnp.exp(m_old - m_new)
    p = jnp.exp(sc - m_new)
    l_ref[...] = alpha * l_ref[...] + jnp.sum(p, axis=1, keepdims=True)
    acc_ref[...] = alpha * acc_ref[...] + _dot_nt(p.astype(BF16), v_t)
    m_ref[...] = m_new

    @pl.when(s == pl.num_programs(1) - 1)
    def _():
        m_out[...] = jnp.broadcast_to(m_ref[...], m_out.shape)
        l_out[...] = jnp.broadcast_to(l_ref[...], l_out.shape)
        acc_out[...] = acc_ref[...]


def _slc_stream(page_table, cache_t, qbd, pen):
    db, n_pages = page_table.shape
    page = cache_t.shape[2]
    nsteps = n_pages // SLC_PAGES

    def page_spec(k):
        return pl.BlockSpec((None, KV_W, page), lambda b, s, pt: (pt[b, s * SLC_PAGES + k], 0, 0))

    per_b = lambda shape: pl.BlockSpec((None,) + shape, lambda b, s, pt: (b,) + (0,) * len(shape))
    grid_spec = pltpu.PrefetchScalarGridSpec(
        num_scalar_prefetch=1,
        grid=(db, nsteps),
        in_specs=[page_spec(k) for k in range(SLC_PAGES)] + [per_b((QROWS, K_W)), per_b(pen.shape[1:])],
        out_specs=[per_b((QROWS, LANES)), per_b((QROWS, LANES)), per_b((QROWS, K_W))],
        scratch_shapes=[pltpu.VMEM((QROWS, 1), F32), pltpu.VMEM((QROWS, 1), F32), pltpu.VMEM((QROWS, K_W), F32)],
    )
    return pl.pallas_call(
        _slc_stream_kernel,
        grid_spec=grid_spec,
        out_shape=[jax.ShapeDtypeStruct((db, QROWS, LANES), F32), jax.ShapeDtypeStruct((db, QROWS, LANES), F32),
                   jax.ShapeDtypeStruct((db, QROWS, K_W), F32)],
        compiler_params=_params(("parallel", "arbitrary")),
        name="slc_stream",
    )(page_table, *([cache_t] * SLC_PAGES), qbd, pen)


def _finish_sample_kernel(m_ref, l_ref, acc_ref, oc_ref, qbd_ref, pen_ref, slc_new_ref, win_new_ref,
                          win_ref, gates_ref, o_ref, *, past_len, new_blk):
    qbd = qbd_ref[...]
    tok = _iota((QROWS, LANES), 0) & 7
    col = _iota((QROWS, LANES), 1)
    pad = jnp.zeros((LANES - 8, K_W), F32)

    kn = jnp.concatenate([slc_new_ref[:, 0:K_W], pad], axis=0).astype(BF16)
    vn = jnp.concatenate([slc_new_ref[:, K_W:], pad], axis=0).astype(BF16)
    pen_col = pen_ref[new_blk // LANES][:, new_blk % LANES:new_blk % LANES + 1].astype(F32)
    sn = jnp.where(col <= tok, _dot_nt(qbd, kn) + pen_col, NEG)
    m_old = m_ref[:, 0:1]
    m_new = jnp.maximum(m_old, jnp.max(sn, axis=1, keepdims=True))
    alpha = jnp.exp(m_old - m_new)
    pn = jnp.exp(sn - m_new)
    l_new = alpha * l_ref[:, 0:1] + jnp.sum(pn, axis=1, keepdims=True)
    o_slc = (alpha * acc_ref[...] + _dot(pn.astype(BF16), vn)) / l_new

    wlen = win_ref.shape[1]
    new_t = jnp.concatenate([win_new_ref[...], jnp.zeros((LANES - 8, KV_W), F32)], axis=0).T
    kw_t = jnp.concatenate([win_ref[0:K_W, :], new_t[0:K_W]], axis=1).astype(BF16)
    vw_t = jnp.concatenate([win_ref[K_W:, :], new_t[K_W:]], axis=1).astype(BF16)
    nkw = wlen + LANES
    idx = _iota((QROWS, nkw), 1)
    t_row = past_len + (_iota((QROWS, nkw), 0) & 7)
    kpos = past_len - wlen + idx
    dist = t_row - kpos
    wmask = jnp.logical_and(jnp.logical_and(dist >= 0, dist <= WINDOW),
                            jnp.logical_and(kpos >= 0, idx < wlen + 8))
    sw = jnp.where(wmask, _dot(qbd, kw_t), NEG)
    mw = jnp.max(sw, axis=1, keepdims=True)
    ew = jnp.where(wmask, jnp.exp(sw - mw), 0.0)
    pw = ew / jnp.maximum(jnp.sum(ew, axis=1, keepdims=True), 1e-30)
    o_win = _dot_nt(pw.astype(BF16), vw_t)

    o_cmp = oc_ref[...]
    gates = gates_ref[...]
    lo = _iota((8, LANES), 1) < HEAD_DIM
    for c in range(D_MODEL // LANES):
        g = c // 2
        halves = []
        for e in range(2):
            r = 2 * (c % 2) + e
            rows = slice(r * 32 + g * 8, r * 32 + g * 8 + 8)
            lanes = slice((g // 2) * LANES, (g // 2 + 1) * LANES)
            base = g * LANES + 3 * r
            blk = (gates[:, base:base + 1] * o_cmp[rows, lanes] + gates[:, base + 1:base + 2] * o_slc[rows, lanes]
                   + gates[:, base + 2:base + 3] * o_win[rows, lanes])
            if g % 2 != e:
                blk = pltpu.roll(blk, HEAD_DIM, 1)
            halves.append(blk)
        o_ref[:, c * LANES:(c + 1) * LANES] = jnp.where(lo, halves[0], halves[1]).astype(BF16)


def _finish_sample(m, l, acc, oc, qbd, pen, slc_new, win_new, win3, gates, past_len, new_blk):
    db = m.shape[0]
    per_b = lambda shape: pl.BlockSpec((None,) + shape, lambda b: (b,) + (0,) * len(shape))
    ins = (m, l, acc, oc, qbd, pen, slc_new, win_new, win3, gates)
    return pl.pallas_call(
        functools.partial(_finish_sample_kernel, past_len=past_len, new_blk=new_blk),
        grid=(db,),
        in_specs=[per_b(a.shape[1:]) for a in ins],
        out_specs=per_b((8, D_MODEL)),
        out_shape=jax.ShapeDtypeStruct((db, 8, D_MODEL), BF16),
        compiler_params=_params(("parallel",)),
        name="finish_sample",
    )(*ins)


def _cmp_weights(w_c):
    eye = jnp.eye(N_KV, dtype=F32)
    bd = jnp.einsum("gh,lde->lgdhe", eye, w_c).reshape(CMP_LEN, K_W, K_W)
    return jnp.concatenate([bd[:CMP_STRIDE], bd[CMP_STRIDE:]], axis=2).astype(BF16)


def _cmp_pair_weights(w_c):
    eye = jnp.eye(2, dtype=F32)
    bd = jnp.einsum("gh,lde->lgdhe", eye, w_c).reshape(CMP_LEN, LANES, LANES)
    fs = jnp.concatenate([bd[:CMP_STRIDE], bd[CMP_STRIDE:]], axis=2)
    return jnp.concatenate([fs[:CMP_STRIDE // 2], fs[CMP_STRIDE // 2:]], axis=1).astype(BF16)


def _pos_operands(pe, w_c):
    pe_rows = jnp.tile(pe.reshape(1, CMP_LEN * HEAD_DIM), (SUBLANES, 1)).astype(BF16)
    w4 = jnp.tile(w_c.reshape(CMP_LEN * HEAD_DIM, HEAD_DIM), (1, N_KV)).astype(BF16)
    return pe_rows, w4


def _tile_gain(g, width):
    return jnp.tile(g, width // g.shape[0]).reshape(1, width).astype(F32)


def kernel(x_prompt, x_sample, cache_cmp_kv, cache_slc_kv, cache_win_kv, state_conv, state_ffn_conv,
           page_table, g_norm_mix, w_in, w_dw, b_dw, g_ln_conv, b_ln_conv, w_conv_out, g_q, g_k_cmp,
           g_k_slc, g_k_win, w_cmp_k, w_cmp_v, pe_cmp_k, pe_cmp_v, w_nsa_out, w_out, g_norm_ffn, w_up,
           w_ffn_dw, b_ffn_dw, w_down):
    bp, seq, _ = x_prompt.shape
    db, dseq, _ = x_sample.shape
    n_pool, page = cache_cmp_kv.shape[0], cache_cmp_kv.shape[1]
    n_pages = page_table.shape[1]
    past_len = n_pages * page
    wlen = cache_win_kv.shape[1]
    assert dseq == 8 and seq % QBLK == 0 and seq // SEL_BLOCK <= LANES and QBLK == WINDOW
    assert past_len % SEL_BLOCK == 0 and wlen == WINDOW and past_len >= WINDOW
    assert n_pages % CMP_PAGES == 0 and n_pages % SLC_PAGES == 0 and page % CMP_STRIDE == 0
    assert LANES % (SLC_PAGES * page // SEL_BLOCK) == 0

    offs = np.cumsum((2 * C_CONV, N_HEADS * HEAD_DIM, 3 * KV_W, 3 * N_HEADS))
    wu = w_in[:, :offs[0]].astype(BF16)
    wq = w_in[:, offs[0]:offs[1]].astype(BF16)
    wkv = w_in[:, offs[1]:offs[2]].astype(BF16)
    wgn = jnp.pad(w_in[:, offs[2]:offs[3]].reshape(D_MODEL, N_KV, 3 * GROUP),
                  ((0, 0), (0, 0), (0, LANES - 3 * GROUP))).reshape(D_MODEL, N_KV * LANES).astype(BF16)
    wgm = w_in[:, offs[3]:].astype(BF16)
    bd16, ex16 = _head_sum_mats(D_MODEL)
    bd4, ex4 = _head_sum_mats(K_W)
    in_wts = (g_norm_mix.reshape(1, D_MODEL), wu, wq, wkv, wgn, wgm, _tile_gain(g_q, D_MODEL),
              _tile_gain(g_k_slc, K_W), _tile_gain(g_k_win, K_W), bd16, ex16, bd4, ex4)
    wdw = jnp.pad(w_dw, ((0, 1), (0, 0)))
    conv_wts = (wdw, b_dw.reshape(1, C_CONV), g_ln_conv.reshape(1, C_CONV), b_ln_conv.reshape(1, C_CONV),
                w_conv_out.astype(BF16))
    wno = w_nsa_out.astype(BF16)
    wout = w_out.astype(BF16)
    gffn = g_norm_ffn.reshape(1, D_MODEL)
    wup = w_up.reshape(D_MODEL, 2 * FFN_NC, FFN_CW).transpose(1, 0, 2).astype(BF16)
    wdn = w_down.reshape(FFN_NC, FFN_CW, D_MODEL).astype(BF16)
    wfd = jnp.concatenate([w_ffn_dw, b_ffn_dw[None, :], jnp.zeros((4, 2 * D_FF), F32)], axis=0)
    wfd = wfd.reshape(SUBLANES, 2 * FFN_NC, FFN_CW).transpose(1, 0, 2)
    bdk = _cmp_weights(w_cmp_k)
    bdv = _cmp_weights(w_cmp_v)
    pek, wk4 = _pos_operands(pe_cmp_k, w_cmp_k)
    pev, wv4 = _pos_operands(pe_cmp_v, w_cmp_v)
    gkc = _tile_gain(g_k_cmp, K_W)
    cw_tail = (pek, wk4, pev, wv4, gkc, bd4, ex4)

    cos_p, sin_p = _rope_tables(jnp.arange(seq))
    (glu, q, cmp_rows, slc_rows, win_rows, gn, gm, k2s, v2s, k2w, v2w) = _inproj(
        x_prompt, cos_p, sin_p, in_wts, True)
    zeros_halo = jnp.zeros((bp, CONV_HALO, C_CONV), F32)
    a_out, conv_tail = _conv_module(glu, glu, zeros_halo, *conv_wts, tt=256, rb=64)
    nch = seq // CMP_STRIDE
    n_cmp = nch - 1
    n_slc = seq // SEL_BLOCK
    cos_c, sin_c = _rope_tables(jnp.arange(nch) * CMP_STRIDE + (CMP_LEN - 1))
    kc2, vc2 = _cmp_prompt(cmp_rows.reshape(bp, nch, CMP_STRIDE * KV_W), (bdk, bdv) + cw_tail, cos_c, sin_c)
    e_tab = jnp.asarray((np.arange(seq)[:, None] // SEL_BLOCK == np.arange(LANES)[None, :]), BF16)
    ov = jnp.asarray(_overlap_t(LANES, nch, 0, n_cmp).T * (np.arange(LANES) < n_slc), BF16)
    gex_np = np.zeros((2 * LANES, 3 * GROUP * LANES), np.float32)
    for k in range(3 * GROUP):
        gex_np[k, k * LANES:(k + 1) * LANES] = 1.0
        gex_np[LANES + k, k * LANES:(k + 1) * LANES] = 1.0
    o_p = _attn_prompt(q, gn, kc2, vc2, k2s, v2s, k2w, v2w, e_tab, ov, jnp.asarray(gex_np, BF16),
                       n_cmp, n_slc, min(N_SEL, n_slc))
    n_tok = bp * seq
    x1 = _merge(o_p.reshape(n_tok, D_MODEL), a_out.reshape(n_tok, D_MODEL), gm.reshape(n_tok, 2 * D_MODEL),
                x_prompt.reshape(n_tok, D_MODEL), wno, wout, 512)
    zeros_fix = jnp.zeros((bp, SUBLANES, 2 * D_FF), F32)
    y_p, ffn_tail = _ffn(x1.reshape(bp, seq, D_MODEL), gffn, wup, wfd, wdn, zeros_fix, zeros_fix, 512, 512)

    kv5 = lambda rows, b, t: rows.reshape(b, t, 2, N_KV, HEAD_DIM)
    out_p = (y_p, kv5(cmp_rows, bp, seq), kv5(slc_rows, bp, seq),
             kv5(win_rows[:, seq - min(WINDOW, seq):], bp, min(WINDOW, seq)),
             conv_tail[:, CONV_HALO - (CONV_K - 1):], ffn_tail[:, :FFN_CONV_K - 1])

    n_s = db * dseq
    pos_s = past_len + (jnp.arange(TOK_TILE) % dseq)
    cos_s, sin_s = _rope_tables(pos_s)
    xs_pad = jnp.pad(x_sample.reshape(1, n_s, D_MODEL), ((0, 0), (0, TOK_TILE - n_s), (0, 0))) if n_s < TOK_TILE \
        else x_sample.reshape(1, n_s, D_MODEL)
    assert xs_pad.shape[1] == TOK_TILE
    (glu_s, q_s, cmp_s, slc_s, win_s, gn_s, gm_s) = [a[0, :n_s] for a in _inproj(xs_pad, cos_s, sin_s, in_wts, False)]
    st32 = jnp.pad(state_conv, ((0, 0), (CONV_HALO - (CONV_K - 1), 0), (0, 0)))
    a_s, conv_tail_s = _conv_module(glu_s.reshape(db, dseq, C_CONV), st32, st32, *conv_wts, tt=dseq, rb=dseq)

    pos_minor = lambda c: c.transpose(0, 2, 3, 4, 1).reshape(c.shape[0], KV_W, c.shape[1])
    cache_cmp_t = pos_minor(cache_cmp_kv)
    cache_slc_t = pos_minor(cache_slc_kv)
    kcs, vcs = _cmp_stream(page_table, cache_cmp_t, _cmp_pair_weights(w_cmp_k), _cmp_pair_weights(w_cmp_v))
    nch_s = past_len // CMP_STRIDE
    n_cmp_s = -(-(past_len + dseq) // CMP_STRIDE) - 1
    n_slc_s = -(-(past_len + dseq) // SEL_BLOCK)
    nsp = -(-n_slc_s // LANES) * LANES
    cos_cs, sin_cs = _rope_tables(jnp.arange(nch_s) * CMP_STRIDE + (CMP_STRIDE - 1))
    ovt_s = jnp.asarray(_overlap_t(nsp, nch_s, 1, n_cmp_s), BF16)
    qbd, oc_s, pen_s = _cmp_sample(kcs, vcs, q_s.reshape(db, dseq, D_MODEL), cw_tail, cos_cs, sin_cs, ovt_s,
                                   past_len, n_slc_s, min(N_SEL, n_slc_s))
    m_s, l_s, acc_s = _slc_stream(page_table, cache_slc_t, qbd, pen_s)
    win3 = cache_win_kv.reshape(db, wlen, KV_W)
    o_s = _finish_sample(m_s, l_s, acc_s, oc_s, qbd, pen_s, slc_s.reshape(db, dseq, KV_W),
                         win_s.reshape(db, dseq, KV_W), pos_minor(cache_win_kv),
                         gn_s.reshape(db, dseq, N_KV * LANES), past_len, past_len // SEL_BLOCK)
    x1_s = _merge(o_s.reshape(n_s, D_MODEL), a_s.reshape(n_s, D_MODEL), gm_s, x_sample.reshape(n_s, D_MODEL),
                  wno, wout, n_s)
    z1 = jnp.zeros((db, dseq - 1, 2 * D_FF), F32)
    fix1 = jnp.concatenate([state_ffn_conv[:, 1:2], z1], axis=1).reshape(1, n_s, 2 * D_FF)
    fix2 = jnp.concatenate([state_ffn_conv, z1[:, 1:]], axis=1).reshape(1, n_s, 2 * D_FF)
    y_s, up_s = _ffn(x1_s.reshape(1, n_s, D_MODEL), gffn, wup, wfd, wdn, fix1, fix2, n_s, dseq)

    win_all = jnp.concatenate([win3, win_s.reshape(db, dseq, KV_W)], axis=1)
    keep = min(WINDOW, past_len + dseq)
    out_s = (y_s.reshape(db, dseq, D_MODEL), kv5(cmp_s, db, dseq), kv5(slc_s, db, dseq),
             kv5(win_all[:, wlen + dseq - keep:], db, keep),
             conv_tail_s[:, CONV_HALO - (CONV_K - 1):],
             up_s.reshape(db, dseq, 2 * D_FF)[:, dseq - (FFN_CONV_K - 1):])
    return (out_p[0], out_s[0]) + out_p[1:] + out_s[1:]
```

```python
import functools

import numpy as np
import jax
import jax.numpy as jnp
from jax import lax
from jax.experimental import pallas as pl
from jax.experimental.pallas import tpu as pltpu

F32 = jnp.float32
BF16 = jnp.bfloat16

D_MODEL = 1024
N_HEADS = 16
HEAD_DIM = 64
N_KV = 4
GROUP = N_HEADS // N_KV
CMP_STRIDE = 16
CMP_LEN = 2 * CMP_STRIDE
SEL_BLOCK = 64
N_SEL = 16
WINDOW = 512
C_CONV = D_MODEL // 2
CONV_K = 31
FFN_CONV_K = 3
D_FF = 2816
ROPE_THETA = 10000.0
EPS = 1e-6
KV_W = 2 * N_KV * HEAD_DIM
K_W = N_KV * HEAD_DIM

LANES = 128
SUBLANES = 8
QBLK = WINDOW
TOK_TILE = 256
ATT_GROUPS = 2
CONV_HALO = 32
VMEM_LIMIT = 56 * 1024 * 1024
FFN_CW = 256
FFN_NC = D_FF // FFN_CW
CMP_PAGES = 32
CMP_GROUP = 8
SLC_PAGES = 64
QROWS = GROUP * N_KV * 8

NEG = -1e30
PEN = -1e9
RANK_FORCED = 1e30
RANK_INVALID = -1e30
RANK_REMOVED = -2e30


def _dot(a, b):
    return jnp.dot(a, b, preferred_element_type=F32)


def _dot_nt(a, b):
    return lax.dot_general(a, b, (((1,), (1,)), ((), ())), preferred_element_type=F32)


def _sigmoid(x):
    return 1.0 / (1.0 + jnp.exp(-x))


def _iota(shape, dim):
    return lax.broadcasted_iota(jnp.int32, shape, dim)


def _params(sem):
    return pltpu.CompilerParams(dimension_semantics=sem, vmem_limit_bytes=VMEM_LIMIT)


def _const_spec(a, ngrid):
    nd = a.ndim
    if ngrid == 1:
        return pl.BlockSpec(a.shape, lambda i: (0,) * nd)
    if ngrid == 2:
        return pl.BlockSpec(a.shape, lambda i, j: (0,) * nd)
    return pl.BlockSpec(a.shape, lambda i, j, k: (0,) * nd)


def _rope_tables(pos):
    half = HEAD_DIM // 2
    inv = ROPE_THETA ** (-jnp.arange(half, dtype=F32) * (2.0 / HEAD_DIM))
    ang = pos.astype(F32)[:, None] * inv[None, :]
    cos, sin = jnp.cos(ang), jnp.sin(ang)
    return jnp.tile(cos, (1, 4)), jnp.tile(jnp.concatenate([-sin, sin], axis=1), (1, 2))


def _head_sum_mats(width):
    heads = width // HEAD_DIM
    lane_head = np.arange(width) // HEAD_DIM
    bd = (lane_head[:, None] == np.arange(LANES)[None, :]).astype(np.float32)
    ex = np.zeros((2 * LANES, width), np.float32)
    ex[:heads] = (np.arange(heads)[:, None] == lane_head[None, :])
    ex[LANES:LANES + heads] = ex[:heads]
    return jnp.asarray(bd, BF16), jnp.asarray(ex, BF16)


def _overlap_t(n_slc_rows, n_cols, col_shift, n_cmp):
    n = np.arange(n_cols) - col_shift
    j = np.arange(n_slc_rows)
    cs = n * CMP_STRIDE
    ss = j * SEL_BLOCK
    ov = (cs[None, :] < ss[:, None] + SEL_BLOCK) & (cs[None, :] + CMP_LEN > ss[:, None])
    ov &= (n[None, :] >= 0) & (n[None, :] < n_cmp)
    return ov.astype(np.float32)


def _head_rmsnorm(x, gain, bd, ex):
    ssum = _dot((x * x).astype(BF16), bd)
    r = lax.rsqrt(ssum * (1.0 / HEAD_DIM) + EPS)
    r_hi = r.astype(BF16)
    r_lo = (r - r_hi.astype(F32)).astype(BF16)
    rfull = _dot(jnp.concatenate([r_hi, r_lo], axis=1), ex)
    return x * rfull * gain


def _rope128(y, cos, sin):
    lane = _iota(y.shape, 1)
    first = (lane & (HEAD_DIM - 1)) < (HEAD_DIM // 2)
    rot = jnp.where(first, pltpu.roll(y, LANES - HEAD_DIM // 2, 1), pltpu.roll(y, HEAD_DIM // 2, 1))
    return y * cos + rot * sin


def _dup_heads(chunk):
    lo = _iota(chunk.shape, 1) < HEAD_DIM
    swapped = pltpu.roll(chunk, HEAD_DIM, 1)
    return jnp.where(lo, chunk, swapped), jnp.where(lo, swapped, chunk)


def _rank_select(val, n_rounds):
    rows = val.shape[0]
    j = _iota(val.shape, 0).astype(F32)
    sel = jnp.zeros(val.shape, F32)
    for _ in range(n_rounds):
        mx = jnp.max(val, axis=0, keepdims=True)
        idx = jnp.min(jnp.where(val == mx, j, float(rows)), axis=0, keepdims=True)
        hit = j == idx
        sel = jnp.where(hit, 1.0, sel)
        val = jnp.where(hit, RANK_REMOVED, val)
    return sel > 0.5


def _inproj_kernel(x_ref, gmix_ref, wu_ref, wq_ref, wkv_ref, wgn_ref, wgm_ref, gq_ref, gks_ref,
                   gkw_ref, bd16_ref, ex16_ref, bd4_ref, ex4_ref, cos_ref, sin_ref,
                   glu_ref, q_ref, cmp_ref, slc_ref, win_ref, gn_ref, gm_ref, *attn_refs):
    x = x_ref[...]
    ms = jnp.mean(x * x, axis=-1, keepdims=True)
    h = (x * lax.rsqrt(ms + EPS) * gmix_ref[...]).astype(BF16)
    cos = cos_ref[...]
    sin = sin_ref[...]

    u = _dot(h, wu_ref[...])
    glu_ref[...] = u[:, :C_CONV] * _sigmoid(u[:, C_CONV:])

    yq = _head_rmsnorm(_dot(h, wq_ref[...]), gq_ref[...], bd16_ref[...], ex16_ref[...])
    scale = HEAD_DIM ** -0.5
    for c in range(D_MODEL // LANES):
        sl = slice(c * LANES, (c + 1) * LANES)
        q_ref[:, sl] = (_rope128(yq[:, sl], cos, sin) * scale).astype(BF16)

    zkv = _dot(h, wkv_ref[...])
    cmp_ref[...] = zkv[:, :KV_W]

    def kv_branch(z, gain_ref, rows_ref, k2_ref, v2_ref):
        yk = _head_rmsnorm(z[:, :K_W], gain_ref[...], bd4_ref[...], ex4_ref[...])
        v = z[:, K_W:]
        rows_ref[:, K_W:] = v
        for c in range(K_W // LANES):
            sl = slice(c * LANES, (c + 1) * LANES)
            kr = _rope128(yk[:, sl], cos, sin)
            rows_ref[:, sl] = kr
            if k2_ref is not None:
                for ref, chunk in ((k2_ref, kr), (v2_ref, v[:, sl])):
                    a2, b2 = _dup_heads(chunk)
                    ref[2 * c] = a2.astype(BF16)
                    ref[2 * c + 1] = b2.astype(BF16)

    if attn_refs:
        k2s_ref, v2s_ref, k2w_ref, v2w_ref, cslab_ref = attn_refs
        for c in range(KV_W // LANES):
            cslab_ref[c] = zkv[:, c * LANES:(c + 1) * LANES]
    else:
        k2s_ref = v2s_ref = k2w_ref = v2w_ref = None
    kv_branch(zkv[:, KV_W:2 * KV_W], gks_ref, slc_ref, k2s_ref, v2s_ref)
    kv_branch(zkv[:, 2 * KV_W:], gkw_ref, win_ref, k2w_ref, v2w_ref)

    gn_ref[...] = _sigmoid(_dot(h, wgn_ref[...]))
    gm_ref[...] = _sigmoid(_dot(h, wgm_ref[...]))


def _inproj(x, cos, sin, wts, attn_layouts):
    B, T, _ = x.shape
    tm = TOK_TILE
    assert T % tm == 0
    nt = T // tm
    tok = lambda w: pl.BlockSpec((None, tm, w), lambda b, t: (b, t, 0))
    tab = pl.BlockSpec((tm, LANES), lambda b, t: (t, 0))
    in_specs = [tok(D_MODEL)] + [_const_spec(a, 2) for a in wts] + [tab, tab]
    out_shape = [
        jax.ShapeDtypeStruct((B, T, C_CONV), F32),
        jax.ShapeDtypeStruct((B, T, D_MODEL), BF16),
        jax.ShapeDtypeStruct((B, T, KV_W), F32),
        jax.ShapeDtypeStruct((B, T, KV_W), F32),
        jax.ShapeDtypeStruct((B, T, KV_W), F32),
        jax.ShapeDtypeStruct((B, T, N_KV * LANES), F32),
        jax.ShapeDtypeStruct((B, T, 2 * D_MODEL), F32),
    ]
    out_specs = [tok(C_CONV), tok(D_MODEL), tok(KV_W), tok(KV_W), tok(KV_W), tok(N_KV * LANES),
                 tok(2 * D_MODEL)]
    if attn_layouts:
        dup = jax.ShapeDtypeStruct((B, N_KV, T, LANES), BF16)
        dup_spec = pl.BlockSpec((None, N_KV, tm, LANES), lambda b, t: (b, 0, t, 0))
        out_shape += [dup] * 4 + [jax.ShapeDtypeStruct((B, KV_W // LANES, T, LANES), F32)]
        out_specs += [dup_spec] * 5
    return pl.pallas_call(
        _inproj_kernel,
        grid=(B, nt),
        in_specs=in_specs,
        out_specs=out_specs,
        out_shape=out_shape,
        compiler_params=_params(("parallel", "parallel")),
        name="inproj",
    )(x, *wts, cos, sin)


def _conv_kernel(glu_ref, halo_ref, st_ref, wdw_ref, bdw_ref, gln_ref, bln_ref, wco_ref,
                 a_ref, cst_ref, xc_ref, s_ref, *, tt, rb):
    ti = pl.program_id(1)

    @pl.when(ti == 0)
    def _():
        xc_ref[0:CONV_HALO, :] = st_ref[...]

    @pl.when(ti > 0)
    def _():
        xc_ref[0:CONV_HALO, :] = halo_ref[...]

    xc_ref[CONV_HALO:CONV_HALO + tt, :] = glu_ref[...]
    xc_ref[CONV_HALO + tt:CONV_HALO + tt + SUBLANES, :] = jnp.zeros((SUBLANES, C_CONV), F32)
    first = CONV_HALO - (CONV_K - 1)
    for r0 in range(0, tt, rb):
        acc = jnp.broadcast_to(bdw_ref[...], (rb, C_CONV))
        for phi in range(SUBLANES):
            y = None
            for u in range(first, first + CONV_K):
                if u % SUBLANES == phi:
                    term = wdw_ref[u - first:u - first + 1, :] * xc_ref[r0 + u - phi:r0 + u - phi + rb + SUBLANES, :]
                    y = term if y is None else y + term
            acc = acc + y[phi:phi + rb]
        mu = jnp.mean(acc, axis=-1, keepdims=True)
        d = acc - mu
        var = jnp.mean(d * d, axis=-1, keepdims=True)
        y = d * lax.rsqrt(var + EPS) * gln_ref[...] + bln_ref[...]
        s_ref[r0:r0 + rb, :] = y * _sigmoid(y)
    a_ref[...] = _dot(s_ref[...].astype(BF16), wco_ref[...])
    cst_ref[...] = xc_ref[tt:tt + CONV_HALO, :]


def _conv_module(glu, halo_src, state32, wdw, bdw, gln, bln, wco, tt, rb):
    B, T, _ = glu.shape
    nt = T // tt
    hb = max(tt // CONV_HALO, 1)
    return pl.pallas_call(
        functools.partial(_conv_kernel, tt=tt, rb=rb),
        grid=(B, nt),
        in_specs=[
            pl.BlockSpec((None, tt, C_CONV), lambda b, t: (b, t, 0)),
            pl.BlockSpec((None, CONV_HALO, C_CONV), lambda b, t: (b, jnp.maximum(t * hb - 1, 0), 0)),
            pl.BlockSpec((None, CONV_HALO, C_CONV), lambda b, t: (b, 0, 0)),
            _const_spec(wdw, 2), _const_spec(bdw, 2), _const_spec(gln, 2), _const_spec(bln, 2),
            _const_spec(wco, 2),
        ],
        out_specs=[
            pl.BlockSpec((None, tt, D_MODEL), lambda b, t: (b, t, 0)),
            pl.BlockSpec((None, CONV_HALO, C_CONV), lambda b, t: (b, 0, 0)),
        ],
        out_shape=[
            jax.ShapeDtypeStruct((B, T, D_MODEL), F32),
            jax.ShapeDtypeStruct((B, CONV_HALO, C_CONV), F32),
        ],
        scratch_shapes=[pltpu.VMEM((CONV_HALO + tt + SUBLANES, C_CONV), F32), pltpu.VMEM((tt, C_CONV), F32)],
        compiler_params=_params(("parallel", "arbitrary")),
        name="conv_module",
    )(glu, halo_src, state32, wdw, bdw, gln, bln, wco)


def _merge_kernel(o_ref, a_ref, gm_ref, x_ref, wno_ref, wout_ref, x1_ref):
    b_out = _dot(o_ref[...], wno_ref[...])
    gm = gm_ref[...]
    m = gm[:, :D_MODEL] * a_ref[...] + gm[:, D_MODEL:] * b_out
    x1_ref[...] = x_ref[...] + _dot(m.astype(BF16), wout_ref[...])


def _merge(o, a_out, gm, x, wno, wout, tm):
    N = x.shape[0]
    row = lambda w: pl.BlockSpec((tm, w), lambda i: (i, 0))
    return pl.pallas_call(
        _merge_kernel,
        grid=(N // tm,),
        in_specs=[row(D_MODEL), row(D_MODEL), row(2 * D_MODEL), row(D_MODEL),
                  _const_spec(wno, 1), _const_spec(wout, 1)],
        out_specs=row(D_MODEL),
        out_shape=jax.ShapeDtypeStruct((N, D_MODEL), F32),
        compiler_params=_params(("parallel",)),
        name="merge",
    )(o, a_out, gm, x, wno, wout)


def _ffn_kernel(x1_ref, gn_ref, wup_ref, wdw_ref, wdn_ref, fix1_ref, fix2_ref,
                y_ref, st_ref, prev_ref, acc_ref, *, tm, seg):
    streaming = seg == tm
    x1 = x1_ref[...]
    ms = jnp.mean(x1 * x1, axis=-1, keepdims=True)
    h = (x1 * lax.rsqrt(ms + EPS) * gn_ref[...]).astype(BF16)
    row = _iota((tm, FFN_CW), 0) & (seg - 1)

    if streaming:
        @pl.when(pl.program_id(1) == 0)
        def _():
            prev_ref[...] = fix2_ref[...]

    def conv3(c):
        cs = slice(c * FFN_CW, (c + 1) * FFN_CW)
        up = _dot(h, wup_ref[c])
        r1 = pltpu.roll(up, 1, 0)
        r2 = pltpu.roll(up, 2, 0)
        if streaming:
            p0 = prev_ref[0:1, cs]
            p1 = prev_ref[1:2, cs]
            s1 = jnp.where(row == 0, p1, r1)
            s2 = jnp.where(row == 0, p0, jnp.where(row == 1, p1, r2))
            prev_ref[0:2, cs] = up[tm - 2:tm, :]
        else:
            s1 = jnp.where(row == 0, fix1_ref[:, cs], r1)
            s2 = jnp.where(row < 2, fix2_ref[:, cs], r2)
            st_ref[:, cs] = up
        w = wdw_ref[c]
        return w[0:1] * s2 + w[1:2] * s1 + w[2:3] * up + w[3:4]

    acc_ref[...] = x1
    for c in range(FFN_NC):
        gate = conv3(c)
        val = conv3(FFN_NC + c)
        act = gate * _sigmoid(gate) * val
        acc_ref[...] += _dot(act.astype(BF16), wdn_ref[c])
    y_ref[...] = acc_ref[...]
    if streaming:
        st_ref[...] = prev_ref[...]


def _ffn(x1, gn, wup, wdw, wdn, fix1, fix2, tm, seg):
    B, T, _ = x1.shape
    nt = T // tm
    streaming = seg == tm
    tok = lambda w: pl.BlockSpec((None, tm, w), lambda b, t: (b, t, 0))
    if streaming:
        fix_specs = [pl.BlockSpec((None, SUBLANES, 2 * D_FF), lambda b, t: (b, 0, 0))] * 2
        st_shape = jax.ShapeDtypeStruct((B, SUBLANES, 2 * D_FF), F32)
        st_spec = pl.BlockSpec((None, SUBLANES, 2 * D_FF), lambda b, t: (b, 0, 0))
    else:
        fix_specs = [tok(2 * D_FF)] * 2
        st_shape = jax.ShapeDtypeStruct((B, T, 2 * D_FF), F32)
        st_spec = tok(2 * D_FF)
    return pl.pallas_call(
        functools.partial(_ffn_kernel, tm=tm, seg=seg),
        grid=(B, nt),
        in_specs=[tok(D_MODEL), _const_spec(gn, 2), _const_spec(wup, 2), _const_spec(wdw, 2),
                  _const_spec(wdn, 2)] + fix_specs,
        out_specs=[tok(D_MODEL), st_spec],
        out_shape=[jax.ShapeDtypeStruct((B, T, D_MODEL), F32), st_shape],
        scratch_shapes=[pltpu.VMEM((SUBLANES, 2 * D_FF), F32), pltpu.VMEM((tm, D_MODEL), F32)],
        compiler_params=_params(("parallel", "arbitrary")),
        name="conv_ffn",
    )(x1, gn, wup, wdw, wdn, fix1, fix2)


def _cmp_prompt_kernel(rows_ref, bdk_ref, bdv_ref, pek_ref, wk4_ref, pev_ref, wv4_ref, gk_ref,
                       bd4_ref, ex4_ref, cos_ref, sin_ref, kc2_ref, vc2_ref, *, nch):
    acc_k = jnp.zeros((nch, 2 * K_W), F32)
    acc_v = jnp.zeros((nch, 2 * K_W), F32)
    for l in range(CMP_STRIDE):
        x = [rows_ref[c, pl.ds(l, nch, stride=CMP_STRIDE), :] for c in range(KV_W // LANES)]
        acc_k += _dot(jnp.concatenate(x[:2], axis=1).astype(BF16), bdk_ref[l])
        acc_v += _dot(jnp.concatenate(x[2:], axis=1).astype(BF16), bdv_ref[l])
    pos_k = _dot(pek_ref[...], wk4_ref[...])[0:1]
    pos_v = _dot(pev_ref[...], wv4_ref[...])[0:1]
    kc = acc_k[:, :K_W] + pltpu.roll(acc_k[:, K_W:], nch - 1, 0) + pos_k
    vc = acc_v[:, :K_W] + pltpu.roll(acc_v[:, K_W:], nch - 1, 0) + pos_v
    yk = _head_rmsnorm(kc, gk_ref[...], bd4_ref[...], ex4_ref[...])
    for c in range(K_W // LANES):
        sl = slice(c * LANES, (c + 1) * LANES)
        for ref, chunk in ((kc2_ref, _rope128(yk[:, sl], cos_ref[...], sin_ref[...])), (vc2_ref, vc[:, sl])):
            a2, b2 = _dup_heads(chunk)
            ref[2 * c] = a2.astype(BF16)
            ref[2 * c + 1] = b2.astype(BF16)


def _cmp_prompt(rows, cw, cos, sin):
    B, nslab, T, _ = rows.shape
    nch = T // CMP_STRIDE
    return pl.pallas_call(
        functools.partial(_cmp_prompt_kernel, nch=nch),
        grid=(B,),
        in_specs=[pl.BlockSpec((None, nslab, T, LANES), lambda b: (b, 0, 0, 0))] + [_const_spec(a, 1) for a in cw]
        + [_const_spec(cos, 1), _const_spec(sin, 1)],
        out_specs=[pl.BlockSpec((None, N_KV, nch, LANES), lambda b: (b, 0, 0, 0))] * 2,
        out_shape=[jax.ShapeDtypeStruct((B, N_KV, nch, LANES), BF16)] * 2,
        compiler_params=_params(("parallel",)),
        name="cmp_prompt",
    )(rows, *cw, cos, sin)


def _attn_prompt_kernel(q_ref, gates_ref, kc2_ref, vc2_ref, k2s_ref, v2s_ref, k2w_ref, v2w_ref,
                        e_ref, ov_ref, gex_ref, o_ref, m_ref, l_ref, acc_ref, mw_ref, lw_ref, accw_ref,
                        *, n_cmp, n_slc, n_top):
    qi = pl.program_id(2)
    qs = qi * QBLK
    ng, nch = kc2_ref.shape[0], kc2_ref.shape[1]
    lo = _iota((QBLK, LANES), 1) < HEAD_DIM
    zero = jnp.zeros((QBLK, LANES), BF16)
    last = jnp.minimum((qs + _iota((QBLK, nch), 0) - (CMP_LEN - 1)) >> 4, n_cmp - 1)
    cmask = _iota((QBLK, nch), 1) <= last

    def masked(x, fill):
        return jnp.concatenate([jnp.where(cmask, x[r * QBLK:(r + 1) * QBLK], fill) for r in range(GROUP)], axis=0)

    kl = _iota((QBLK, QBLK), 1)
    tl = _iota((QBLK, QBLK), 0)
    tri = {"causal": kl <= tl, "band": kl >= tl}
    slc_state = lambda gi: (m_ref.at[gi], l_ref.at[gi], acc_ref.at[gi])
    win_state = lambda gi: (mw_ref.at[gi], lw_ref.at[gi], accw_ref.at[gi])

    def reset(state):
        m_st, l_st, acc_st = state
        m_st[...] = jnp.full(m_st.shape, NEG, F32)
        l_st[...] = jnp.zeros(l_st.shape, F32)
        acc_st[...] = jnp.zeros(acc_st.shape, F32)

    def attend(state, kblk, v2blk, q_rows, mode):
        m_st, l_st, acc_st = state
        sc = _dot_nt(q_rows, kblk)
        if mode != "none":
            sc = jnp.concatenate([jnp.where(tri[mode], sc[r * QBLK:(r + 1) * QBLK], NEG)
                                  for r in range(GROUP)], axis=0)
        m_old = m_st[...]
        m_new = jnp.maximum(m_old, jnp.max(sc, axis=1, keepdims=True))
        alpha = jnp.exp(m_old - m_new)
        p = jnp.exp(sc - jnp.concatenate([m_new] * (sc.shape[1] // LANES), axis=1))
        l_st[...] = alpha * l_st[...] + jnp.sum(p, axis=1, keepdims=True)
        acc_st[...] = alpha * acc_st[...] + _dot(p.astype(BF16), v2blk)
        m_st[...] = m_new

    def blk(ref, kb):
        return ref[pl.ds(pl.multiple_of(kb * QBLK, QBLK), QBLK), :]

    def front(gi):
        qm = []
        for p in range(GROUP // 2):
            qp = q_ref[:, gi * K_W + p * LANES:gi * K_W + (p + 1) * LANES]
            qm += [jnp.where(lo, qp, zero), jnp.where(lo, zero, qp)]
        q_all = jnp.concatenate(qm, axis=0)

        s = masked(_dot_nt(q_all, kc2_ref[gi]), NEG)
        mx = jnp.max(s, axis=1, keepdims=True)
        e = masked(jnp.exp(s - mx), 0.0)
        p_cmp = (e * (1.0 / jnp.maximum(jnp.sum(e, axis=1, keepdims=True), 1e-30))).astype(BF16)
        o_cmp = _dot(p_cmp, vc2_ref[gi])
        imp4 = _dot(p_cmp, ov_ref[...])
        imp = imp4[0:QBLK]
        for r in range(1, GROUP):
            imp = imp + imp4[r * QBLK:(r + 1) * QBLK]

        reset(win_state(gi))
        attend(win_state(gi), blk(k2w_ref.at[gi], qi), blk(v2w_ref.at[gi], qi), q_all, "causal")

        imp_t = imp.T[0:n_slc]
        j = _iota((n_slc, QBLK), 0)
        tq = qs + _iota((n_slc, QBLK), 1)
        cur = tq >> 6
        forced = jnp.logical_or(j == 0, jnp.logical_or(j == cur, j == cur - 1))
        rank = jnp.where(forced, RANK_FORCED, jnp.where(j * SEL_BLOCK <= tq, imp_t, RANK_INVALID))
        pen_t = jnp.where(_rank_select(rank, n_top), 0.0, PEN)
        if n_slc < LANES:
            pen_t = jnp.concatenate([pen_t, jnp.zeros((LANES - n_slc, QBLK), F32)], axis=0)
        pen = pen_t.T.astype(BF16)
        qa_all = jnp.concatenate([q_all, jnp.concatenate([pen] * GROUP, axis=0)], axis=1)
        reset(slc_state(gi))
        return q_all, qa_all, o_cmp

    parts = [front(gi) for gi in range(ng)]

    def slc_step(kb, mode):
        e_blk = blk(e_ref, kb)
        for gi in range(ng):
            keys = jnp.concatenate([blk(k2s_ref.at[gi], kb), e_blk], axis=1)
            attend(slc_state(gi), keys, blk(v2s_ref.at[gi], kb), parts[gi][1], mode)

    def slc_body(kb, carry):
        slc_step(kb, "none")
        return carry

    lax.fori_loop(0, qi, slc_body, 0)
    slc_step(qi, "causal")

    @pl.when(qi >= 1)
    def _():
        for gi in range(ng):
            attend(win_state(gi), blk(k2w_ref.at[gi], qi - 1), blk(v2w_ref.at[gi], qi - 1), parts[gi][0], "band")

    for gi in range(ng):
        o_cmp = parts[gi][2]
        o_slc = acc_ref[gi] / l_ref[gi]
        o_win = accw_ref[gi] / lw_ref[gi]
        gt = gates_ref[:, gi * LANES:(gi + 1) * LANES]
        g_hi = gt.astype(BF16)
        g_lo = (gt - g_hi.astype(F32)).astype(BF16)
        gex = _dot(jnp.concatenate([g_hi, g_lo], axis=1), gex_ref[...])
        mixed = []
        for r in range(GROUP):
            rs = slice(r * QBLK, (r + 1) * QBLK)
            gl = lambda k: gex[:, (3 * r + k) * LANES:(3 * r + k + 1) * LANES]
            mixed.append(gl(0) * o_cmp[rs] + gl(1) * o_slc[rs] + gl(2) * o_win[rs])
        for p in range(GROUP // 2):
            o_ref[:, gi * K_W + p * LANES:gi * K_W + (p + 1) * LANES] = jnp.where(
                lo, mixed[2 * p], mixed[2 * p + 1]).astype(BF16)


def _attn_prompt(q, gates, kc2, vc2, k2s, v2s, k2w, v2w, e_tab, ov, gex, n_cmp, n_slc, n_top):
    B, T, _ = q.shape
    nq = T // QBLK
    nch = kc2.shape[2]
    rows = GROUP * QBLK
    ng = ATT_GROUPS
    per_bg = lambda n: pl.BlockSpec((None, ng, n, LANES), lambda b, g, i: (b, g, 0, 0))
    return pl.pallas_call(
        functools.partial(_attn_prompt_kernel, n_cmp=n_cmp, n_slc=n_slc, n_top=n_top),
        grid=(B, N_KV // ng, nq),
        in_specs=[
            pl.BlockSpec((None, QBLK, ng * K_W), lambda b, g, i: (b, i, g)),
            pl.BlockSpec((None, QBLK, ng * LANES), lambda b, g, i: (b, i, g)),
            per_bg(nch), per_bg(nch), per_bg(T), per_bg(T), per_bg(T), per_bg(T),
            _const_spec(e_tab, 3), _const_spec(ov, 3), _const_spec(gex, 3),
        ],
        out_specs=pl.BlockSpec((None, QBLK, ng * K_W), lambda b, g, i: (b, i, g)),
        out_shape=jax.ShapeDtypeStruct((B, T, D_MODEL), BF16),
        scratch_shapes=[pltpu.VMEM((ng, rows, LANES), F32)] * 6,
        compiler_params=_params(("parallel", "parallel", "arbitrary")),
        name="attn_prompt",
    )(q, gates, kc2, vc2, k2s, v2s, k2w, v2w, e_tab, ov, gex)


def _cmp_stream_kernel(pt_ref, *refs, m):
    pages = refs[:CMP_PAGES]
    bdk_ref, bdv_ref, kc_ref, vc_ref, fs_ref, slab_ref = refs[CMP_PAGES:]
    s = pl.program_id(1)
    per_page = m // CMP_PAGES
    nslab = KV_W // LANES

    @pl.when(s == 0)
    def _():
        fs_ref[0:SUBLANES, :] = jnp.zeros((SUBLANES, 2 * K_W), F32)

    half = K_W // 2

    def page_group(k0, k1):
        mg = (k1 - k0) * per_page
        for k in range(k0, k1):
            for c in range(nslab):
                slab_ref[k, c] = pages[k][c * LANES:(c + 1) * LANES, :].T

        def rows_of(c, l):
            return jnp.concatenate([slab_ref[k, c, pl.ds(l, per_page, stride=CMP_STRIDE), :]
                                    for k in range(k0, k1)], axis=0)

        def pair_rows(c, l):
            return jnp.concatenate([rows_of(c, l), rows_of(c, l + CMP_STRIDE // 2)], axis=1)

        acc_k = jnp.zeros((2 * mg, K_W), F32)
        acc_v = jnp.zeros((2 * mg, K_W), F32)
        for l in range(CMP_STRIDE // 2):
            xk = jnp.concatenate([pair_rows(0, l), pair_rows(1, l)], axis=0)
            xv = jnp.concatenate([pair_rows(2, l), pair_rows(3, l)], axis=0)
            acc_k += _dot(xk.astype(BF16), bdk_ref[l])
            acc_v += _dot(xv.astype(BF16), bdv_ref[l])
        return (jnp.concatenate([acc_k[:mg, :half], acc_k[mg:, :half], acc_v[:mg, :half], acc_v[mg:, :half]], axis=1),
                jnp.concatenate([acc_k[:mg, half:], acc_k[mg:, half:]], axis=1),
                jnp.concatenate([acc_v[:mg, half:], acc_v[mg:, half:]], axis=1))

    groups = [page_group(k0, k0 + CMP_GROUP) for k0 in range(0, CMP_PAGES, CMP_GROUP)]
    first, second_k, second_v = [jnp.concatenate([g[i] for g in groups], axis=0) for i in range(3)]
    fs_ref[SUBLANES:SUBLANES + m, :] = first
    shifted = fs_ref[SUBLANES - 1:SUBLANES - 1 + m, :]
    kc_ref[...] = shifted[:, :K_W] + second_k
    vc_ref[...] = shifted[:, K_W:] + second_v
    fs_ref[SUBLANES - 1:SUBLANES, :] = fs_ref[SUBLANES - 1 + m:SUBLANES + m, :]


def _cmp_stream(page_table, cache_t, bdk, bdv):
    db, n_pages = page_table.shape
    page = cache_t.shape[2]
    per_page = page // CMP_STRIDE
    m = CMP_PAGES * per_page
    nch = n_pages * per_page
    nsteps = n_pages // CMP_PAGES

    def page_spec(k):
        return pl.BlockSpec((None, KV_W, page), lambda b, s, pt: (pt[b, s * CMP_PAGES + k], 0, 0))

    grid_spec = pltpu.PrefetchScalarGridSpec(
        num_scalar_prefetch=1,
        grid=(db, nsteps),
        in_specs=[page_spec(k) for k in range(CMP_PAGES)]
        + [pl.BlockSpec(bdk.shape, lambda b, s, pt: (0, 0, 0)), pl.BlockSpec(bdv.shape, lambda b, s, pt: (0, 0, 0))],
        out_specs=[pl.BlockSpec((None, m, K_W), lambda b, s, pt: (b, s, 0))] * 2,
        scratch_shapes=[pltpu.VMEM((SUBLANES + m, 2 * K_W), F32),
                        pltpu.VMEM((CMP_PAGES, KV_W // LANES, page, LANES), F32)],
    )
    return pl.pallas_call(
        functools.partial(_cmp_stream_kernel, m=m),
        grid_spec=grid_spec,
        out_shape=[jax.ShapeDtypeStruct((db, nch, K_W), F32)] * 2,
        compiler_params=_params(("parallel", "arbitrary")),
        name="cmp_stream",
    )(page_table, *([cache_t] * CMP_PAGES), bdk, bdv)


def _block_diag_q(q):
    lo = _iota((8, LANES), 1) < HEAD_DIM
    zero = jnp.zeros((8, LANES), BF16)
    blocks = []
    for r in range(GROUP):
        for g in range(N_KV):
            hh = g * GROUP + r
            pair = q[:, (hh // 2) * LANES:(hh // 2 + 1) * LANES]
            if hh % 2 != g % 2:
                pair = pltpu.roll(pair.astype(F32), HEAD_DIM, 1).astype(BF16)
            keep = jnp.where(lo, pair, zero) if g % 2 == 0 else jnp.where(lo, zero, pair)
            blocks.append(jnp.concatenate([keep, zero] if g < 2 else [zero, keep], axis=1))
    return jnp.concatenate(blocks, axis=0)


def _cmp_sample_kernel(kcs_ref, vcs_ref, q_ref, pek_ref, wk4_ref, pev_ref, wv4_ref, gk_ref, bd4_ref,
                       ex4_ref, cos_ref, sin_ref, ovt_ref, qbd_ref, oc_ref, pen_ref,
                       *, past_len, n_slc, n_top):
    nch = kcs_ref.shape[0]
    pos_k = _dot(pek_ref[...], wk4_ref[...])[0:1]
    pos_v = _dot(pev_ref[...], wv4_ref[...])[0:1]
    yk = _head_rmsnorm(kcs_ref[...] + pos_k, gk_ref[...], bd4_ref[...], ex4_ref[...])
    kcn = jnp.concatenate([_rope128(yk[:, c * LANES:(c + 1) * LANES], cos_ref[...], sin_ref[...])
                           for c in range(K_W // LANES)], axis=1).astype(BF16)
    vc = (vcs_ref[...] + pos_v).astype(BF16)
    qbd = _block_diag_q(q_ref[...])
    qbd_ref[...] = qbd

    s = _dot_nt(qbd, kcn)
    mi = _iota((QROWS, nch), 1)
    t_row = past_len + (_iota((QROWS, nch), 0) & 7)
    mask = jnp.logical_and(mi * CMP_STRIDE + (CMP_STRIDE - 1) <= t_row, mi >= 1)
    s = jnp.where(mask, s, NEG)
    mx = jnp.max(s, axis=1, keepdims=True)
    e = jnp.where(mask, jnp.exp(s - mx), 0.0)
    p = (e / jnp.maximum(jnp.sum(e, axis=1, keepdims=True), 1e-30)).astype(BF16)
    oc_ref[...] = _dot(p, vc)

    imp = _dot_nt(ovt_ref[...], p)
    imp = imp + pltpu.roll(imp, 32, 1) + pltpu.roll(imp, 64, 1) + pltpu.roll(imp, 96, 1)
    nsp = imp.shape[0]
    j = _iota((nsp, QROWS), 0)
    t_lane = past_len + (_iota((nsp, QROWS), 1) & 7)
    cur = t_lane >> 6
    forced = jnp.logical_or(j == 0, jnp.logical_or(j == cur, j == cur - 1))
    rank = jnp.where(forced, RANK_FORCED, jnp.where(j * SEL_BLOCK <= t_lane, imp, RANK_INVALID))
    rank = jnp.where(j < n_slc, rank, RANK_REMOVED)
    sel = _rank_select(rank, n_top)
    pen_t = jnp.where(sel, 0.0, PEN)
    for blk in range(nsp // LANES):
        pen_ref[blk] = pen_t[blk * LANES:(blk + 1) * LANES, :].T.astype(BF16)


def _cmp_sample(kcs, vcs, q, cw_tail, cos, sin, ovt, past_len, n_slc, n_top):
    db, nch, _ = kcs.shape
    nsp = ovt.shape[0]
    per_b = lambda shape: pl.BlockSpec((None,) + shape, lambda b: (b,) + (0,) * len(shape))
    return pl.pallas_call(
        functools.partial(_cmp_sample_kernel, past_len=past_len, n_slc=n_slc, n_top=n_top),
        grid=(db,),
        in_specs=[per_b((nch, K_W)), per_b((nch, K_W)), per_b((8, D_MODEL))]
        + [_const_spec(a, 1) for a in cw_tail] + [_const_spec(cos, 1), _const_spec(sin, 1), _const_spec(ovt, 1)],
        out_specs=[per_b((QROWS, K_W)), per_b((QROWS, K_W)), per_b((nsp // LANES, QROWS, LANES))],
        out_shape=[jax.ShapeDtypeStruct((db, QROWS, K_W), BF16),
                   jax.ShapeDtypeStruct((db, QROWS, K_W), F32),
                   jax.ShapeDtypeStruct((db, nsp // LANES, QROWS, LANES), BF16)],
        compiler_params=_params(("parallel",)),
        name="cmp_sample",
    )(kcs, vcs, q, *cw_tail, cos, sin, ovt)


def _slc_stream_kernel(pt_ref, *refs):
    pages = refs[:SLC_PAGES]
    qbd_ref, pen_ref, m_out, l_out, acc_out, m_ref, l_ref, acc_ref = refs[SLC_PAGES:]
    s = pl.program_id(1)
    page = pages[0].shape[1]
    nk = SLC_PAGES * page
    blocks_per_step = nk // SEL_BLOCK

    @pl.when(s == 0)
    def _():
        m_ref[...] = jnp.full(m_ref.shape, NEG, F32)
        l_ref[...] = jnp.zeros(l_ref.shape, F32)
        acc_ref[...] = jnp.zeros(acc_ref.shape, F32)

    k_t = jnp.concatenate([pg[0:K_W, :] for pg in pages], axis=1).astype(BF16)
    v_t = jnp.concatenate([pg[K_W:, :] for pg in pages], axis=1).astype(BF16)
    steps_per_lane_block = LANES // blocks_per_step
    pen = pen_ref[s // steps_per_lane_block]
    jrow = _iota((LANES, nk), 0)
    blk = (s % steps_per_lane_block) * blocks_per_step + (_iota((LANES, nk), 1) >> 6)
    expand = jnp.where(jrow == blk, 1.0, 0.0).astype(BF16)
    sc = _dot(qbd_ref[...], k_t) + _dot(pen, expand)
    m_old = m_ref[...]
    m_new = jnp.maximum(m_old, jnp.max(sc, axis=1, keepdims=True))
    alpha = jnp.exp(m_old - m_new)
    p = jnp.exp(sc - m_new)
    l_ref[...] = alpha * l_ref[...] + jnp.sum(p, axis=1, keepdims=True)
    acc_ref[...] = alpha * acc_ref[...] + _dot_nt(p.astype(BF16), v_t)
    m_ref[...] = m_new

    @pl.when(s == pl.num_programs(1) - 1)
    def _():
        m_out[...] = jnp.broadcast_to(m_ref[...], m_out.shape)
        l_out[...] = jnp.broadcast_to(l_ref[...], l_out.shape)
        acc_out[...] = acc_ref[...]


def _slc_stream(page_table, cache_t, qbd, pen):
    db, n_pages = page_table.shape
    page = cache_t.shape[2]
    nsteps = n_pages // SLC_PAGES

    def page_spec(k):
        return pl.BlockSpec((None, KV_W, page), lambda b, s, pt: (pt[b, s * SLC_PAGES + k], 0, 0))

    per_b = lambda shape: pl.BlockSpec((None,) + shape, lambda b, s, pt: (b,) + (0,) * len(shape))
    grid_spec = pltpu.PrefetchScalarGridSpec(
        num_scalar_prefetch=1,
        grid=(db, nsteps),
        in_specs=[page_spec(k) for k in range(SLC_PAGES)] + [per_b((QROWS, K_W)), per_b(pen.shape[1:])],
        out_specs=[per_b((QROWS, LANES)), per_b((QROWS, LANES)), per_b((QROWS, K_W))],
        scratch_shapes=[pltpu.VMEM((QROWS, 1), F32), pltpu.VMEM((QROWS, 1), F32), pltpu.VMEM((QROWS, K_W), F32)],
    )
    return pl.pallas_call(
        _slc_stream_kernel,
        grid_spec=grid_spec,
        out_shape=[jax.ShapeDtypeStruct((db, QROWS, LANES), F32), jax.ShapeDtypeStruct((db, QROWS, LANES), F32),
                   jax.ShapeDtypeStruct((db, QROWS, K_W), F32)],
        compiler_params=_params(("parallel", "arbitrary")),
        name="slc_stream",
    )(page_table, *([cache_t] * SLC_PAGES), qbd, pen)


def _finish_sample_kernel(m_ref, l_ref, acc_ref, oc_ref, qbd_ref, pen_ref, slc_new_ref, win_new_ref,
                          win_ref, gates_ref, o_ref, *, past_len, new_blk):
    qbd = qbd_ref[...]
    tok = _iota((QROWS, LANES), 0) & 7
    col = _iota((QROWS, LANES), 1)
    pad = jnp.zeros((LANES - 8, K_W), F32)

    kn = jnp.concatenate([slc_new_ref[:, 0:K_W], pad], axis=0).astype(BF16)
    vn = jnp.concatenate([slc_new_ref[:, K_W:], pad], axis=0).astype(BF16)
    pen_col = pen_ref[new_blk // LANES][:, new_blk % LANES:new_blk % LANES + 1].astype(F32)
    sn = jnp.where(col <= tok, _dot_nt(qbd, kn) + pen_col, NEG)
    m_old = m_ref[:, 0:1]
    m_new = jnp.maximum(m_old, jnp.max(sn, axis=1, keepdims=True))
    alpha = jnp.exp(m_old - m_new)
    pn = jnp.exp(sn - m_new)
    l_new = alpha * l_ref[:, 0:1] + jnp.sum(pn, axis=1, keepdims=True)
    o_slc = (alpha * acc_ref[...] + _dot(pn.astype(BF16), vn)) / l_new

    wlen = win_ref.shape[1]
    new_t = jnp.concatenate([win_new_ref[...], jnp.zeros((LANES - 8, KV_W), F32)], axis=0).T
    kw_t = jnp.concatenate([win_ref[0:K_W, :], new_t[0:K_W]], axis=1).astype(BF16)
    vw_t = jnp.concatenate([win_ref[K_W:, :], new_t[K_W:]], axis=1).astype(BF16)
    nkw = wlen + LANES
    idx = _iota((QROWS, nkw), 1)
    t_row = past_len + (_iota((QROWS, nkw), 0) & 7)
    kpos = past_len - wlen + idx
    dist = t_row - kpos
    wmask = jnp.logical_and(jnp.logical_and(dist >= 0, dist <= WINDOW),
                            jnp.logical_and(kpos >= 0, idx < wlen + 8))
    sw = jnp.where(wmask, _dot(qbd, kw_t), NEG)
    mw = jnp.max(sw, axis=1, keepdims=True)
    ew = jnp.where(wmask, jnp.exp(sw - mw), 0.0)
    pw = ew / jnp.maximum(jnp.sum(ew, axis=1, keepdims=True), 1e-30)
    o_win = _dot_nt(pw.astype(BF16), vw_t)

    o_cmp = oc_ref[...]
    gates = gates_ref[...]
    lo = _iota((8, LANES), 1) < HEAD_DIM
    for c in range(D_MODEL // LANES):
        g = c // 2
        halves = []
        for e in range(2):
            r = 2 * (c % 2) + e
            rows = slice(r * 32 + g * 8, r * 32 + g * 8 + 8)
            lanes = slice((g // 2) * LANES, (g // 2 + 1) * LANES)
            base = g * LANES + 3 * r
            blk = (gates[:, base:base + 1] * o_cmp[rows, lanes] + gates[:, base + 1:base + 2] * o_slc[rows, lanes]
                   + gates[:, base + 2:base + 3] * o_win[rows, lanes])
            if g % 2 != e:
                blk = pltpu.roll(blk, HEAD_DIM, 1)
            halves.append(blk)
        o_ref[:, c * LANES:(c + 1) * LANES] = jnp.where(lo, halves[0], halves[1]).astype(BF16)


def _finish_sample(m, l, acc, oc, qbd, pen, slc_new, win_new, win3, gates, past_len, new_blk):
    db = m.shape[0]
    per_b = lambda shape: pl.BlockSpec((None,) + shape, lambda b: (b,) + (0,) * len(shape))
    ins = (m, l, acc, oc, qbd, pen, slc_new, win_new, win3, gates)
    return pl.pallas_call(
        functools.partial(_finish_sample_kernel, past_len=past_len, new_blk=new_blk),
        grid=(db,),
        in_specs=[per_b(a.shape[1:]) for a in ins],
        out_specs=per_b((8, D_MODEL)),
        out_shape=jax.ShapeDtypeStruct((db, 8, D_MODEL), BF16),
        compiler_params=_params(("parallel",)),
        name="finish_sample",
    )(*ins)


def _cmp_weights(w_c):
    eye = jnp.eye(N_KV, dtype=F32)
    bd = jnp.einsum("gh,lde->lgdhe", eye, w_c).reshape(CMP_LEN, K_W, K_W)
    return jnp.concatenate([bd[:CMP_STRIDE], bd[CMP_STRIDE:]], axis=2).astype(BF16)


def _cmp_pair_weights(w_c):
    eye = jnp.eye(2, dtype=F32)
    bd = jnp.einsum("gh,lde->lgdhe", eye, w_c).reshape(CMP_LEN, LANES, LANES)
    fs = jnp.concatenate([bd[:CMP_STRIDE], bd[CMP_STRIDE:]], axis=2)
    return jnp.concatenate([fs[:CMP_STRIDE // 2], fs[CMP_STRIDE // 2:]], axis=1).astype(BF16)


def _pos_operands(pe, w_c):
    pe_rows = jnp.tile(pe.reshape(1, CMP_LEN * HEAD_DIM), (SUBLANES, 1)).astype(BF16)
    w4 = jnp.tile(w_c.reshape(CMP_LEN * HEAD_DIM, HEAD_DIM), (1, N_KV)).astype(BF16)
    return pe_rows, w4


def _tile_gain(g, width):
    return jnp.tile(g, width // g.shape[0]).reshape(1, width).astype(F32)


def kernel(x_prompt, x_sample, cache_cmp_kv, cache_slc_kv, cache_win_kv, state_conv, state_ffn_conv,
           page_table, g_norm_mix, w_in, w_dw, b_dw, g_ln_conv, b_ln_conv, w_conv_out, g_q, g_k_cmp,
           g_k_slc, g_k_win, w_cmp_k, w_cmp_v, pe_cmp_k, pe_cmp_v, w_nsa_out, w_out, g_norm_ffn, w_up,
           w_ffn_dw, b_ffn_dw, w_down):
    bp, seq, _ = x_prompt.shape
    db, dseq, _ = x_sample.shape
    n_pool, page = cache_cmp_kv.shape[0], cache_cmp_kv.shape[1]
    n_pages = page_table.shape[1]
    past_len = n_pages * page
    wlen = cache_win_kv.shape[1]
    assert dseq == 8 and seq % QBLK == 0 and seq // SEL_BLOCK <= LANES and QBLK == WINDOW
    assert past_len % SEL_BLOCK == 0 and wlen == WINDOW and past_len >= WINDOW
    assert n_pages % CMP_PAGES == 0 and n_pages % SLC_PAGES == 0 and page % CMP_STRIDE == 0
    assert LANES % (SLC_PAGES * page // SEL_BLOCK) == 0

    offs = np.cumsum((2 * C_CONV, N_HEADS * HEAD_DIM, 3 * KV_W, 3 * N_HEADS))
    wu = w_in[:, :offs[0]].astype(BF16)
    wq = w_in[:, offs[0]:offs[1]].astype(BF16)
    wkv = w_in[:, offs[1]:offs[2]].astype(BF16)
    wgn = jnp.pad(w_in[:, offs[2]:offs[3]].reshape(D_MODEL, N_KV, 3 * GROUP),
                  ((0, 0), (0, 0), (0, LANES - 3 * GROUP))).reshape(D_MODEL, N_KV * LANES).astype(BF16)
    wgm = w_in[:, offs[3]:].astype(BF16)
    bd16, ex16 = _head_sum_mats(D_MODEL)
    bd4, ex4 = _head_sum_mats(K_W)
    in_wts = (g_norm_mix.reshape(1, D_MODEL), wu, wq, wkv, wgn, wgm, _tile_gain(g_q, D_MODEL),
              _tile_gain(g_k_slc, K_W), _tile_gain(g_k_win, K_W), bd16, ex16, bd4, ex4)
    wdw = jnp.pad(w_dw, ((0, 1), (0, 0)))
    conv_wts = (wdw, b_dw.reshape(1, C_CONV), g_ln_conv.reshape(1, C_CONV), b_ln_conv.reshape(1, C_CONV),
                w_conv_out.astype(BF16))
    wno = w_nsa_out.astype(BF16)
    wout = w_out.astype(BF16)
    gffn = g_norm_ffn.reshape(1, D_MODEL)
    wup = w_up.reshape(D_MODEL, 2 * FFN_NC, FFN_CW).transpose(1, 0, 2).astype(BF16)
    wdn = w_down.reshape(FFN_NC, FFN_CW, D_MODEL).astype(BF16)
    wfd = jnp.concatenate([w_ffn_dw, b_ffn_dw[None, :], jnp.zeros((4, 2 * D_FF), F32)], axis=0)
    wfd = wfd.reshape(SUBLANES, 2 * FFN_NC, FFN_CW).transpose(1, 0, 2)
    bdk = _cmp_weights(w_cmp_k)
    bdv = _cmp_weights(w_cmp_v)
    pek, wk4 = _pos_operands(pe_cmp_k, w_cmp_k)
    pev, wv4 = _pos_operands(pe_cmp_v, w_cmp_v)
    gkc = _tile_gain(g_k_cmp, K_W)
    cw_tail = (pek, wk4, pev, wv4, gkc, bd4, ex4)

    cos_p, sin_p = _rope_tables(jnp.arange(seq))
    (glu, q, cmp_rows, slc_rows, win_rows, gn, gm, k2s, v2s, k2w, v2w, cmp_slabs) = _inproj(
        x_prompt, cos_p, sin_p, in_wts, True)
    zeros_halo = jnp.zeros((bp, CONV_HALO, C_CONV), F32)
    a_out, conv_tail = _conv_module(glu, glu, zeros_halo, *conv_wts, tt=256, rb=64)
    nch = seq // CMP_STRIDE
    n_cmp = nch - 1
    n_slc = seq // SEL_BLOCK
    cos_c, sin_c = _rope_tables(jnp.arange(nch) * CMP_STRIDE + (CMP_LEN - 1))
    kc2, vc2 = _cmp_prompt(cmp_slabs, (bdk, bdv) + cw_tail, cos_c, sin_c)
    e_tab = jnp.asarray((np.arange(seq)[:, None] // SEL_BLOCK == np.arange(LANES)[None, :]), BF16)
    ov = jnp.asarray(_overlap_t(LANES, nch, 0, n_cmp).T * (np.arange(LANES) < n_slc), BF16)
    gex_np = np.zeros((2 * LANES, 3 * GROUP * LANES), np.float32)
    for k in range(3 * GROUP):
        gex_np[k, k * LANES:(k + 1) * LANES] = 1.0
        gex_np[LANES + k, k * LANES:(k + 1) * LANES] = 1.0
    o_p = _attn_prompt(q, gn, kc2, vc2, k2s, v2s, k2w, v2w, e_tab, ov, jnp.asarray(gex_np, BF16),
                       n_cmp, n_slc, min(N_SEL, n_slc))
    n_tok = bp * seq
    x1 = _merge(o_p.reshape(n_tok, D_MODEL), a_out.reshape(n_tok, D_MODEL), gm.reshape(n_tok, 2 * D_MODEL),
                x_prompt.reshape(n_tok, D_MODEL), wno, wout, 512)
    zeros_fix = jnp.zeros((bp, SUBLANES, 2 * D_FF), F32)
    y_p, ffn_tail = _ffn(x1.reshape(bp, seq, D_MODEL), gffn, wup, wfd, wdn, zeros_fix, zeros_fix, 512, 512)

    kv5 = lambda rows, b, t: rows.reshape(b, t, 2, N_KV, HEAD_DIM)
    out_p = (y_p, kv5(cmp_rows, bp, seq), kv5(slc_rows, bp, seq),
             kv5(win_rows[:, seq - min(WINDOW, seq):], bp, min(WINDOW, seq)),
             conv_tail[:, CONV_HALO - (CONV_K - 1):], ffn_tail[:, :FFN_CONV_K - 1])

    n_s = db * dseq
    pos_s = past_len + (jnp.arange(TOK_TILE) % dseq)
    cos_s, sin_s = _rope_tables(pos_s)
    xs_pad = jnp.pad(x_sample.reshape(1, n_s, D_MODEL), ((0, 0), (0, TOK_TILE - n_s), (0, 0))) if n_s < TOK_TILE \
        else x_sample.reshape(1, n_s, D_MODEL)
    assert xs_pad.shape[1] == TOK_TILE
    (glu_s, q_s, cmp_s, slc_s, win_s, gn_s, gm_s) = [a[0, :n_s] for a in _inproj(xs_pad, cos_s, sin_s, in_wts, False)]
    st32 = jnp.pad(state_conv, ((0, 0), (CONV_HALO - (CONV_K - 1), 0), (0, 0)))
    a_s, conv_tail_s = _conv_module(glu_s.reshape(db, dseq, C_CONV), st32, st32, *conv_wts, tt=dseq, rb=dseq)

    pos_minor = lambda c: c.transpose(0, 2, 3, 4, 1).reshape(c.shape[0], KV_W, c.shape[1])
    cache_cmp_t = pos_minor(cache_cmp_kv)
    cache_slc_t = pos_minor(cache_slc_kv)
    kcs, vcs = _cmp_stream(page_table, cache_cmp_t, _cmp_pair_weights(w_cmp_k), _cmp_pair_weights(w_cmp_v))
    nch_s = past_len // CMP_STRIDE
    n_cmp_s = -(-(past_len + dseq) // CMP_STRIDE) - 1
    n_slc_s = -(-(past_len + dseq) // SEL_BLOCK)
    nsp = -(-n_slc_s // LANES) * LANES
    cos_cs, sin_cs = _rope_tables(jnp.arange(nch_s) * CMP_STRIDE + (CMP_STRIDE - 1))
    ovt_s = jnp.asarray(_overlap_t(nsp, nch_s, 1, n_cmp_s), BF16)
    qbd, oc_s, pen_s = _cmp_sample(kcs, vcs, q_s.reshape(db, dseq, D_MODEL), cw_tail, cos_cs, sin_cs, ovt_s,
                                   past_len, n_slc_s, min(N_SEL, n_slc_s))
    m_s, l_s, acc_s = _slc_stream(page_table, cache_slc_t, qbd, pen_s)
    win3 = cache_win_kv.reshape(db, wlen, KV_W)
    o_s = _finish_sample(m_s, l_s, acc_s, oc_s, qbd, pen_s, slc_s.reshape(db, dseq, KV_W),
                         win_s.reshape(db, dseq, KV_W), pos_minor(cache_win_kv),
                         gn_s.reshape(db, dseq, N_KV * LANES), past_len, past_len // SEL_BLOCK)
    x1_s = _merge(o_s.reshape(n_s, D_MODEL), a_s.reshape(n_s, D_MODEL), gm_s, x_sample.reshape(n_s, D_MODEL),
                  wno, wout, n_s)
    z1 = jnp.zeros((db, dseq - 1, 2 * D_FF), F32)
    fix1 = jnp.concatenate([state_ffn_conv[:, 1:2], z1], axis=1).reshape(1, n_s, 2 * D_FF)
    fix2 = jnp.concatenate([state_ffn_conv, z1[:, 1:]], axis=1).reshape(1, n_s, 2 * D_FF)
    y_s, up_s = _ffn(x1_s.reshape(1, n_s, D_MODEL), gffn, wup, wfd, wdn, fix1, fix2, n_s, dseq)

    win_all = jnp.concatenate([win3, win_s.reshape(db, dseq, KV_W)], axis=1)
    keep = min(WINDOW, past_len + dseq)
    out_s = (y_s.reshape(db, dseq, D_MODEL), kv5(cmp_s, db, dseq), kv5(slc_s, db, dseq),
             kv5(win_all[:, wlen + dseq - keep:], db, keep),
             conv_tail_s[:, CONV_HALO - (CONV_K - 1):],
             up_s.reshape(db, dseq, 2 * D_FF)[:, dseq - (FFN_CONV_K - 1):])
    return (out_p[0], out_s[0]) + out_p[1:] + out_s[1:]
```

```python
import functools

import numpy as np
import jax
import jax.numpy as jnp
from jax import lax
from jax.experimental import pallas as pl
from jax.experimental.pallas import tpu as pltpu

F32 = jnp.float32
BF16 = jnp.bfloat16

D_MODEL = 1024
N_HEADS = 16
HEAD_DIM = 64
N_KV = 4
GROUP = N_HEADS // N_KV
CMP_STRIDE = 16
CMP_LEN = 2 * CMP_STRIDE
SEL_BLOCK = 64
N_SEL = 16
WINDOW = 512
C_CONV = D_MODEL // 2
CONV_K = 31
FFN_CONV_K = 3
D_FF = 2816
ROPE_THETA = 10000.0
EPS = 1e-6
KV_W = 2 * N_KV * HEAD_DIM
K_W = N_KV * HEAD_DIM

LANES = 128
SUBLANES = 8
QBLK = WINDOW
TOK_TILE = 256
ATT_GROUPS = 2
CONV_HALO = 32
VMEM_LIMIT = 56 * 1024 * 1024
FFN_CW = 256
FFN_NC = D_FF // FFN_CW
CMP_PAGES = 32
CMP_GROUP = 8
SLC_PAGES = 64
QROWS = GROUP * N_KV * 8

NEG = -1e30
PEN = -1e30
RANK_FORCED = 1e30
RANK_INVALID = -1e30
RANK_REMOVED = -2e30


def _dot(a, b):
    return jnp.dot(a, b, preferred_element_type=F32)


def _dot_nt(a, b):
    return lax.dot_general(a, b, (((1,), (1,)), ((), ())), preferred_element_type=F32)


def _sigmoid(x):
    return 1.0 / (1.0 + jnp.exp(-x))


def _iota(shape, dim):
    return lax.broadcasted_iota(jnp.int32, shape, dim)


def _params(sem):
    return pltpu.CompilerParams(dimension_semantics=sem, vmem_limit_bytes=VMEM_LIMIT)


def _const_spec(a, ngrid):
    nd = a.ndim
    if ngrid == 1:
        return pl.BlockSpec(a.shape, lambda i: (0,) * nd)
    if ngrid == 2:
        return pl.BlockSpec(a.shape, lambda i, j: (0,) * nd)
    return pl.BlockSpec(a.shape, lambda i, j, k: (0,) * nd)


def _rope_tables(pos):
    half = HEAD_DIM // 2
    inv = ROPE_THETA ** (-jnp.arange(half, dtype=F32) * (2.0 / HEAD_DIM))
    ang = pos.astype(F32)[:, None] * inv[None, :]
    cos, sin = jnp.cos(ang), jnp.sin(ang)
    return jnp.tile(cos, (1, 4)), jnp.tile(jnp.concatenate([-sin, sin], axis=1), (1, 2))


def _head_sum_mats(width):
    heads = width // HEAD_DIM
    lane_head = np.arange(width) // HEAD_DIM
    bd = (lane_head[:, None] == np.arange(LANES)[None, :]).astype(np.float32)
    ex = np.zeros((2 * LANES, width), np.float32)
    ex[:heads] = (np.arange(heads)[:, None] == lane_head[None, :])
    ex[LANES:LANES + heads] = ex[:heads]
    return jnp.asarray(bd, BF16), jnp.asarray(ex, BF16)


def _overlap_t(n_slc_rows, n_cols, col_shift, n_cmp):
    n = np.arange(n_cols) - col_shift
    j = np.arange(n_slc_rows)
    cs = n * CMP_STRIDE
    ss = j * SEL_BLOCK
    ov = (cs[None, :] < ss[:, None] + SEL_BLOCK) & (cs[None, :] + CMP_LEN > ss[:, None])
    ov &= (n[None, :] >= 0) & (n[None, :] < n_cmp)
    return ov.astype(np.float32)


def _head_rmsnorm(x, gain, bd, ex):
    ssum = _dot((x * x).astype(BF16), bd)
    r = lax.rsqrt(ssum * (1.0 / HEAD_DIM) + EPS)
    r_hi = r.astype(BF16)
    r_lo = (r - r_hi.astype(F32)).astype(BF16)
    rfull = _dot(jnp.concatenate([r_hi, r_lo], axis=1), ex)
    return x * rfull * gain


def _rope128(y, cos, sin):
    lane = _iota(y.shape, 1)
    first = (lane & (HEAD_DIM - 1)) < (HEAD_DIM // 2)
    rot = jnp.where(first, pltpu.roll(y, LANES - HEAD_DIM // 2, 1), pltpu.roll(y, HEAD_DIM // 2, 1))
    return y * cos + rot * sin


def _dup_heads(chunk):
    lo = _iota(chunk.shape, 1) < HEAD_DIM
    swapped = pltpu.roll(chunk, HEAD_DIM, 1)
    return jnp.where(lo, chunk, swapped), jnp.where(lo, swapped, chunk)


def _rank_select(val, n_rounds):
    rows = val.shape[0]
    j = _iota(val.shape, 0).astype(F32)
    sel = jnp.zeros(val.shape, F32)
    for _ in range(n_rounds):
        mx = jnp.max(val, axis=0, keepdims=True)
        idx = jnp.min(jnp.where(val == mx, j, float(rows)), axis=0, keepdims=True)
        hit = j == idx
        sel = jnp.where(hit, 1.0, sel)
        val = jnp.where(hit, RANK_REMOVED, val)
    return sel > 0.5


def _inproj_kernel(x_ref, gmix_ref, wu_ref, wq_ref, wkv_ref, wgn_ref, wgm_ref, gq_ref, gks_ref,
                   gkw_ref, bd16_ref, ex16_ref, bd4_ref, ex4_ref, cos_ref, sin_ref,
                   glu_ref, q_ref, cmp_ref, slc_ref, win_ref, gn_ref, gm_ref, *attn_refs):
    x = x_ref[...]
    ms = jnp.mean(x * x, axis=-1, keepdims=True)
    h = (x * lax.rsqrt(ms + EPS) * gmix_ref[...]).astype(BF16)
    cos = cos_ref[...]
    sin = sin_ref[...]

    u = _dot(h, wu_ref[...])
    glu_ref[...] = u[:, :C_CONV] * _sigmoid(u[:, C_CONV:])

    yq = _head_rmsnorm(_dot(h, wq_ref[...]), gq_ref[...], bd16_ref[...], ex16_ref[...])
    scale = HEAD_DIM ** -0.5
    for c in range(D_MODEL // LANES):
        sl = slice(c * LANES, (c + 1) * LANES)
        q_ref[:, sl] = (_rope128(yq[:, sl], cos, sin) * scale).astype(BF16)

    zkv = _dot(h, wkv_ref[...])
    cmp_ref[...] = zkv[:, :KV_W]

    def kv_branch(z, gain_ref, rows_ref, k2_ref, v2_ref):
        yk = _head_rmsnorm(z[:, :K_W], gain_ref[...], bd4_ref[...], ex4_ref[...])
        v = z[:, K_W:]
        rows_ref[:, K_W:] = v
        for c in range(K_W // LANES):
            sl = slice(c * LANES, (c + 1) * LANES)
            kr = _rope128(yk[:, sl], cos, sin)
            rows_ref[:, sl] = kr
            if k2_ref is not None:
                for ref, chunk in ((k2_ref, kr), (v2_ref, v[:, sl])):
                    a2, b2 = _dup_heads(chunk)
                    ref[2 * c] = a2.astype(BF16)
                    ref[2 * c + 1] = b2.astype(BF16)

    if attn_refs:
        k2s_ref, v2s_ref, k2w_ref, v2w_ref, cslab_ref = attn_refs
        for c in range(KV_W // LANES):
            cslab_ref[c] = zkv[:, c * LANES:(c + 1) * LANES]
    else:
        k2s_ref = v2s_ref = k2w_ref = v2w_ref = None
    kv_branch(zkv[:, KV_W:2 * KV_W], gks_ref, slc_ref, k2s_ref, v2s_ref)
    kv_branch(zkv[:, 2 * KV_W:], gkw_ref, win_ref, k2w_ref, v2w_ref)

    gn_ref[...] = _sigmoid(_dot(h, wgn_ref[...]))
    gm_ref[...] = _sigmoid(_dot(h, wgm_ref[...]))


def _inproj(x, cos, sin, wts, attn_layouts):
    B, T, _ = x.shape
    tm = TOK_TILE
    assert T % tm == 0
    nt = T // tm
    tok = lambda w: pl.BlockSpec((None, tm, w), lambda b, t: (b, t, 0))
    tab = pl.BlockSpec((tm, LANES), lambda b, t: (t, 0))
    in_specs = [tok(D_MODEL)] + [_const_spec(a, 2) for a in wts] + [tab, tab]
    out_shape = [
        jax.ShapeDtypeStruct((B, T, C_CONV), F32),
        jax.ShapeDtypeStruct((B, T, D_MODEL), BF16),
        jax.ShapeDtypeStruct((B, T, KV_W), F32),
        jax.ShapeDtypeStruct((B, T, KV_W), F32),
        jax.ShapeDtypeStruct((B, T, KV_W), F32),
        jax.ShapeDtypeStruct((B, T, N_KV * LANES), F32),
        jax.ShapeDtypeStruct((B, T, 2 * D_MODEL), F32),
    ]
    out_specs = [tok(C_CONV), tok(D_MODEL), tok(KV_W), tok(KV_W), tok(KV_W), tok(N_KV * LANES),
                 tok(2 * D_MODEL)]
    if attn_layouts:
        dup = jax.ShapeDtypeStruct((B, N_KV, T, LANES), BF16)
        dup_spec = pl.BlockSpec((None, N_KV, tm, LANES), lambda b, t: (b, 0, t, 0))
        out_shape += [dup] * 4 + [jax.ShapeDtypeStruct((B, KV_W // LANES, T, LANES), F32)]
        out_specs += [dup_spec] * 5
    return pl.pallas_call(
        _inproj_kernel,
        grid=(B, nt),
        in_specs=in_specs,
        out_specs=out_specs,
        out_shape=out_shape,
        compiler_params=_params(("parallel", "parallel")),
        name="inproj",
    )(x, *wts, cos, sin)


def _conv_kernel(glu_ref, halo_ref, st_ref, wdw_ref, bdw_ref, gln_ref, bln_ref, wco_ref,
                 a_ref, cst_ref, xc_ref, s_ref, *, tt, rb):
    ti = pl.program_id(1)

    @pl.when(ti == 0)
    def _():
        xc_ref[0:CONV_HALO, :] = st_ref[...]

    @pl.when(ti > 0)
    def _():
        xc_ref[0:CONV_HALO, :] = halo_ref[...]

    xc_ref[CONV_HALO:CONV_HALO + tt, :] = glu_ref[...]
    xc_ref[CONV_HALO + tt:CONV_HALO + tt + SUBLANES, :] = jnp.zeros((SUBLANES, C_CONV), F32)
    first = CONV_HALO - (CONV_K - 1)
    for r0 in range(0, tt, rb):
        acc = jnp.broadcast_to(bdw_ref[...], (rb, C_CONV))
        for phi in range(SUBLANES):
            y = None
            for u in range(first, first + CONV_K):
                if u % SUBLANES == phi:
                    term = wdw_ref[u - first:u - first + 1, :] * xc_ref[r0 + u - phi:r0 + u - phi + rb + SUBLANES, :]
                    y = term if y is None else y + term
            acc = acc + y[phi:phi + rb]
        mu = jnp.mean(acc, axis=-1, keepdims=True)
        d = acc - mu
        var = jnp.mean(d * d, axis=-1, keepdims=True)
        y = d * lax.rsqrt(var + EPS) * gln_ref[...] + bln_ref[...]
        s_ref[r0:r0 + rb, :] = y * _sigmoid(y)
    a_ref[...] = _dot(s_ref[...].astype(BF16), wco_ref[...])
    cst_ref[...] = xc_ref[tt:tt + CONV_HALO, :]


def _conv_module(glu, halo_src, state32, wdw, bdw, gln, bln, wco, tt, rb):
    B, T, _ = glu.shape
    nt = T // tt
    hb = max(tt // CONV_HALO, 1)
    return pl.pallas_call(
        functools.partial(_conv_kernel, tt=tt, rb=rb),
        grid=(B, nt),
        in_specs=[
            pl.BlockSpec((None, tt, C_CONV), lambda b, t: (b, t, 0)),
            pl.BlockSpec((None, CONV_HALO, C_CONV), lambda b, t: (b, jnp.maximum(t * hb - 1, 0), 0)),
            pl.BlockSpec((None, CONV_HALO, C_CONV), lambda b, t: (b, 0, 0)),
            _const_spec(wdw, 2), _const_spec(bdw, 2), _const_spec(gln, 2), _const_spec(bln, 2),
            _const_spec(wco, 2),
        ],
        out_specs=[
            pl.BlockSpec((None, tt, D_MODEL), lambda b, t: (b, t, 0)),
            pl.BlockSpec((None, CONV_HALO, C_CONV), lambda b, t: (b, 0, 0)),
        ],
        out_shape=[
            jax.ShapeDtypeStruct((B, T, D_MODEL), F32),
            jax.ShapeDtypeStruct((B, CONV_HALO, C_CONV), F32),
        ],
        scratch_shapes=[pltpu.VMEM((CONV_HALO + tt + SUBLANES, C_CONV), F32), pltpu.VMEM((tt, C_CONV), F32)],
        compiler_params=_params(("parallel", "arbitrary")),
        name="conv_module",
    )(glu, halo_src, state32, wdw, bdw, gln, bln, wco)


def _merge_kernel(o_ref, a_ref, gm_ref, x_ref, wno_ref, wout_ref, x1_ref):
    b_out = _dot(o_ref[...], wno_ref[...])
    gm = gm_ref[...]
    m = gm[:, :D_MODEL] * a_ref[...] + gm[:, D_MODEL:] * b_out
    x1_ref[...] = x_ref[...] + _dot(m.astype(BF16), wout_ref[...])


def _merge(o, a_out, gm, x, wno, wout, tm):
    N = x.shape[0]
    row = lambda w: pl.BlockSpec((tm, w), lambda i: (i, 0))
    return pl.pallas_call(
        _merge_kernel,
        grid=(N // tm,),
        in_specs=[row(D_MODEL), row(D_MODEL), row(2 * D_MODEL), row(D_MODEL),
                  _const_spec(wno, 1), _const_spec(wout, 1)],
        out_specs=row(D_MODEL),
        out_shape=jax.ShapeDtypeStruct((N, D_MODEL), F32),
        compiler_params=_params(("parallel",)),
        name="merge",
    )(o, a_out, gm, x, wno, wout)


def _ffn_kernel(x1_ref, gn_ref, wup_ref, wdw_ref, wdn_ref, fix1_ref, fix2_ref,
                y_ref, st_ref, prev_ref, acc_ref, *, tm, seg):
    streaming = seg == tm
    x1 = x1_ref[...]
    ms = jnp.mean(x1 * x1, axis=-1, keepdims=True)
    h = (x1 * lax.rsqrt(ms + EPS) * gn_ref[...]).astype(BF16)
    row = _iota((tm, FFN_CW), 0) & (seg - 1)

    if streaming:
        @pl.when(pl.program_id(1) == 0)
        def _():
            prev_ref[...] = fix2_ref[...]

    def conv3(c):
        cs = slice(c * FFN_CW, (c + 1) * FFN_CW)
        up = _dot(h, wup_ref[c])
        r1 = pltpu.roll(up, 1, 0)
        r2 = pltpu.roll(up, 2, 0)
        if streaming:
            p0 = prev_ref[0:1, cs]
            p1 = prev_ref[1:2, cs]
            s1 = jnp.where(row == 0, p1, r1)
            s2 = jnp.where(row == 0, p0, jnp.where(row == 1, p1, r2))
            prev_ref[0:2, cs] = up[tm - 2:tm, :]
        else:
            s1 = jnp.where(row == 0, fix1_ref[:, cs], r1)
            s2 = jnp.where(row < 2, fix2_ref[:, cs], r2)
            st_ref[:, cs] = up
        w = wdw_ref[c]
        return w[0:1] * s2 + w[1:2] * s1 + w[2:3] * up + w[3:4]

    acc_ref[...] = x1
    for c in range(FFN_NC):
        gate = conv3(c)
        val = conv3(FFN_NC + c)
        act = gate * _sigmoid(gate) * val
        acc_ref[...] += _dot(act.astype(BF16), wdn_ref[c])
    y_ref[...] = acc_ref[...]
    if streaming:
        st_ref[...] = prev_ref[...]


def _ffn(x1, gn, wup, wdw, wdn, fix1, fix2, tm, seg):
    B, T, _ = x1.shape
    nt = T // tm
    streaming = seg == tm
    tok = lambda w: pl.BlockSpec((None, tm, w), lambda b, t: (b, t, 0))
    if streaming:
        fix_specs = [pl.BlockSpec((None, SUBLANES, 2 * D_FF), lambda b, t: (b, 0, 0))] * 2
        st_shape = jax.ShapeDtypeStruct((B, SUBLANES, 2 * D_FF), F32)
        st_spec = pl.BlockSpec((None, SUBLANES, 2 * D_FF), lambda b, t: (b, 0, 0))
    else:
        fix_specs = [tok(2 * D_FF)] * 2
        st_shape = jax.ShapeDtypeStruct((B, T, 2 * D_FF), F32)
        st_spec = tok(2 * D_FF)
    return pl.pallas_call(
        functools.partial(_ffn_kernel, tm=tm, seg=seg),
        grid=(B, nt),
        in_specs=[tok(D_MODEL), _const_spec(gn, 2), _const_spec(wup, 2), _const_spec(wdw, 2),
                  _const_spec(wdn, 2)] + fix_specs,
        out_specs=[tok(D_MODEL), st_spec],
        out_shape=[jax.ShapeDtypeStruct((B, T, D_MODEL), F32), st_shape],
        scratch_shapes=[pltpu.VMEM((SUBLANES, 2 * D_FF), F32), pltpu.VMEM((tm, D_MODEL), F32)],
        compiler_params=_params(("parallel", "arbitrary")),
        name="conv_ffn",
    )(x1, gn, wup, wdw, wdn, fix1, fix2)


def _cmp_prompt_kernel(rows_ref, bdk_ref, bdv_ref, pek_ref, wk4_ref, pev_ref, wv4_ref, gk_ref,
                       bd4_ref, ex4_ref, cos_ref, sin_ref, kc2_ref, vc2_ref, *, nch):
    acc_k = jnp.zeros((nch, 2 * K_W), F32)
    acc_v = jnp.zeros((nch, 2 * K_W), F32)
    for l in range(CMP_STRIDE):
        x = [rows_ref[c, pl.ds(l, nch, stride=CMP_STRIDE), :] for c in range(KV_W // LANES)]
        acc_k += _dot(jnp.concatenate(x[:2], axis=1).astype(BF16), bdk_ref[l])
        acc_v += _dot(jnp.concatenate(x[2:], axis=1).astype(BF16), bdv_ref[l])
    pos_k = _dot(pek_ref[...], wk4_ref[...])[0:1]
    pos_v = _dot(pev_ref[...], wv4_ref[...])[0:1]
    kc = acc_k[:, :K_W] + pltpu.roll(acc_k[:, K_W:], nch - 1, 0) + pos_k
    vc = acc_v[:, :K_W] + pltpu.roll(acc_v[:, K_W:], nch - 1, 0) + pos_v
    yk = _head_rmsnorm(kc, gk_ref[...], bd4_ref[...], ex4_ref[...])
    for c in range(K_W // LANES):
        sl = slice(c * LANES, (c + 1) * LANES)
        for ref, chunk in ((kc2_ref, _rope128(yk[:, sl], cos_ref[...], sin_ref[...])), (vc2_ref, vc[:, sl])):
            a2, b2 = _dup_heads(chunk)
            ref[2 * c] = a2.astype(BF16)
            ref[2 * c + 1] = b2.astype(BF16)


def _cmp_prompt(rows, cw, cos, sin):
    B, nslab, T, _ = rows.shape
    nch = T // CMP_STRIDE
    return pl.pallas_call(
        functools.partial(_cmp_prompt_kernel, nch=nch),
        grid=(B,),
        in_specs=[pl.BlockSpec((None, nslab, T, LANES), lambda b: (b, 0, 0, 0))] + [_const_spec(a, 1) for a in cw]
        + [_const_spec(cos, 1), _const_spec(sin, 1)],
        out_specs=[pl.BlockSpec((None, N_KV, nch, LANES), lambda b: (b, 0, 0, 0))] * 2,
        out_shape=[jax.ShapeDtypeStruct((B, N_KV, nch, LANES), BF16)] * 2,
        compiler_params=_params(("parallel",)),
        name="cmp_prompt",
    )(rows, *cw, cos, sin)


def _attn_prompt_kernel(q_ref, gates_ref, kc2_ref, vc2_ref, k2s_ref, v2s_ref, k2w_ref, v2w_ref,
                        e_ref, ov_ref, gex_ref, o_ref, m_ref, l_ref, acc_ref, mw_ref, lw_ref, accw_ref,
                        *, n_cmp, n_slc, n_top):
    qi = pl.program_id(2)
    qs = qi * QBLK
    ng, nch = kc2_ref.shape[0], kc2_ref.shape[1]
    lo = _iota((QBLK, LANES), 1) < HEAD_DIM
    zero = jnp.zeros((QBLK, LANES), BF16)
    last = jnp.minimum((qs + _iota((QBLK, nch), 0) - (CMP_LEN - 1)) >> 4, n_cmp - 1)
    cmask = _iota((QBLK, nch), 1) <= last

    def masked(x, fill):
        return jnp.concatenate([jnp.where(cmask, x[r * QBLK:(r + 1) * QBLK], fill) for r in range(GROUP)], axis=0)

    kl = _iota((QBLK, QBLK), 1)
    tl = _iota((QBLK, QBLK), 0)
    tri = {"causal": kl <= tl, "band": kl >= tl}
    slc_state = lambda gi: (m_ref.at[gi], l_ref.at[gi], acc_ref.at[gi])
    win_state = lambda gi: (mw_ref.at[gi], lw_ref.at[gi], accw_ref.at[gi])

    def reset(state):
        m_st, l_st, acc_st = state
        m_st[...] = jnp.full(m_st.shape, NEG, F32)
        l_st[...] = jnp.zeros(l_st.shape, F32)
        acc_st[...] = jnp.zeros(acc_st.shape, F32)

    def attend(state, kblk, v2blk, q_rows, mode):
        m_st, l_st, acc_st = state
        sc = _dot_nt(q_rows, kblk)
        if mode != "none":
            sc = jnp.concatenate([jnp.where(tri[mode], sc[r * QBLK:(r + 1) * QBLK], NEG)
                                  for r in range(GROUP)], axis=0)
        m_old = m_st[...]
        m_new = jnp.maximum(m_old, jnp.max(sc, axis=1, keepdims=True))
        alpha = jnp.exp(m_old - m_new)
        p = jnp.exp(sc - jnp.concatenate([m_new] * (sc.shape[1] // LANES), axis=1))
        l_st[...] = alpha * l_st[...] + jnp.sum(p, axis=1, keepdims=True)
        acc_st[...] = alpha * acc_st[...] + _dot(p.astype(BF16), v2blk)
        m_st[...] = m_new

    def blk(ref, kb):
        return ref[pl.ds(pl.multiple_of(kb * QBLK, QBLK), QBLK), :]

    def front(gi):
        qm = []
        for p in range(GROUP // 2):
            qp = q_ref[:, gi * K_W + p * LANES:gi * K_W + (p + 1) * LANES]
            qm += [jnp.where(lo, qp, zero), jnp.where(lo, zero, qp)]
        q_all = jnp.concatenate(qm, axis=0)

        s = masked(_dot_nt(q_all, kc2_ref[gi]), NEG)
        mx = jnp.max(s, axis=1, keepdims=True)
        e = masked(jnp.exp(s - mx), 0.0)
        p_cmp = (e * (1.0 / jnp.maximum(jnp.sum(e, axis=1, keepdims=True), 1e-30))).astype(BF16)
        o_cmp = _dot(p_cmp, vc2_ref[gi])
        imp4 = _dot(p_cmp, ov_ref[...])
        imp = imp4[0:QBLK]
        for r in range(1, GROUP):
            imp = imp + imp4[r * QBLK:(r + 1) * QBLK]

        reset(win_state(gi))
        attend(win_state(gi), blk(k2w_ref.at[gi], qi), blk(v2w_ref.at[gi], qi), q_all, "causal")

        imp_t = imp.T[0:n_slc]
        j = _iota((n_slc, QBLK), 0)
        tq = qs + _iota((n_slc, QBLK), 1)
        cur = tq >> 6
        forced = jnp.logical_or(j == 0, jnp.logical_or(j == cur, j == cur - 1))
        rank = jnp.where(forced, RANK_FORCED, jnp.where(j * SEL_BLOCK <= tq, imp_t, RANK_INVALID))
        pen_t = jnp.where(_rank_select(rank, n_top), 0.0, PEN)
        if n_slc < LANES:
            pen_t = jnp.concatenate([pen_t, jnp.zeros((LANES - n_slc, QBLK), F32)], axis=0)
        pen = pen_t.T.astype(BF16)
        qa_all = jnp.concatenate([q_all, jnp.concatenate([pen] * GROUP, axis=0)], axis=1)
        reset(slc_state(gi))
        return q_all, qa_all, o_cmp

    parts = [front(gi) for gi in range(ng)]

    def slc_step(kb, mode):
        e_blk = blk(e_ref, kb)
        for gi in range(ng):
            keys = jnp.concatenate([blk(k2s_ref.at[gi], kb), e_blk], axis=1)
            attend(slc_state(gi), keys, blk(v2s_ref.at[gi], kb), parts[gi][1], mode)

    def slc_body(kb, carry):
        slc_step(kb, "none")
        return carry

    lax.fori_loop(0, qi, slc_body, 0)
    slc_step(qi, "causal")

    @pl.when(qi >= 1)
    def _():
        for gi in range(ng):
            attend(win_state(gi), blk(k2w_ref.at[gi], qi - 1), blk(v2w_ref.at[gi], qi - 1), parts[gi][0], "band")

    for gi in range(ng):
        o_cmp = parts[gi][2]
        o_slc = acc_ref[gi] / l_ref[gi]
        o_win = accw_ref[gi] / lw_ref[gi]
        gt = gates_ref[:, gi * LANES:(gi + 1) * LANES]
        g_hi = gt.astype(BF16)
        g_lo = (gt - g_hi.astype(F32)).astype(BF16)
        gex = _dot(jnp.concatenate([g_hi, g_lo], axis=1), gex_ref[...])
        mixed = []
        for r in range(GROUP):
            rs = slice(r * QBLK, (r + 1) * QBLK)
            gl = lambda k: gex[:, (3 * r + k) * LANES:(3 * r + k + 1) * LANES]
            mixed.append(gl(0) * o_cmp[rs] + gl(1) * o_slc[rs] + gl(2) * o_win[rs])
        for p in range(GROUP // 2):
            o_ref[:, gi * K_W + p * LANES:gi * K_W + (p + 1) * LANES] = jnp.where(
                lo, mixed[2 * p], mixed[2 * p + 1]).astype(BF16)


def _attn_prompt(q, gates, kc2, vc2, k2s, v2s, k2w, v2w, e_tab, ov, gex, n_cmp, n_slc, n_top):
    B, T, _ = q.shape
    nq = T // QBLK
    nch = kc2.shape[2]
    rows = GROUP * QBLK
    ng = ATT_GROUPS
    per_bg = lambda n: pl.BlockSpec((None, ng, n, LANES), lambda b, g, i: (b, g, 0, 0))
    return pl.pallas_call(
        functools.partial(_attn_prompt_kernel, n_cmp=n_cmp, n_slc=n_slc, n_top=n_top),
        grid=(B, N_KV // ng, nq),
        in_specs=[
            pl.BlockSpec((None, QBLK, ng * K_W), lambda b, g, i: (b, i, g)),
            pl.BlockSpec((None, QBLK, ng * LANES), lambda b, g, i: (b, i, g)),
            per_bg(nch), per_bg(nch), per_bg(T), per_bg(T), per_bg(T), per_bg(T),
            _const_spec(e_tab, 3), _const_spec(ov, 3), _const_spec(gex, 3),
        ],
        out_specs=pl.BlockSpec((None, QBLK, ng * K_W), lambda b, g, i: (b, i, g)),
        out_shape=jax.ShapeDtypeStruct((B, T, D_MODEL), BF16),
        scratch_shapes=[pltpu.VMEM((ng, rows, LANES), F32)] * 6,
        compiler_params=_params(("parallel", "parallel", "arbitrary")),
        name="attn_prompt",
    )(q, gates, kc2, vc2, k2s, v2s, k2w, v2w, e_tab, ov, gex)


def _cmp_stream_kernel(pt_ref, *refs, m):
    pages = refs[:CMP_PAGES]
    bdk_ref, bdv_ref, kc_ref, vc_ref, fs_ref, slab_ref = refs[CMP_PAGES:]
    s = pl.program_id(1)
    per_page = m // CMP_PAGES
    nslab = KV_W // LANES

    @pl.when(s == 0)
    def _():
        fs_ref[0:SUBLANES, :] = jnp.zeros((SUBLANES, 2 * K_W), F32)

    half = K_W // 2

    def page_group(k0, k1):
        mg = (k1 - k0) * per_page
        for k in range(k0, k1):
            for c in range(nslab):
                slab_ref[k, c] = pages[k][c * LANES:(c + 1) * LANES, :].T

        def rows_of(c, l):
            return jnp.concatenate([slab_ref[k, c, pl.ds(l, per_page, stride=CMP_STRIDE), :]
                                    for k in range(k0, k1)], axis=0)

        def pair_rows(c, l):
            return jnp.concatenate([rows_of(c, l), rows_of(c, l + CMP_STRIDE // 2)], axis=1)

        acc_k = jnp.zeros((2 * mg, K_W), F32)
        acc_v = jnp.zeros((2 * mg, K_W), F32)
        for l in range(CMP_STRIDE // 2):
            xk = jnp.concatenate([pair_rows(0, l), pair_rows(1, l)], axis=0)
            xv = jnp.concatenate([pair_rows(2, l), pair_rows(3, l)], axis=0)
            acc_k += _dot(xk.astype(BF16), bdk_ref[l])
            acc_v += _dot(xv.astype(BF16), bdv_ref[l])
        return (jnp.concatenate([acc_k[:mg, :half], acc_k[mg:, :half], acc_v[:mg, :half], acc_v[mg:, :half]], axis=1),
                jnp.concatenate([acc_k[:mg, half:], acc_k[mg:, half:]], axis=1),
                jnp.concatenate([acc_v[:mg, half:], acc_v[mg:, half:]], axis=1))

    groups = [page_group(k0, k0 + CMP_GROUP) for k0 in range(0, CMP_PAGES, CMP_GROUP)]
    first, second_k, second_v = [jnp.concatenate([g[i] for g in groups], axis=0) for i in range(3)]
    fs_ref[SUBLANES:SUBLANES + m, :] = first
    shifted = fs_ref[SUBLANES - 1:SUBLANES - 1 + m, :]
    kc_ref[...] = shifted[:, :K_W] + second_k
    vc_ref[...] = shifted[:, K_W:] + second_v
    fs_ref[SUBLANES - 1:SUBLANES, :] = fs_ref[SUBLANES - 1 + m:SUBLANES + m, :]


def _cmp_stream(page_table, cache_t, bdk, bdv):
    db, n_pages = page_table.shape
    page = cache_t.shape[2]
    per_page = page // CMP_STRIDE
    m = CMP_PAGES * per_page
    nch = n_pages * per_page
    nsteps = n_pages // CMP_PAGES

    def page_spec(k):
        return pl.BlockSpec((None, KV_W, page), lambda b, s, pt: (pt[b, s * CMP_PAGES + k], 0, 0))

    grid_spec = pltpu.PrefetchScalarGridSpec(
        num_scalar_prefetch=1,
        grid=(db, nsteps),
        in_specs=[page_spec(k) for k in range(CMP_PAGES)]
        + [pl.BlockSpec(bdk.shape, lambda b, s, pt: (0, 0, 0)), pl.BlockSpec(bdv.shape, lambda b, s, pt: (0, 0, 0))],
        out_specs=[pl.BlockSpec((None, m, K_W), lambda b, s, pt: (b, s, 0))] * 2,
        scratch_shapes=[pltpu.VMEM((SUBLANES + m, 2 * K_W), F32),
                        pltpu.VMEM((CMP_PAGES, KV_W // LANES, page, LANES), F32)],
    )
    return pl.pallas_call(
        functools.partial(_cmp_stream_kernel, m=m),
        grid_spec=grid_spec,
        out_shape=[jax.ShapeDtypeStruct((db, nch, K_W), F32)] * 2,
        compiler_params=_params(("parallel", "arbitrary")),
        name="cmp_stream",
    )(page_table, *([cache_t] * CMP_PAGES), bdk, bdv)


def _block_diag_q(q):
    lo = _iota((8, LANES), 1) < HEAD_DIM
    zero = jnp.zeros((8, LANES), BF16)
    blocks = []
    for r in range(GROUP):
        for g in range(N_KV):
            hh = g * GROUP + r
            pair = q[:, (hh // 2) * LANES:(hh // 2 + 1) * LANES]
            if hh % 2 != g % 2:
                pair = pltpu.roll(pair.astype(F32), HEAD_DIM, 1).astype(BF16)
            keep = jnp.where(lo, pair, zero) if g % 2 == 0 else jnp.where(lo, zero, pair)
            blocks.append(jnp.concatenate([keep, zero] if g < 2 else [zero, keep], axis=1))
    return jnp.concatenate(blocks, axis=0)


def _cmp_sample_kernel(kcs_ref, vcs_ref, q_ref, pek_ref, wk4_ref, pev_ref, wv4_ref, gk_ref, bd4_ref,
                       ex4_ref, cos_ref, sin_ref, ovt_ref, qbd_ref, oc_ref, pen_ref,
                       *, past_len, n_slc, n_top):
    nch = kcs_ref.shape[0]
    pos_k = _dot(pek_ref[...], wk4_ref[...])[0:1]
    pos_v = _dot(pev_ref[...], wv4_ref[...])[0:1]
    yk = _head_rmsnorm(kcs_ref[...] + pos_k, gk_ref[...], bd4_ref[...], ex4_ref[...])
    kcn = jnp.concatenate([_rope128(yk[:, c * LANES:(c + 1) * LANES], cos_ref[...], sin_ref[...])
                           for c in range(K_W // LANES)], axis=1).astype(BF16)
    vc = (vcs_ref[...] + pos_v).astype(BF16)
    qbd = _block_diag_q(q_ref[...])
    qbd_ref[...] = qbd

    s = _dot_nt(qbd, kcn)
    mi = _iota((QROWS, nch), 1)
    t_row = past_len + (_iota((QROWS, nch), 0) & 7)
    mask = jnp.logical_and(mi * CMP_STRIDE + (CMP_STRIDE - 1) <= t_row, mi >= 1)
    s = jnp.where(mask, s, NEG)
    mx = jnp.max(s, axis=1, keepdims=True)
    e = jnp.where(mask, jnp.exp(s - mx), 0.0)
    p = (e / jnp.maximum(jnp.sum(e, axis=1, keepdims=True), 1e-30)).astype(BF16)
    oc_ref[...] = _dot(p, vc)

    imp = _dot_nt(ovt_ref[...], p)
    imp = imp + pltpu.roll(imp, 32, 1) + pltpu.roll(imp, 64, 1) + pltpu.roll(imp, 96, 1)
    nsp = imp.shape[0]
    j = _iota((nsp, QROWS), 0)
    t_lane = past_len + (_iota((nsp, QROWS), 1) & 7)
    cur = t_lane >> 6
    forced = jnp.logical_or(j == 0, jnp.logical_or(j == cur, j == cur - 1))
    rank = jnp.where(forced, RANK_FORCED, jnp.where(j * SEL_BLOCK <= t_lane, imp, RANK_INVALID))
    rank = jnp.where(j < n_slc, rank, RANK_REMOVED)
    sel = _rank_select(rank, n_top)
    pen_t = jnp.where(sel, 0.0, PEN)
    for blk in range(nsp // LANES):
        pen_ref[blk] = pen_t[blk * LANES:(blk + 1) * LANES, :].T.astype(BF16)


def _cmp_sample(kcs, vcs, q, cw_tail, cos, sin, ovt, past_len, n_slc, n_top):
    db, nch, _ = kcs.shape
    nsp = ovt.shape[0]
    per_b = lambda shape: pl.BlockSpec((None,) + shape, lambda b: (b,) + (0,) * len(shape))
    return pl.pallas_call(
        functools.partial(_cmp_sample_kernel, past_len=past_len, n_slc=n_slc, n_top=n_top),
        grid=(db,),
        in_specs=[per_b((nch, K_W)), per_b((nch, K_W)), per_b((8, D_MODEL))]
        + [_const_spec(a, 1) for a in cw_tail] + [_const_spec(cos, 1), _const_spec(sin, 1), _const_spec(ovt, 1)],
        out_specs=[per_b((QROWS, K_W)), per_b((QROWS, K_W)), per_b((nsp // LANES, QROWS, LANES))],
        out_shape=[jax.ShapeDtypeStruct((db, QROWS, K_W), BF16),
                   jax.ShapeDtypeStruct((db, QROWS, K_W), F32),
                   jax.ShapeDtypeStruct((db, nsp // LANES, QROWS, LANES), BF16)],
        compiler_params=_params(("parallel",)),
        name="cmp_sample",
    )(kcs, vcs, q, *cw_tail, cos, sin, ovt)


def _slc_stream_kernel(pt_ref, *refs):
    pages = refs[:SLC_PAGES]
    qbd_ref, pen_ref, m_out, l_out, acc_out, m_ref, l_ref, acc_ref = refs[SLC_PAGES:]
    s = pl.program_id(1)
    page = pages[0].shape[1]
    nk = SLC_PAGES * page
    blocks_per_step = nk // SEL_BLOCK

    @pl.when(s == 0)
    def _():
        m_ref[...] = jnp.full(m_ref.shape, NEG, F32)
        l_ref[...] = jnp.zeros(l_ref.shape, F32)
        acc_ref[...] = jnp.zeros(acc_ref.shape, F32)

    k_t = jnp.concatenate([pg[0:K_W, :] for pg in pages], axis=1).astype(BF16)
    v_t = jnp.concatenate([pg[K_W:, :] for pg in pages], axis=1).astype(BF16)
    steps_per_lane_block = LANES // blocks_per_step
    pen = pen_ref[s // steps_per_lane_block]
    jrow = _iota((LANES, nk), 0)
    blk = (s % steps_per_lane_block) * blocks_per_step + (_iota((LANES, nk), 1) >> 6)
    expand = jnp.where(jrow == blk, 1.0, 0.0).astype(BF16)
    sc = _dot(qbd_ref[...], k_t) + _dot(pen, expand)
    m_old = m_ref[...]
    m_new = jnp.maximum(m_old, jnp.max(sc, axis=1, keepdims=True))
    alpha = jnp.exp(m_old - m_new)
    p = jnp.exp(sc - m_new)
    l_ref[...] = alpha * l_ref[...] + jnp.sum(p, axis=1, keepdims=True)
    acc_ref[...] = alpha * acc_ref[...] + _dot_nt(p.astype(BF16), v_t)
    m_ref[...] = m_new

    @pl.when(s == pl.num_programs(1) - 1)
    def _():
        m_out[...] = jnp.broadcast_to(m_ref[...], m_out.shape)
        l_out[...] = jnp.broadcast_to(l_ref[...], l_out.shape)
        acc_out[...] = acc_ref[...]


def _slc_stream(page_table, cache_t, qbd, pen):
    db, n_pages = page_table.shape
    page = cache_t.shape[2]
    nsteps = n_pages // SLC_PAGES

    def page_spec(k):
        return pl.BlockSpec((None, KV_W, page), lambda b, s, pt: (pt[b, s * SLC_PAGES + k], 0, 0))

    per_b = lambda shape: pl.BlockSpec((None,) + shape, lambda b, s, pt: (b,) + (0,) * len(shape))
    grid_spec = pltpu.PrefetchScalarGridSpec(
        num_scalar_prefetch=1,
        grid=(db, nsteps),
        in_specs=[page_spec(k) for k in range(SLC_PAGES)] + [per_b((QROWS, K_W)), per_b(pen.shape[1:])],
        out_specs=[per_b((QROWS, LANES)), per_b((QROWS, LANES)), per_b((QROWS, K_W))],
        scratch_shapes=[pltpu.VMEM((QROWS, 1), F32), pltpu.VMEM((QROWS, 1), F32), pltpu.VMEM((QROWS, K_W), F32)],
    )
    return pl.pallas_call(
        _slc_stream_kernel,
        grid_spec=grid_spec,
        out_shape=[jax.ShapeDtypeStruct((db, QROWS, LANES), F32), jax.ShapeDtypeStruct((db, QROWS, LANES), F32),
                   jax.ShapeDtypeStruct((db, QROWS, K_W), F32)],
        compiler_params=_params(("parallel", "arbitrary")),
        name="slc_stream",
    )(page_table, *([cache_t] * SLC_PAGES), qbd, pen)


def _finish_sample_kernel(m_ref, l_ref, acc_ref, oc_ref, qbd_ref, pen_ref, slc_new_ref, win_new_ref,
                          win_ref, gates_ref, o_ref, *, past_len, new_blk):
    qbd = qbd_ref[...]
    tok = _iota((QROWS, LANES), 0) & 7
    col = _iota((QROWS, LANES), 1)
    pad = jnp.zeros((LANES - 8, K_W), F32)

    kn = jnp.concatenate([slc_new_ref[:, 0:K_W], pad], axis=0).astype(BF16)
    vn = jnp.concatenate([slc_new_ref[:, K_W:], pad], axis=0).astype(BF16)
    pen_col = pen_ref[new_blk // LANES][:, new_blk % LANES:new_blk % LANES + 1].astype(F32)
    sn = jnp.where(col <= tok, _dot_nt(qbd, kn) + pen_col, NEG)
    m_old = m_ref[:, 0:1]
    m_new = jnp.maximum(m_old, jnp.max(sn, axis=1, keepdims=True))
    alpha = jnp.exp(m_old - m_new)
    pn = jnp.exp(sn - m_new)
    l_new = alpha * l_ref[:, 0:1] + jnp.sum(pn, axis=1, keepdims=True)
    o_slc = (alpha * acc_ref[...] + _dot(pn.astype(BF16), vn)) / l_new

    wlen = win_ref.shape[1]
    new_t = jnp.concatenate([win_new_ref[...], jnp.zeros((LANES - 8, KV_W), F32)], axis=0).T
    kw_t = jnp.concatenate([win_ref[0:K_W, :], new_t[0:K_W]], axis=1).astype(BF16)
    vw_t = jnp.concatenate([win_ref[K_W:, :], new_t[K_W:]], axis=1).astype(BF16)
    nkw = wlen + LANES
    idx = _iota((QROWS, nkw), 1)
    t_row = past_len + (_iota((QROWS, nkw), 0) & 7)
    kpos = past_len - wlen + idx
    dist = t_row - kpos
    wmask = jnp.logical_and(jnp.logical_and(dist >= 0, dist <= WINDOW),
                            jnp.logical_and(kpos >= 0, idx < wlen + 8))
    sw = jnp.where(wmask, _dot(qbd, kw_t), NEG)
    mw = jnp.max(sw, axis=1, keepdims=True)
    ew = jnp.where(wmask, jnp.exp(sw - mw), 0.0)
    pw = ew / jnp.maximum(jnp.sum(ew, axis=1, keepdims=True), 1e-30)
    o_win = _dot_nt(pw.astype(BF16), vw_t)

    o_cmp = oc_ref[...]
    gates = gates_ref[...]
    lo = _iota((8, LANES), 1) < HEAD_DIM
    for c in range(D_MODEL // LANES):
        g = c // 2
        halves = []
        for e in range(2):
            r = 2 * (c % 2) + e
            rows = slice(r * 32 + g * 8, r * 32 + g * 8 + 8)
            lanes = slice((g // 2) * LANES, (g // 2 + 1) * LANES)
            base = g * LANES + 3 * r
            blk = (gates[:, base:base + 1] * o_cmp[rows, lanes] + gates[:, base + 1:base + 2] * o_slc[rows, lanes]
                   + gates[:, base + 2:base + 3] * o_win[rows, lanes])
            if g % 2 != e:
                blk = pltpu.roll(blk, HEAD_DIM, 1)
            halves.append(blk)
        o_ref[:, c * LANES:(c + 1) * LANES] = jnp.where(lo, halves[0], halves[1]).astype(BF16)


def _finish_sample(m, l, acc, oc, qbd, pen, slc_new, win_new, win3, gates, past_len, new_blk):
    db = m.shape[0]
    per_b = lambda shape: pl.BlockSpec((None,) + shape, lambda b: (b,) + (0,) * len(shape))
    ins = (m, l, acc, oc, qbd, pen, slc_new, win_new, win3, gates)
    return pl.pallas_call(
        functools.partial(_finish_sample_kernel, past_len=past_len, new_blk=new_blk),
        grid=(db,),
        in_specs=[per_b(a.shape[1:]) for a in ins],
        out_specs=per_b((8, D_MODEL)),
        out_shape=jax.ShapeDtypeStruct((db, 8, D_MODEL), BF16),
        compiler_params=_params(("parallel",)),
        name="finish_sample",
    )(*ins)


def _cmp_weights(w_c):
    eye = jnp.eye(N_KV, dtype=F32)
    bd = jnp.einsum("gh,lde->lgdhe", eye, w_c).reshape(CMP_LEN, K_W, K_W)
    return jnp.concatenate([bd[:CMP_STRIDE], bd[CMP_STRIDE:]], axis=2).astype(BF16)


def _cmp_pair_weights(w_c):
    eye = jnp.eye(2, dtype=F32)
    bd = jnp.einsum("gh,lde->lgdhe", eye, w_c).reshape(CMP_LEN, LANES, LANES)
    fs = jnp.concatenate([bd[:CMP_STRIDE], bd[CMP_STRIDE:]], axis=2)
    return jnp.concatenate([fs[:CMP_STRIDE // 2], fs[CMP_STRIDE // 2:]], axis=1).astype(BF16)


def _pos_operands(pe, w_c):
    pe_rows = jnp.tile(pe.reshape(1, CMP_LEN * HEAD_DIM), (SUBLANES, 1)).astype(BF16)
    w4 = jnp.tile(w_c.reshape(CMP_LEN * HEAD_DIM, HEAD_DIM), (1, N_KV)).astype(BF16)
    return pe_rows, w4


def _tile_gain(g, width):
    return jnp.tile(g, width // g.shape[0]).reshape(1, width).astype(F32)


def kernel(x_prompt, x_sample, cache_cmp_kv, cache_slc_kv, cache_win_kv, state_conv, state_ffn_conv,
           page_table, g_norm_mix, w_in, w_dw, b_dw, g_ln_conv, b_ln_conv, w_conv_out, g_q, g_k_cmp,
           g_k_slc, g_k_win, w_cmp_k, w_cmp_v, pe_cmp_k, pe_cmp_v, w_nsa_out, w_out, g_norm_ffn, w_up,
           w_ffn_dw, b_ffn_dw, w_down):
    bp, seq, _ = x_prompt.shape
    db, dseq, _ = x_sample.shape
    n_pool, page = cache_cmp_kv.shape[0], cache_cmp_kv.shape[1]
    n_pages = page_table.shape[1]
    past_len = n_pages * page
    wlen = cache_win_kv.shape[1]
    assert dseq == 8 and seq % QBLK == 0 and seq // SEL_BLOCK <= LANES and QBLK == WINDOW
    assert past_len % SEL_BLOCK == 0 and wlen == WINDOW and past_len >= WINDOW
    assert n_pages % CMP_PAGES == 0 and n_pages % SLC_PAGES == 0 and page % CMP_STRIDE == 0
    assert LANES % (SLC_PAGES * page // SEL_BLOCK) == 0

    offs = np.cumsum((2 * C_CONV, N_HEADS * HEAD_DIM, 3 * KV_W, 3 * N_HEADS))
    wu = w_in[:, :offs[0]].astype(BF16)
    wq = w_in[:, offs[0]:offs[1]].astype(BF16)
    wkv = w_in[:, offs[1]:offs[2]].astype(BF16)
    wgn = jnp.pad(w_in[:, offs[2]:offs[3]].reshape(D_MODEL, N_KV, 3 * GROUP),
                  ((0, 0), (0, 0), (0, LANES - 3 * GROUP))).reshape(D_MODEL, N_KV * LANES).astype(BF16)
    wgm = w_in[:, offs[3]:].astype(BF16)
    bd16, ex16 = _head_sum_mats(D_MODEL)
    bd4, ex4 = _head_sum_mats(K_W)
    in_wts = (g_norm_mix.reshape(1, D_MODEL), wu, wq, wkv, wgn, wgm, _tile_gain(g_q, D_MODEL),
              _tile_gain(g_k_slc, K_W), _tile_gain(g_k_win, K_W), bd16, ex16, bd4, ex4)
    wdw = jnp.pad(w_dw, ((0, 1), (0, 0)))
    conv_wts = (wdw, b_dw.reshape(1, C_CONV), g_ln_conv.reshape(1, C_CONV), b_ln_conv.reshape(1, C_CONV),
                w_conv_out.astype(BF16))
    wno = w_nsa_out.astype(BF16)
    wout = w_out.astype(BF16)
    gffn = g_norm_ffn.reshape(1, D_MODEL)
    wup = w_up.reshape(D_MODEL, 2 * FFN_NC, FFN_CW).transpose(1, 0, 2).astype(BF16)
    wdn = w_down.reshape(FFN_NC, FFN_CW, D_MODEL).astype(BF16)
    wfd = jnp.concatenate([w_ffn_dw, b_ffn_dw[None, :], jnp.zeros((4, 2 * D_FF), F32)], axis=0)
    wfd = wfd.reshape(SUBLANES, 2 * FFN_NC, FFN_CW).transpose(1, 0, 2)
    bdk = _cmp_weights(w_cmp_k)
    bdv = _cmp_weights(w_cmp_v)
    pek, wk4 = _pos_operands(pe_cmp_k, w_cmp_k)
    pev, wv4 = _pos_operands(pe_cmp_v, w_cmp_v)
    gkc = _tile_gain(g_k_cmp, K_W)
    cw_tail = (pek, wk4, pev, wv4, gkc, bd4, ex4)

    cos_p, sin_p = _rope_tables(jnp.arange(seq))
    (glu, q, cmp_rows, slc_rows, win_rows, gn, gm, k2s, v2s, k2w, v2w, cmp_slabs) = _inproj(
        x_prompt, cos_p, sin_p, in_wts, True)
    zeros_halo = jnp.zeros((bp, CONV_HALO, C_CONV), F32)
    a_out, conv_tail = _conv_module(glu, glu, zeros_halo, *conv_wts, tt=256, rb=64)
    nch = seq // CMP_STRIDE
    n_cmp = nch - 1
    n_slc = seq // SEL_BLOCK
    cos_c, sin_c = _rope_tables(jnp.arange(nch) * CMP_STRIDE + (CMP_LEN - 1))
    kc2, vc2 = _cmp_prompt(cmp_slabs, (bdk, bdv) + cw_tail, cos_c, sin_c)
    e_tab = jnp.asarray((np.arange(seq)[:, None] // SEL_BLOCK == np.arange(LANES)[None, :]), BF16)
    ov = jnp.asarray(_overlap_t(LANES, nch, 0, n_cmp).T * (np.arange(LANES) < n_slc), BF16)
    gex_np = np.zeros((2 * LANES, 3 * GROUP * LANES), np.float32)
    for k in range(3 * GROUP):
        gex_np[k, k * LANES:(k + 1) * LANES] = 1.0
        gex_np[LANES + k, k * LANES:(k + 1) * LANES] = 1.0
    o_p = _attn_prompt(q, gn, kc2, vc2, k2s, v2s, k2w, v2w, e_tab, ov, jnp.asarray(gex_np, BF16),
                       n_cmp, n_slc, min(N_SEL, n_slc))
    n_tok = bp * seq
    x1 = _merge(o_p.reshape(n_tok, D_MODEL), a_out.reshape(n_tok, D_MODEL), gm.reshape(n_tok, 2 * D_MODEL),
                x_prompt.reshape(n_tok, D_MODEL), wno, wout, 512)
    zeros_fix = jnp.zeros((bp, SUBLANES, 2 * D_FF), F32)
    y_p, ffn_tail = _ffn(x1.reshape(bp, seq, D_MODEL), gffn, wup, wfd, wdn, zeros_fix, zeros_fix, 512, 512)

    kv5 = lambda rows, b, t: rows.reshape(b, t, 2, N_KV, HEAD_DIM)
    out_p = (y_p, kv5(cmp_rows, bp, seq), kv5(slc_rows, bp, seq),
             kv5(win_rows[:, seq - min(WINDOW, seq):], bp, min(WINDOW, seq)),
             conv_tail[:, CONV_HALO - (CONV_K - 1):], ffn_tail[:, :FFN_CONV_K - 1])

    n_s = db * dseq
    pos_s = past_len + (jnp.arange(TOK_TILE) % dseq)
    cos_s, sin_s = _rope_tables(pos_s)
    xs_pad = jnp.pad(x_sample.reshape(1, n_s, D_MODEL), ((0, 0), (0, TOK_TILE - n_s), (0, 0))) if n_s < TOK_TILE \
        else x_sample.reshape(1, n_s, D_MODEL)
    assert xs_pad.shape[1] == TOK_TILE
    (glu_s, q_s, cmp_s, slc_s, win_s, gn_s, gm_s) = [a[0, :n_s] for a in _inproj(xs_pad, cos_s, sin_s, in_wts, False)]
    st32 = jnp.pad(state_conv, ((0, 0), (CONV_HALO - (CONV_K - 1), 0), (0, 0)))
    a_s, conv_tail_s = _conv_module(glu_s.reshape(db, dseq, C_CONV), st32, st32, *conv_wts, tt=dseq, rb=dseq)

    pos_minor = lambda c: c.transpose(0, 2, 3, 4, 1).reshape(c.shape[0], KV_W, c.shape[1])
    cache_cmp_t = pos_minor(cache_cmp_kv)
    cache_slc_t = pos_minor(cache_slc_kv)
    kcs, vcs = _cmp_stream(page_table, cache_cmp_t, _cmp_pair_weights(w_cmp_k), _cmp_pair_weights(w_cmp_v))
    nch_s = past_len // CMP_STRIDE
    n_cmp_s = -(-(past_len + dseq) // CMP_STRIDE) - 1
    n_slc_s = -(-(past_len + dseq) // SEL_BLOCK)
    nsp = -(-n_slc_s // LANES) * LANES
    cos_cs, sin_cs = _rope_tables(jnp.arange(nch_s) * CMP_STRIDE + (CMP_STRIDE - 1))
    ovt_s = jnp.asarray(_overlap_t(nsp, nch_s, 1, n_cmp_s), BF16)
    qbd, oc_s, pen_s = _cmp_sample(kcs, vcs, q_s.reshape(db, dseq, D_MODEL), cw_tail, cos_cs, sin_cs, ovt_s,
                                   past_len, n_slc_s, min(N_SEL, n_slc_s))
    m_s, l_s, acc_s = _slc_stream(page_table, cache_slc_t, qbd, pen_s)
    win3 = cache_win_kv.reshape(db, wlen, KV_W)
    o_s = _finish_sample(m_s, l_s, acc_s, oc_s, qbd, pen_s, slc_s.reshape(db, dseq, KV_W),
                         win_s.reshape(db, dseq, KV_W), pos_minor(cache_win_kv),
                         gn_s.reshape(db, dseq, N_KV * LANES), past_len, past_len // SEL_BLOCK)
    x1_s = _merge(o_s.reshape(n_s, D_MODEL), a_s.reshape(n_s, D_MODEL), gm_s, x_sample.reshape(n_s, D_MODEL),
                  wno, wout, n_s)
    z1 = jnp.zeros((db, dseq - 1, 2 * D_FF), F32)
    fix1 = jnp.concatenate([state_ffn_conv[:, 1:2], z1], axis=1).reshape(1, n_s, 2 * D_FF)
    fix2 = jnp.concatenate([state_ffn_conv, z1[:, 1:]], axis=1).reshape(1, n_s, 2 * D_FF)
    y_s, up_s = _ffn(x1_s.reshape(1, n_s, D_MODEL), gffn, wup, wfd, wdn, fix1, fix2, n_s, dseq)

    win_all = jnp.concatenate([win3, win_s.reshape(db, dseq, KV_W)], axis=1)
    keep = min(WINDOW, past_len + dseq)
    out_s = (y_s.reshape(db, dseq, D_MODEL), kv5(cmp_s, db, dseq), kv5(slc_s, db, dseq),
             kv5(win_all[:, wlen + dseq - keep:], db, keep),
             conv_tail_s[:, CONV_HALO - (CONV_K - 1):],
             up_s.reshape(db, dseq, 2 * D_FF)[:, dseq - (FFN_CONV_K - 1):])
    return (out_p[0], out_s[0]) + out_p[1:] + out_s[1:]
```

```python
import functools

import numpy as np
import jax
import jax.numpy as jnp
from jax import lax
from jax.experimental import pallas as pl
from jax.experimental.pallas import tpu as pltpu

F32 = jnp.float32
BF16 = jnp.bfloat16

D_MODEL = 1024
N_HEADS = 16
HEAD_DIM = 64
N_KV = 4
GROUP = N_HEADS // N_KV
CMP_STRIDE = 16
CMP_LEN = 2 * CMP_STRIDE
SEL_BLOCK = 64
N_SEL = 16
WINDOW = 512
C_CONV = D_MODEL // 2
CONV_K = 31
FFN_CONV_K = 3
D_FF = 2816
ROPE_THETA = 10000.0
EPS = 1e-6
KV_W = 2 * N_KV * HEAD_DIM
K_W = N_KV * HEAD_DIM

LANES = 128
SUBLANES = 8
QBLK = WINDOW
TOK_TILE = 256
ATT_GROUPS = 2
CONV_HALO = 32
VMEM_LIMIT = 56 * 1024 * 1024
FFN_CW = 256
FFN_NC = D_FF // FFN_CW
CMP_PAGES = 64
CMP_GROUP = 8
SLC_PAGES = 64
QROWS = GROUP * N_KV * 8

NEG = -1e30
PEN = -1e30
RANK_FORCED = 1e30
RANK_INVALID = -1e30
RANK_REMOVED = -2e30


def _dot(a, b):
    return jnp.dot(a, b, preferred_element_type=F32)


def _dot_nt(a, b):
    return lax.dot_general(a, b, (((1,), (1,)), ((), ())), preferred_element_type=F32)


def _sigmoid(x):
    return 1.0 / (1.0 + jnp.exp(-x))


def _iota(shape, dim):
    return lax.broadcasted_iota(jnp.int32, shape, dim)


def _params(sem):
    return pltpu.CompilerParams(dimension_semantics=sem, vmem_limit_bytes=VMEM_LIMIT)


def _const_spec(a, ngrid):
    nd = a.ndim
    if ngrid == 1:
        return pl.BlockSpec(a.shape, lambda i: (0,) * nd)
    if ngrid == 2:
        return pl.BlockSpec(a.shape, lambda i, j: (0,) * nd)
    return pl.BlockSpec(a.shape, lambda i, j, k: (0,) * nd)


def _rope_tables(pos):
    half = HEAD_DIM // 2
    inv = ROPE_THETA ** (-jnp.arange(half, dtype=F32) * (2.0 / HEAD_DIM))
    ang = pos.astype(F32)[:, None] * inv[None, :]
    cos, sin = jnp.cos(ang), jnp.sin(ang)
    return jnp.tile(cos, (1, 4)), jnp.tile(jnp.concatenate([-sin, sin], axis=1), (1, 2))


def _head_sum_mats(width):
    heads = width // HEAD_DIM
    lane_head = np.arange(width) // HEAD_DIM
    bd = (lane_head[:, None] == np.arange(LANES)[None, :]).astype(np.float32)
    ex = np.zeros((2 * LANES, width), np.float32)
    ex[:heads] = (np.arange(heads)[:, None] == lane_head[None, :])
    ex[LANES:LANES + heads] = ex[:heads]
    return jnp.asarray(bd, BF16), jnp.asarray(ex, BF16)


def _overlap_t(n_slc_rows, n_cols, col_shift, n_cmp):
    n = np.arange(n_cols) - col_shift
    j = np.arange(n_slc_rows)
    cs = n * CMP_STRIDE
    ss = j * SEL_BLOCK
    ov = (cs[None, :] < ss[:, None] + SEL_BLOCK) & (cs[None, :] + CMP_LEN > ss[:, None])
    ov &= (n[None, :] >= 0) & (n[None, :] < n_cmp)
    return ov.astype(np.float32)


def _head_rmsnorm(x, gain, bd, ex):
    ssum = _dot((x * x).astype(BF16), bd)
    r = lax.rsqrt(ssum * (1.0 / HEAD_DIM) + EPS)
    r_hi = r.astype(BF16)
    r_lo = (r - r_hi.astype(F32)).astype(BF16)
    rfull = _dot(jnp.concatenate([r_hi, r_lo], axis=1), ex)
    return x * rfull * gain


def _rope128(y, cos, sin):
    lane = _iota(y.shape, 1)
    first = (lane & (HEAD_DIM - 1)) < (HEAD_DIM // 2)
    rot = jnp.where(first, pltpu.roll(y, LANES - HEAD_DIM // 2, 1), pltpu.roll(y, HEAD_DIM // 2, 1))
    return y * cos + rot * sin


def _dup_heads(chunk):
    lo = _iota(chunk.shape, 1) < HEAD_DIM
    swapped = pltpu.roll(chunk, HEAD_DIM, 1)
    return jnp.where(lo, chunk, swapped), jnp.where(lo, swapped, chunk)


def _rank_select(val, n_rounds):
    rows = val.shape[0]
    j = _iota(val.shape, 0).astype(F32)
    sel = jnp.zeros(val.shape, F32)
    for _ in range(n_rounds):
        mx = jnp.max(val, axis=0, keepdims=True)
        idx = jnp.min(jnp.where(val == mx, j, float(rows)), axis=0, keepdims=True)
        hit = j == idx
        sel = jnp.where(hit, 1.0, sel)
        val = jnp.where(hit, RANK_REMOVED, val)
    return sel > 0.5


def _inproj_kernel(x_ref, gmix_ref, wu_ref, wq_ref, wkv_ref, wgn_ref, wgm_ref, gq_ref, gks_ref,
                   gkw_ref, bd16_ref, ex16_ref, bd4_ref, ex4_ref, cos_ref, sin_ref,
                   glu_ref, q_ref, cmp_ref, slc_ref, win_ref, gn_ref, gm_ref, *attn_refs):
    x = x_ref[...]
    ms = jnp.mean(x * x, axis=-1, keepdims=True)
    h = (x * lax.rsqrt(ms + EPS) * gmix_ref[...]).astype(BF16)
    cos = cos_ref[...]
    sin = sin_ref[...]

    u = _dot(h, wu_ref[...])
    glu_ref[...] = u[:, :C_CONV] * _sigmoid(u[:, C_CONV:])

    yq = _head_rmsnorm(_dot(h, wq_ref[...]), gq_ref[...], bd16_ref[...], ex16_ref[...])
    scale = HEAD_DIM ** -0.5
    for c in range(D_MODEL // LANES):
        sl = slice(c * LANES, (c + 1) * LANES)
        q_ref[:, sl] = (_rope128(yq[:, sl], cos, sin) * scale).astype(BF16)

    zkv = _dot(h, wkv_ref[...])
    cmp_ref[...] = zkv[:, :KV_W]

    def kv_branch(z, gain_ref, rows_ref, k2_ref, v2_ref):
        yk = _head_rmsnorm(z[:, :K_W], gain_ref[...], bd4_ref[...], ex4_ref[...])
        v = z[:, K_W:]
        rows_ref[:, K_W:] = v
        for c in range(K_W // LANES):
            sl = slice(c * LANES, (c + 1) * LANES)
            kr = _rope128(yk[:, sl], cos, sin)
            rows_ref[:, sl] = kr
            if k2_ref is not None:
                for ref, chunk in ((k2_ref, kr), (v2_ref, v[:, sl])):
                    a2, b2 = _dup_heads(chunk)
                    ref[2 * c] = a2.astype(BF16)
                    ref[2 * c + 1] = b2.astype(BF16)

    if attn_refs:
        k2s_ref, v2s_ref, k2w_ref, v2w_ref, cslab_ref = attn_refs
        for c in range(KV_W // LANES):
            cslab_ref[c] = zkv[:, c * LANES:(c + 1) * LANES]
    else:
        k2s_ref = v2s_ref = k2w_ref = v2w_ref = None
    kv_branch(zkv[:, KV_W:2 * KV_W], gks_ref, slc_ref, k2s_ref, v2s_ref)
    kv_branch(zkv[:, 2 * KV_W:], gkw_ref, win_ref, k2w_ref, v2w_ref)

    gn_ref[...] = _sigmoid(_dot(h, wgn_ref[...]))
    gm_ref[...] = _sigmoid(_dot(h, wgm_ref[...]))


def _inproj(x, cos, sin, wts, attn_layouts):
    B, T, _ = x.shape
    tm = TOK_TILE
    assert T % tm == 0
    nt = T // tm
    tok = lambda w: pl.BlockSpec((None, tm, w), lambda b, t: (b, t, 0))
    tab = pl.BlockSpec((tm, LANES), lambda b, t: (t, 0))
    in_specs = [tok(D_MODEL)] + [_const_spec(a, 2) for a in wts] + [tab, tab]
    out_shape = [
        jax.ShapeDtypeStruct((B, T, C_CONV), F32),
        jax.ShapeDtypeStruct((B, T, D_MODEL), BF16),
        jax.ShapeDtypeStruct((B, T, KV_W), F32),
        jax.ShapeDtypeStruct((B, T, KV_W), F32),
        jax.ShapeDtypeStruct((B, T, KV_W), F32),
        jax.ShapeDtypeStruct((B, T, N_KV * LANES), F32),
        jax.ShapeDtypeStruct((B, T, 2 * D_MODEL), F32),
    ]
    out_specs = [tok(C_CONV), tok(D_MODEL), tok(KV_W), tok(KV_W), tok(KV_W), tok(N_KV * LANES),
                 tok(2 * D_MODEL)]
    if attn_layouts:
        dup = jax.ShapeDtypeStruct((B, N_KV, T, LANES), BF16)
        dup_spec = pl.BlockSpec((None, N_KV, tm, LANES), lambda b, t: (b, 0, t, 0))
        out_shape += [dup] * 4 + [jax.ShapeDtypeStruct((B, KV_W // LANES, T, LANES), F32)]
        out_specs += [dup_spec] * 5
    return pl.pallas_call(
        _inproj_kernel,
        grid=(B, nt),
        in_specs=in_specs,
        out_specs=out_specs,
        out_shape=out_shape,
        compiler_params=_params(("parallel", "parallel")),
        name="inproj",
    )(x, *wts, cos, sin)


def _conv_kernel(glu_ref, halo_ref, st_ref, wdw_ref, bdw_ref, gln_ref, bln_ref, wco_ref,
                 a_ref, cst_ref, xc_ref, s_ref, *, tt, rb):
    ti = pl.program_id(1)

    @pl.when(ti == 0)
    def _():
        xc_ref[0:CONV_HALO, :] = st_ref[...]

    @pl.when(ti > 0)
    def _():
        xc_ref[0:CONV_HALO, :] = halo_ref[...]

    xc_ref[CONV_HALO:CONV_HALO + tt, :] = glu_ref[...]
    xc_ref[CONV_HALO + tt:CONV_HALO + tt + SUBLANES, :] = jnp.zeros((SUBLANES, C_CONV), F32)
    first = CONV_HALO - (CONV_K - 1)
    for r0 in range(0, tt, rb):
        acc = jnp.broadcast_to(bdw_ref[...], (rb, C_CONV))
        for phi in range(SUBLANES):
            y = None
            for u in range(first, first + CONV_K):
                if u % SUBLANES == phi:
                    term = wdw_ref[u - first:u - first + 1, :] * xc_ref[r0 + u - phi:r0 + u - phi + rb + SUBLANES, :]
                    y = term if y is None else y + term
            acc = acc + y[phi:phi + rb]
        mu = jnp.mean(acc, axis=-1, keepdims=True)
        d = acc - mu
        var = jnp.mean(d * d, axis=-1, keepdims=True)
        y = d * lax.rsqrt(var + EPS) * gln_ref[...] + bln_ref[...]
        s_ref[r0:r0 + rb, :] = y * _sigmoid(y)
    a_ref[...] = _dot(s_ref[...].astype(BF16), wco_ref[...])
    cst_ref[...] = xc_ref[tt:tt + CONV_HALO, :]


def _conv_module(glu, halo_src, state32, wdw, bdw, gln, bln, wco, tt, rb):
    B, T, _ = glu.shape
    nt = T // tt
    hb = max(tt // CONV_HALO, 1)
    return pl.pallas_call(
        functools.partial(_conv_kernel, tt=tt, rb=rb),
        grid=(B, nt),
        in_specs=[
            pl.BlockSpec((None, tt, C_CONV), lambda b, t: (b, t, 0)),
            pl.BlockSpec((None, CONV_HALO, C_CONV), lambda b, t: (b, jnp.maximum(t * hb - 1, 0), 0)),
            pl.BlockSpec((None, CONV_HALO, C_CONV), lambda b, t: (b, 0, 0)),
            _const_spec(wdw, 2), _const_spec(bdw, 2), _const_spec(gln, 2), _const_spec(bln, 2),
            _const_spec(wco, 2),
        ],
        out_specs=[
            pl.BlockSpec((None, tt, D_MODEL), lambda b, t: (b, t, 0)),
            pl.BlockSpec((None, CONV_HALO, C_CONV), lambda b, t: (b, 0, 0)),
        ],
        out_shape=[
            jax.ShapeDtypeStruct((B, T, D_MODEL), F32),
            jax.ShapeDtypeStruct((B, CONV_HALO, C_CONV), F32),
        ],
        scratch_shapes=[pltpu.VMEM((CONV_HALO + tt + SUBLANES, C_CONV), F32), pltpu.VMEM((tt, C_CONV), F32)],
        compiler_params=_params(("parallel", "arbitrary")),
        name="conv_module",
    )(glu, halo_src, state32, wdw, bdw, gln, bln, wco)


def _merge_kernel(o_ref, a_ref, gm_ref, x_ref, wno_ref, wout_ref, x1_ref):
    b_out = _dot(o_ref[...], wno_ref[...])
    gm = gm_ref[...]
    m = gm[:, :D_MODEL] * a_ref[...] + gm[:, D_MODEL:] * b_out
    x1_ref[...] = x_ref[...] + _dot(m.astype(BF16), wout_ref[...])


def _merge(o, a_out, gm, x, wno, wout, tm):
    N = x.shape[0]
    row = lambda w: pl.BlockSpec((tm, w), lambda i: (i, 0))
    return pl.pallas_call(
        _merge_kernel,
        grid=(N // tm,),
        in_specs=[row(D_MODEL), row(D_MODEL), row(2 * D_MODEL), row(D_MODEL),
                  _const_spec(wno, 1), _const_spec(wout, 1)],
        out_specs=row(D_MODEL),
        out_shape=jax.ShapeDtypeStruct((N, D_MODEL), F32),
        compiler_params=_params(("parallel",)),
        name="merge",
    )(o, a_out, gm, x, wno, wout)


def _ffn_kernel(x1_ref, gn_ref, wup_ref, wdw_ref, wdn_ref, fix1_ref, fix2_ref,
                y_ref, st_ref, prev_ref, acc_ref, *, tm, seg):
    streaming = seg == tm
    x1 = x1_ref[...]
    ms = jnp.mean(x1 * x1, axis=-1, keepdims=True)
    h = (x1 * lax.rsqrt(ms + EPS) * gn_ref[...]).astype(BF16)
    row = _iota((tm, FFN_CW), 0) & (seg - 1)

    if streaming:
        @pl.when(pl.program_id(1) == 0)
        def _():
            prev_ref[...] = fix2_ref[...]

    def conv3(c):
        cs = slice(c * FFN_CW, (c + 1) * FFN_CW)
        up = _dot(h, wup_ref[c])
        r1 = pltpu.roll(up, 1, 0)
        r2 = pltpu.roll(up, 2, 0)
        if streaming:
            p0 = prev_ref[0:1, cs]
            p1 = prev_ref[1:2, cs]
            s1 = jnp.where(row == 0, p1, r1)
            s2 = jnp.where(row == 0, p0, jnp.where(row == 1, p1, r2))
            prev_ref[0:2, cs] = up[tm - 2:tm, :]
        else:
            s1 = jnp.where(row == 0, fix1_ref[:, cs], r1)
            s2 = jnp.where(row < 2, fix2_ref[:, cs], r2)
            st_ref[:, cs] = up
        w = wdw_ref[c]
        return w[0:1] * s2 + w[1:2] * s1 + w[2:3] * up + w[3:4]

    acc_ref[...] = x1
    for c in range(FFN_NC):
        gate = conv3(c)
        val = conv3(FFN_NC + c)
        act = gate * _sigmoid(gate) * val
        acc_ref[...] += _dot(act.astype(BF16), wdn_ref[c])
    y_ref[...] = acc_ref[...]
    if streaming:
        st_ref[...] = prev_ref[...]


def _ffn(x1, gn, wup, wdw, wdn, fix1, fix2, tm, seg):
    B, T, _ = x1.shape
    nt = T // tm
    streaming = seg == tm
    tok = lambda w: pl.BlockSpec((None, tm, w), lambda b, t: (b, t, 0))
    if streaming:
        fix_specs = [pl.BlockSpec((None, SUBLANES, 2 * D_FF), lambda b, t: (b, 0, 0))] * 2
        st_shape = jax.ShapeDtypeStruct((B, SUBLANES, 2 * D_FF), F32)
        st_spec = pl.BlockSpec((None, SUBLANES, 2 * D_FF), lambda b, t: (b, 0, 0))
    else:
        fix_specs = [tok(2 * D_FF)] * 2
        st_shape = jax.ShapeDtypeStruct((B, T, 2 * D_FF), F32)
        st_spec = tok(2 * D_FF)
    return pl.pallas_call(
        functools.partial(_ffn_kernel, tm=tm, seg=seg),
        grid=(B, nt),
        in_specs=[tok(D_MODEL), _const_spec(gn, 2), _const_spec(wup, 2), _const_spec(wdw, 2),
                  _const_spec(wdn, 2)] + fix_specs,
        out_specs=[tok(D_MODEL), st_spec],
        out_shape=[jax.ShapeDtypeStruct((B, T, D_MODEL), F32), st_shape],
        scratch_shapes=[pltpu.VMEM((SUBLANES, 2 * D_FF), F32), pltpu.VMEM((tm, D_MODEL), F32)],
        compiler_params=_params(("parallel", "arbitrary")),
        name="conv_ffn",
    )(x1, gn, wup, wdw, wdn, fix1, fix2)


def _cmp_prompt_kernel(rows_ref, bdk_ref, bdv_ref, pek_ref, wk4_ref, pev_ref, wv4_ref, gk_ref,
                       bd4_ref, ex4_ref, cos_ref, sin_ref, kc2_ref, vc2_ref, *, nch):
    acc_k = jnp.zeros((nch, 2 * K_W), F32)
    acc_v = jnp.zeros((nch, 2 * K_W), F32)
    for l in range(CMP_STRIDE):
        x = [rows_ref[c, pl.ds(l, nch, stride=CMP_STRIDE), :] for c in range(KV_W // LANES)]
        acc_k += _dot(jnp.concatenate(x[:2], axis=1).astype(BF16), bdk_ref[l])
        acc_v += _dot(jnp.concatenate(x[2:], axis=1).astype(BF16), bdv_ref[l])
    pos_k = _dot(pek_ref[...], wk4_ref[...])[0:1]
    pos_v = _dot(pev_ref[...], wv4_ref[...])[0:1]
    kc = acc_k[:, :K_W] + pltpu.roll(acc_k[:, K_W:], nch - 1, 0) + pos_k
    vc = acc_v[:, :K_W] + pltpu.roll(acc_v[:, K_W:], nch - 1, 0) + pos_v
    yk = _head_rmsnorm(kc, gk_ref[...], bd4_ref[...], ex4_ref[...])
    for c in range(K_W // LANES):
        sl = slice(c * LANES, (c + 1) * LANES)
        for ref, chunk in ((kc2_ref, _rope128(yk[:, sl], cos_ref[...], sin_ref[...])), (vc2_ref, vc[:, sl])):
            a2, b2 = _dup_heads(chunk)
            ref[2 * c] = a2.astype(BF16)
            ref[2 * c + 1] = b2.astype(BF16)


def _cmp_prompt(rows, cw, cos, sin):
    B, nslab, T, _ = rows.shape
    nch = T // CMP_STRIDE
    return pl.pallas_call(
        functools.partial(_cmp_prompt_kernel, nch=nch),
        grid=(B,),
        in_specs=[pl.BlockSpec((None, nslab, T, LANES), lambda b: (b, 0, 0, 0))] + [_const_spec(a, 1) for a in cw]
        + [_const_spec(cos, 1), _const_spec(sin, 1)],
        out_specs=[pl.BlockSpec((None, N_KV, nch, LANES), lambda b: (b, 0, 0, 0))] * 2,
        out_shape=[jax.ShapeDtypeStruct((B, N_KV, nch, LANES), BF16)] * 2,
        compiler_params=_params(("parallel",)),
        name="cmp_prompt",
    )(rows, *cw, cos, sin)


def _attn_prompt_kernel(q_ref, gates_ref, kc2_ref, vc2_ref, k2s_ref, v2s_ref, k2w_ref, v2w_ref,
                        e_ref, ov_ref, gex_ref, o_ref, m_ref, l_ref, acc_ref, mw_ref, lw_ref, accw_ref,
                        *, n_cmp, n_slc, n_top):
    qi = pl.program_id(2)
    qs = qi * QBLK
    ng, nch = kc2_ref.shape[0], kc2_ref.shape[1]
    lo = _iota((QBLK, LANES), 1) < HEAD_DIM
    zero = jnp.zeros((QBLK, LANES), BF16)
    last = jnp.minimum((qs + _iota((QBLK, nch), 0) - (CMP_LEN - 1)) >> 4, n_cmp - 1)
    cmask = _iota((QBLK, nch), 1) <= last

    def masked(x, fill):
        return jnp.concatenate([jnp.where(cmask, x[r * QBLK:(r + 1) * QBLK], fill) for r in range(GROUP)], axis=0)

    kl = _iota((QBLK, QBLK), 1)
    tl = _iota((QBLK, QBLK), 0)
    tri = {"causal": kl <= tl, "band": kl >= tl}
    slc_state = lambda gi: (m_ref.at[gi], l_ref.at[gi], acc_ref.at[gi])
    win_state = lambda gi: (mw_ref.at[gi], lw_ref.at[gi], accw_ref.at[gi])

    def reset(state):
        m_st, l_st, acc_st = state
        m_st[...] = jnp.full(m_st.shape, NEG, F32)
        l_st[...] = jnp.zeros(l_st.shape, F32)
        acc_st[...] = jnp.zeros(acc_st.shape, F32)

    def attend(state, kblk, v2blk, q_rows, mode):
        m_st, l_st, acc_st = state
        sc = _dot_nt(q_rows, kblk)
        if mode != "none":
            sc = jnp.concatenate([jnp.where(tri[mode], sc[r * QBLK:(r + 1) * QBLK], NEG)
                                  for r in range(GROUP)], axis=0)
        m_old = m_st[...]
        m_new = jnp.maximum(m_old, jnp.max(sc, axis=1, keepdims=True))
        alpha = jnp.exp(m_old - m_new)
        p = jnp.exp(sc - jnp.concatenate([m_new] * (sc.shape[1] // LANES), axis=1))
        l_st[...] = alpha * l_st[...] + jnp.sum(p, axis=1, keepdims=True)
        acc_st[...] = alpha * acc_st[...] + _dot(p.astype(BF16), v2blk)
        m_st[...] = m_new

    def blk(ref, kb):
        return ref[pl.ds(pl.multiple_of(kb * QBLK, QBLK), QBLK), :]

    def front(gi):
        qm = []
        for p in range(GROUP // 2):
            qp = q_ref[:, gi * K_W + p * LANES:gi * K_W + (p + 1) * LANES]
            qm += [jnp.where(lo, qp, zero), jnp.where(lo, zero, qp)]
        q_all = jnp.concatenate(qm, axis=0)

        s = masked(_dot_nt(q_all, kc2_ref[gi]), NEG)
        mx = jnp.max(s, axis=1, keepdims=True)
        e = masked(jnp.exp(s - mx), 0.0)
        p_cmp = (e * (1.0 / jnp.maximum(jnp.sum(e, axis=1, keepdims=True), 1e-30))).astype(BF16)
        o_cmp = _dot(p_cmp, vc2_ref[gi])
        imp4 = _dot(p_cmp, ov_ref[...])
        imp = imp4[0:QBLK]
        for r in range(1, GROUP):
            imp = imp + imp4[r * QBLK:(r + 1) * QBLK]

        reset(win_state(gi))
        attend(win_state(gi), blk(k2w_ref.at[gi], qi), blk(v2w_ref.at[gi], qi), q_all, "causal")

        imp_t = imp.T[0:n_slc]
        j = _iota((n_slc, QBLK), 0)
        tq = qs + _iota((n_slc, QBLK), 1)
        cur = tq >> 6
        forced = jnp.logical_or(j == 0, jnp.logical_or(j == cur, j == cur - 1))
        rank = jnp.where(forced, RANK_FORCED, jnp.where(j * SEL_BLOCK <= tq, imp_t, RANK_INVALID))
        pen_t = jnp.where(_rank_select(rank, n_top), 0.0, PEN)
        if n_slc < LANES:
            pen_t = jnp.concatenate([pen_t, jnp.zeros((LANES - n_slc, QBLK), F32)], axis=0)
        pen = pen_t.T.astype(BF16)
        qa_all = jnp.concatenate([q_all, jnp.concatenate([pen] * GROUP, axis=0)], axis=1)
        reset(slc_state(gi))
        return q_all, qa_all, o_cmp

    parts = [front(gi) for gi in range(ng)]

    def slc_step(kb, mode):
        e_blk = blk(e_ref, kb)
        for gi in range(ng):
            keys = jnp.concatenate([blk(k2s_ref.at[gi], kb), e_blk], axis=1)
            attend(slc_state(gi), keys, blk(v2s_ref.at[gi], kb), parts[gi][1], mode)

    def slc_body(kb, carry):
        slc_step(kb, "none")
        return carry

    lax.fori_loop(0, qi, slc_body, 0)
    slc_step(qi, "causal")

    @pl.when(qi >= 1)
    def _():
        for gi in range(ng):
            attend(win_state(gi), blk(k2w_ref.at[gi], qi - 1), blk(v2w_ref.at[gi], qi - 1), parts[gi][0], "band")

    for gi in range(ng):
        o_cmp = parts[gi][2]
        o_slc = acc_ref[gi] / l_ref[gi]
        o_win = accw_ref[gi] / lw_ref[gi]
        gt = gates_ref[:, gi * LANES:(gi + 1) * LANES]
        g_hi = gt.astype(BF16)
        g_lo = (gt - g_hi.astype(F32)).astype(BF16)
        gex = _dot(jnp.concatenate([g_hi, g_lo], axis=1), gex_ref[...])
        mixed = []
        for r in range(GROUP):
            rs = slice(r * QBLK, (r + 1) * QBLK)
            gl = lambda k: gex[:, (3 * r + k) * LANES:(3 * r + k + 1) * LANES]
            mixed.append(gl(0) * o_cmp[rs] + gl(1) * o_slc[rs] + gl(2) * o_win[rs])
        for p in range(GROUP // 2):
            o_ref[:, gi * K_W + p * LANES:gi * K_W + (p + 1) * LANES] = jnp.where(
                lo, mixed[2 * p], mixed[2 * p + 1]).astype(BF16)


def _attn_prompt(q, gates, kc2, vc2, k2s, v2s, k2w, v2w, e_tab, ov, gex, n_cmp, n_slc, n_top):
    B, T, _ = q.shape
    nq = T // QBLK
    nch = kc2.shape[2]
    rows = GROUP * QBLK
    ng = ATT_GROUPS
    per_bg = lambda n: pl.BlockSpec((None, ng, n, LANES), lambda b, g, i: (b, g, 0, 0))
    return pl.pallas_call(
        functools.partial(_attn_prompt_kernel, n_cmp=n_cmp, n_slc=n_slc, n_top=n_top),
        grid=(B, N_KV // ng, nq),
        in_specs=[
            pl.BlockSpec((None, QBLK, ng * K_W), lambda b, g, i: (b, i, g)),
            pl.BlockSpec((None, QBLK, ng * LANES), lambda b, g, i: (b, i, g)),
            per_bg(nch), per_bg(nch), per_bg(T), per_bg(T), per_bg(T), per_bg(T),
            _const_spec(e_tab, 3), _const_spec(ov, 3), _const_spec(gex, 3),
        ],
        out_specs=pl.BlockSpec((None, QBLK, ng * K_W), lambda b, g, i: (b, i, g)),
        out_shape=jax.ShapeDtypeStruct((B, T, D_MODEL), BF16),
        scratch_shapes=[pltpu.VMEM((ng, rows, LANES), F32)] * 6,
        compiler_params=_params(("parallel", "parallel", "arbitrary")),
        name="attn_prompt",
    )(q, gates, kc2, vc2, k2s, v2s, k2w, v2w, e_tab, ov, gex)


def _cmp_stream_kernel(pt_ref, *refs, m):
    pages = refs[:CMP_PAGES]
    bdk_ref, bdv_ref, kc_ref, vc_ref, fs_ref, slab_ref = refs[CMP_PAGES:]
    s = pl.program_id(1)
    per_page = m // CMP_PAGES
    nslab = KV_W // LANES

    @pl.when(s == 0)
    def _():
        fs_ref[0:SUBLANES, :] = jnp.zeros((SUBLANES, 2 * K_W), F32)

    half = K_W // 2

    def page_group(k0, k1):
        mg = (k1 - k0) * per_page
        for k in range(k0, k1):
            for c in range(nslab):
                slab_ref[k, c] = pages[k][c * LANES:(c + 1) * LANES, :].T

        def rows_of(c, l):
            return jnp.concatenate([slab_ref[k, c, pl.ds(l, per_page, stride=CMP_STRIDE), :]
                                    for k in range(k0, k1)], axis=0)

        def pair_rows(c, l):
            return jnp.concatenate([rows_of(c, l), rows_of(c, l + CMP_STRIDE // 2)], axis=1)

        acc_k = jnp.zeros((2 * mg, K_W), F32)
        acc_v = jnp.zeros((2 * mg, K_W), F32)
        for l in range(CMP_STRIDE // 2):
            xk = jnp.concatenate([pair_rows(0, l), pair_rows(1, l)], axis=0)
            xv = jnp.concatenate([pair_rows(2, l), pair_rows(3, l)], axis=0)
            acc_k += _dot(xk.astype(BF16), bdk_ref[l])
            acc_v += _dot(xv.astype(BF16), bdv_ref[l])
        return (jnp.concatenate([acc_k[:mg, :half], acc_k[mg:, :half], acc_v[:mg, :half], acc_v[mg:, :half]], axis=1),
                jnp.concatenate([acc_k[:mg, half:], acc_k[mg:, half:]], axis=1),
                jnp.concatenate([acc_v[:mg, half:], acc_v[mg:, half:]], axis=1))

    groups = [page_group(k0, k0 + CMP_GROUP) for k0 in range(0, CMP_PAGES, CMP_GROUP)]
    first, second_k, second_v = [jnp.concatenate([g[i] for g in groups], axis=0) for i in range(3)]
    fs_ref[SUBLANES:SUBLANES + m, :] = first
    shifted = fs_ref[SUBLANES - 1:SUBLANES - 1 + m, :]
    kc_ref[...] = shifted[:, :K_W] + second_k
    vc_ref[...] = shifted[:, K_W:] + second_v
    fs_ref[SUBLANES - 1:SUBLANES, :] = fs_ref[SUBLANES - 1 + m:SUBLANES + m, :]


def _cmp_stream(page_table, cache_t, bdk, bdv):
    db, n_pages = page_table.shape
    page = cache_t.shape[2]
    per_page = page // CMP_STRIDE
    m = CMP_PAGES * per_page
    nch = n_pages * per_page
    nsteps = n_pages // CMP_PAGES

    def page_spec(k):
        return pl.BlockSpec((None, KV_W, page), lambda b, s, pt: (pt[b, s * CMP_PAGES + k], 0, 0))

    grid_spec = pltpu.PrefetchScalarGridSpec(
        num_scalar_prefetch=1,
        grid=(db, nsteps),
        in_specs=[page_spec(k) for k in range(CMP_PAGES)]
        + [pl.BlockSpec(bdk.shape, lambda b, s, pt: (0, 0, 0)), pl.BlockSpec(bdv.shape, lambda b, s, pt: (0, 0, 0))],
        out_specs=[pl.BlockSpec((None, m, K_W), lambda b, s, pt: (b, s, 0))] * 2,
        scratch_shapes=[pltpu.VMEM((SUBLANES + m, 2 * K_W), F32),
                        pltpu.VMEM((CMP_PAGES, KV_W // LANES, page, LANES), F32)],
    )
    return pl.pallas_call(
        functools.partial(_cmp_stream_kernel, m=m),
        grid_spec=grid_spec,
        out_shape=[jax.ShapeDtypeStruct((db, nch, K_W), F32)] * 2,
        compiler_params=_params(("parallel", "arbitrary")),
        name="cmp_stream",
    )(page_table, *([cache_t] * CMP_PAGES), bdk, bdv)


def _block_diag_q(q):
    lo = _iota((8, LANES), 1) < HEAD_DIM
    zero = jnp.zeros((8, LANES), BF16)
    blocks = []
    for r in range(GROUP):
        for g in range(N_KV):
            hh = g * GROUP + r
            pair = q[:, (hh // 2) * LANES:(hh // 2 + 1) * LANES]
            if hh % 2 != g % 2:
                pair = pltpu.roll(pair.astype(F32), HEAD_DIM, 1).astype(BF16)
            keep = jnp.where(lo, pair, zero) if g % 2 == 0 else jnp.where(lo, zero, pair)
            blocks.append(jnp.concatenate([keep, zero] if g < 2 else [zero, keep], axis=1))
    return jnp.concatenate(blocks, axis=0)


def _cmp_sample_kernel(kcs_ref, vcs_ref, q_ref, pek_ref, wk4_ref, pev_ref, wv4_ref, gk_ref, bd4_ref,
                       ex4_ref, cos_ref, sin_ref, ovt_ref, qbd_ref, oc_ref, pen_ref,
                       *, past_len, n_slc, n_top):
    nch = kcs_ref.shape[0]
    pos_k = _dot(pek_ref[...], wk4_ref[...])[0:1]
    pos_v = _dot(pev_ref[...], wv4_ref[...])[0:1]
    yk = _head_rmsnorm(kcs_ref[...] + pos_k, gk_ref[...], bd4_ref[...], ex4_ref[...])
    kcn = jnp.concatenate([_rope128(yk[:, c * LANES:(c + 1) * LANES], cos_ref[...], sin_ref[...])
                           for c in range(K_W // LANES)], axis=1).astype(BF16)
    vc = (vcs_ref[...] + pos_v).astype(BF16)
    qbd = _block_diag_q(q_ref[...])
    qbd_ref[...] = qbd

    s = _dot_nt(qbd, kcn)
    mi = _iota((QROWS, nch), 1)
    t_row = past_len + (_iota((QROWS, nch), 0) & 7)
    mask = jnp.logical_and(mi * CMP_STRIDE + (CMP_STRIDE - 1) <= t_row, mi >= 1)
    s = jnp.where(mask, s, NEG)
    mx = jnp.max(s, axis=1, keepdims=True)
    e = jnp.where(mask, jnp.exp(s - mx), 0.0)
    p = (e / jnp.maximum(jnp.sum(e, axis=1, keepdims=True), 1e-30)).astype(BF16)
    oc_ref[...] = _dot(p, vc)

    imp = _dot_nt(ovt_ref[...], p)
    imp = imp + pltpu.roll(imp, 32, 1) + pltpu.roll(imp, 64, 1) + pltpu.roll(imp, 96, 1)
    nsp = imp.shape[0]
    j = _iota((nsp, QROWS), 0)
    t_lane = past_len + (_iota((nsp, QROWS), 1) & 7)
    cur = t_lane >> 6
    forced = jnp.logical_or(j == 0, jnp.logical_or(j == cur, j == cur - 1))
    rank = jnp.where(forced, RANK_FORCED, jnp.where(j * SEL_BLOCK <= t_lane, imp, RANK_INVALID))
    rank = jnp.where(j < n_slc, rank, RANK_REMOVED)
    sel = _rank_select(rank, n_top)
    pen_t = jnp.where(sel, 0.0, PEN)
    for blk in range(nsp // LANES):
        pen_ref[blk] = pen_t[blk * LANES:(blk + 1) * LANES, :].T.astype(BF16)


def _cmp_sample(kcs, vcs, q, cw_tail, cos, sin, ovt, past_len, n_slc, n_top):
    db, nch, _ = kcs.shape
    nsp = ovt.shape[0]
    per_b = lambda shape: pl.BlockSpec((None,) + shape, lambda b: (b,) + (0,) * len(shape))
    return pl.pallas_call(
        functools.partial(_cmp_sample_kernel, past_len=past_len, n_slc=n_slc, n_top=n_top),
        grid=(db,),
        in_specs=[per_b((nch, K_W)), per_b((nch, K_W)), per_b((8, D_MODEL))]
        + [_const_spec(a, 1) for a in cw_tail] + [_const_spec(cos, 1), _const_spec(sin, 1), _const_spec(ovt, 1)],
        out_specs=[per_b((QROWS, K_W)), per_b((QROWS, K_W)), per_b((nsp // LANES, QROWS, LANES))],
        out_shape=[jax.ShapeDtypeStruct((db, QROWS, K_W), BF16),
                   jax.ShapeDtypeStruct((db, QROWS, K_W), F32),
                   jax.ShapeDtypeStruct((db, nsp // LANES, QROWS, LANES), BF16)],
        compiler_params=_params(("parallel",)),
        name="cmp_sample",
    )(kcs, vcs, q, *cw_tail, cos, sin, ovt)


def _slc_stream_kernel(pt_ref, *refs):
    pages = refs[:SLC_PAGES]
    qbd_ref, pen_ref, m_out, l_out, acc_out, m_ref, l_ref, acc_ref = refs[SLC_PAGES:]
    s = pl.program_id(1)
    page = pages[0].shape[1]
    nk = SLC_PAGES * page
    blocks_per_step = nk // SEL_BLOCK

    @pl.when(s == 0)
    def _():
        m_ref[...] = jnp.full(m_ref.shape, NEG, F32)
        l_ref[...] = jnp.zeros(l_ref.shape, F32)
        acc_ref[...] = jnp.zeros(acc_ref.shape, F32)

    k_t = jnp.concatenate([pg[0:K_W, :] for pg in pages], axis=1).astype(BF16)
    v_t = jnp.concatenate([pg[K_W:, :] for pg in pages], axis=1).astype(BF16)
    steps_per_lane_block = LANES // blocks_per_step
    pen = pen_ref[s // steps_per_lane_block]
    jrow = _iota((LANES, nk), 0)
    blk = (s % steps_per_lane_block) * blocks_per_step + (_iota((LANES, nk), 1) >> 6)
    expand = jnp.where(jrow == blk, 1.0, 0.0).astype(BF16)
    sc = _dot(qbd_ref[...], k_t) + _dot(pen, expand)
    m_old = m_ref[...]
    m_new = jnp.maximum(m_old, jnp.max(sc, axis=1, keepdims=True))
    alpha = jnp.exp(m_old - m_new)
    p = jnp.exp(sc - m_new)
    l_ref[...] = alpha * l_ref[...] + jnp.sum(p, axis=1, keepdims=True)
    acc_ref[...] = alpha * acc_ref[...] + _dot_nt(p.astype(BF16), v_t)
    m_ref[...] = m_new

    @pl.when(s == pl.num_programs(1) - 1)
    def _():
        m_out[...] = jnp.broadcast_to(m_ref[...], m_out.shape)
        l_out[...] = jnp.broadcast_to(l_ref[...], l_out.shape)
        acc_out[...] = acc_ref[...]


def _slc_stream(page_table, cache_t, qbd, pen):
    db, n_pages = page_table.shape
    page = cache_t.shape[2]
    nsteps = n_pages // SLC_PAGES

    def page_spec(k):
        return pl.BlockSpec((None, KV_W, page), lambda b, s, pt: (pt[b, s * SLC_PAGES + k], 0, 0))

    per_b = lambda shape: pl.BlockSpec((None,) + shape, lambda b, s, pt: (b,) + (0,) * len(shape))
    grid_spec = pltpu.PrefetchScalarGridSpec(
        num_scalar_prefetch=1,
        grid=(db, nsteps),
        in_specs=[page_spec(k) for k in range(SLC_PAGES)] + [per_b((QROWS, K_W)), per_b(pen.shape[1:])],
        out_specs=[per_b((QROWS, LANES)), per_b((QROWS, LANES)), per_b((QROWS, K_W))],
        scratch_shapes=[pltpu.VMEM((QROWS, 1), F32), pltpu.VMEM((QROWS, 1), F32), pltpu.VMEM((QROWS, K_W), F32)],
    )
    return pl.pallas_call(
        _slc_stream_kernel,
        grid_spec=grid_spec,
        out_shape=[jax.ShapeDtypeStruct((db, QROWS, LANES), F32), jax.ShapeDtypeStruct((db, QROWS, LANES), F32),
                   jax.ShapeDtypeStruct((db, QROWS, K_W), F32)],
        compiler_params=_params(("parallel", "arbitrary")),
        name="slc_stream",
    )(page_table, *([cache_t] * SLC_PAGES), qbd, pen)


def _finish_sample_kernel(m_ref, l_ref, acc_ref, oc_ref, qbd_ref, pen_ref, slc_new_ref, win_new_ref,
                          win_ref, gates_ref, o_ref, *, past_len, new_blk):
    qbd = qbd_ref[...]
    tok = _iota((QROWS, LANES), 0) & 7
    col = _iota((QROWS, LANES), 1)
    pad = jnp.zeros((LANES - 8, K_W), F32)

    kn = jnp.concatenate([slc_new_ref[:, 0:K_W], pad], axis=0).astype(BF16)
    vn = jnp.concatenate([slc_new_ref[:, K_W:], pad], axis=0).astype(BF16)
    pen_col = pen_ref[new_blk // LANES][:, new_blk % LANES:new_blk % LANES + 1].astype(F32)
    sn = jnp.where(col <= tok, _dot_nt(qbd, kn) + pen_col, NEG)
    m_old = m_ref[:, 0:1]
    m_new = jnp.maximum(m_old, jnp.max(sn, axis=1, keepdims=True))
    alpha = jnp.exp(m_old - m_new)
    pn = jnp.exp(sn - m_new)
    l_new = alpha * l_ref[:, 0:1] + jnp.sum(pn, axis=1, keepdims=True)
    o_slc = (alpha * acc_ref[...] + _dot(pn.astype(BF16), vn)) / l_new

    wlen = win_ref.shape[1]
    new_t = jnp.concatenate([win_new_ref[...], jnp.zeros((LANES - 8, KV_W), F32)], axis=0).T
    kw_t = jnp.concatenate([win_ref[0:K_W, :], new_t[0:K_W]], axis=1).astype(BF16)
    vw_t = jnp.concatenate([win_ref[K_W:, :], new_t[K_W:]], axis=1).astype(BF16)
    nkw = wlen + LANES
    idx = _iota((QROWS, nkw), 1)
    t_row = past_len + (_iota((QROWS, nkw), 0) & 7)
    kpos = past_len - wlen + idx
    dist = t_row - kpos
    wmask = jnp.logical_and(jnp.logical_and(dist >= 0, dist <= WINDOW),
                            jnp.logical_and(kpos >= 0, idx < wlen + 8))
    sw = jnp.where(wmask, _dot(qbd, kw_t), NEG)
    mw = jnp.max(sw, axis=1, keepdims=True)
    ew = jnp.where(wmask, jnp.exp(sw - mw), 0.0)
    pw = ew / jnp.maximum(jnp.sum(ew, axis=1, keepdims=True), 1e-30)
    o_win = _dot_nt(pw.astype(BF16), vw_t)

    o_cmp = oc_ref[...]
    gates = gates_ref[...]
    lo = _iota((8, LANES), 1) < HEAD_DIM
    for c in range(D_MODEL // LANES):
        g = c // 2
        halves = []
        for e in range(2):
            r = 2 * (c % 2) + e
            rows = slice(r * 32 + g * 8, r * 32 + g * 8 + 8)
            lanes = slice((g // 2) * LANES, (g // 2 + 1) * LANES)
            base = g * LANES + 3 * r
            blk = (gates[:, base:base + 1] * o_cmp[rows, lanes] + gates[:, base + 1:base + 2] * o_slc[rows, lanes]
                   + gates[:, base + 2:base + 3] * o_win[rows, lanes])
            if g % 2 != e:
                blk = pltpu.roll(blk, HEAD_DIM, 1)
            halves.append(blk)
        o_ref[:, c * LANES:(c + 1) * LANES] = jnp.where(lo, halves[0], halves[1]).astype(BF16)


def _finish_sample(m, l, acc, oc, qbd, pen, slc_new, win_new, win3, gates, past_len, new_blk):
    db = m.shape[0]
    per_b = lambda shape: pl.BlockSpec((None,) + shape, lambda b: (b,) + (0,) * len(shape))
    ins = (m, l, acc, oc, qbd, pen, slc_new, win_new, win3, gates)
    return pl.pallas_call(
        functools.partial(_finish_sample_kernel, past_len=past_len, new_blk=new_blk),
        grid=(db,),
        in_specs=[per_b(a.shape[1:]) for a in ins],
        out_specs=per_b((8, D_MODEL)),
        out_shape=jax.ShapeDtypeStruct((db, 8, D_MODEL), BF16),
        compiler_params=_params(("parallel",)),
        name="finish_sample",
    )(*ins)


def _cmp_weights(w_c):
    eye = jnp.eye(N_KV, dtype=F32)
    bd = jnp.einsum("gh,lde->lgdhe", eye, w_c).reshape(CMP_LEN, K_W, K_W)
    return jnp.concatenate([bd[:CMP_STRIDE], bd[CMP_STRIDE:]], axis=2).astype(BF16)


def _cmp_pair_weights(w_c):
    eye = jnp.eye(2, dtype=F32)
    bd = jnp.einsum("gh,lde->lgdhe", eye, w_c).reshape(CMP_LEN, LANES, LANES)
    fs = jnp.concatenate([bd[:CMP_STRIDE], bd[CMP_STRIDE:]], axis=2)
    return jnp.concatenate([fs[:CMP_STRIDE // 2], fs[CMP_STRIDE // 2:]], axis=1).astype(BF16)


def _pos_operands(pe, w_c):
    pe_rows = jnp.tile(pe.reshape(1, CMP_LEN * HEAD_DIM), (SUBLANES, 1)).astype(BF16)
    w4 = jnp.tile(w_c.reshape(CMP_LEN * HEAD_DIM, HEAD_DIM), (1, N_KV)).astype(BF16)
    return pe_rows, w4


def _tile_gain(g, width):
    return jnp.tile(g, width // g.shape[0]).reshape(1, width).astype(F32)


def kernel(x_prompt, x_sample, cache_cmp_kv, cache_slc_kv, cache_win_kv, state_conv, state_ffn_conv,
           page_table, g_norm_mix, w_in, w_dw, b_dw, g_ln_conv, b_ln_conv, w_conv_out, g_q, g_k_cmp,
           g_k_slc, g_k_win, w_cmp_k, w_cmp_v, pe_cmp_k, pe_cmp_v, w_nsa_out, w_out, g_norm_ffn, w_up,
           w_ffn_dw, b_ffn_dw, w_down):
    bp, seq, _ = x_prompt.shape
    db, dseq, _ = x_sample.shape
    n_pool, page = cache_cmp_kv.shape[0], cache_cmp_kv.shape[1]
    n_pages = page_table.shape[1]
    past_len = n_pages * page
    wlen = cache_win_kv.shape[1]
    assert dseq == 8 and seq % QBLK == 0 and seq // SEL_BLOCK <= LANES and QBLK == WINDOW
    assert past_len % SEL_BLOCK == 0 and wlen == WINDOW and past_len >= WINDOW
    assert n_pages % CMP_PAGES == 0 and n_pages % SLC_PAGES == 0 and page % CMP_STRIDE == 0
    assert LANES % (SLC_PAGES * page // SEL_BLOCK) == 0

    offs = np.cumsum((2 * C_CONV, N_HEADS * HEAD_DIM, 3 * KV_W, 3 * N_HEADS))
    wu = w_in[:, :offs[0]].astype(BF16)
    wq = w_in[:, offs[0]:offs[1]].astype(BF16)
    wkv = w_in[:, offs[1]:offs[2]].astype(BF16)
    wgn = jnp.pad(w_in[:, offs[2]:offs[3]].reshape(D_MODEL, N_KV, 3 * GROUP),
                  ((0, 0), (0, 0), (0, LANES - 3 * GROUP))).reshape(D_MODEL, N_KV * LANES).astype(BF16)
    wgm = w_in[:, offs[3]:].astype(BF16)
    bd16, ex16 = _head_sum_mats(D_MODEL)
    bd4, ex4 = _head_sum_mats(K_W)
    in_wts = (g_norm_mix.reshape(1, D_MODEL), wu, wq, wkv, wgn, wgm, _tile_gain(g_q, D_MODEL),
              _tile_gain(g_k_slc, K_W), _tile_gain(g_k_win, K_W), bd16, ex16, bd4, ex4)
    wdw = jnp.pad(w_dw, ((0, 1), (0, 0)))
    conv_wts = (wdw, b_dw.reshape(1, C_CONV), g_ln_conv.reshape(1, C_CONV), b_ln_conv.reshape(1, C_CONV),
                w_conv_out.astype(BF16))
    wno = w_nsa_out.astype(BF16)
    wout = w_out.astype(BF16)
    gffn = g_norm_ffn.reshape(1, D_MODEL)
    wup = w_up.reshape(D_MODEL, 2 * FFN_NC, FFN_CW).transpose(1, 0, 2).astype(BF16)
    wdn = w_down.reshape(FFN_NC, FFN_CW, D_MODEL).astype(BF16)
    wfd = jnp.concatenate([w_ffn_dw, b_ffn_dw[None, :], jnp.zeros((4, 2 * D_FF), F32)], axis=0)
    wfd = wfd.reshape(SUBLANES, 2 * FFN_NC, FFN_CW).transpose(1, 0, 2)
    bdk = _cmp_weights(w_cmp_k)
    bdv = _cmp_weights(w_cmp_v)
    pek, wk4 = _pos_operands(pe_cmp_k, w_cmp_k)
    pev, wv4 = _pos_operands(pe_cmp_v, w_cmp_v)
    gkc = _tile_gain(g_k_cmp, K_W)
    cw_tail = (pek, wk4, pev, wv4, gkc, bd4, ex4)

    cos_p, sin_p = _rope_tables(jnp.arange(seq))
    (glu, q, cmp_rows, slc_rows, win_rows, gn, gm, k2s, v2s, k2w, v2w, cmp_slabs) = _inproj(
        x_prompt, cos_p, sin_p, in_wts, True)
    zeros_halo = jnp.zeros((bp, CONV_HALO, C_CONV), F32)
    a_out, conv_tail = _conv_module(glu, glu, zeros_halo, *conv_wts, tt=256, rb=64)
    nch = seq // CMP_STRIDE
    n_cmp = nch - 1
    n_slc = seq // SEL_BLOCK
    cos_c, sin_c = _rope_tables(jnp.arange(nch) * CMP_STRIDE + (CMP_LEN - 1))
    kc2, vc2 = _cmp_prompt(cmp_slabs, (bdk, bdv) + cw_tail, cos_c, sin_c)
    e_tab = jnp.asarray((np.arange(seq)[:, None] // SEL_BLOCK == np.arange(LANES)[None, :]), BF16)
    ov = jnp.asarray(_overlap_t(LANES, nch, 0, n_cmp).T * (np.arange(LANES) < n_slc), BF16)
    gex_np = np.zeros((2 * LANES, 3 * GROUP * LANES), np.float32)
    for k in range(3 * GROUP):
        gex_np[k, k * LANES:(k + 1) * LANES] = 1.0
        gex_np[LANES + k, k * LANES:(k + 1) * LANES] = 1.0
    o_p = _attn_prompt(q, gn, kc2, vc2, k2s, v2s, k2w, v2w, e_tab, ov, jnp.asarray(gex_np, BF16),
                       n_cmp, n_slc, min(N_SEL, n_slc))
    n_tok = bp * seq
    x1 = _merge(o_p.reshape(n_tok, D_MODEL), a_out.reshape(n_tok, D_MODEL), gm.reshape(n_tok, 2 * D_MODEL),
                x_prompt.reshape(n_tok, D_MODEL), wno, wout, 512)
    zeros_fix = jnp.zeros((bp, SUBLANES, 2 * D_FF), F32)
    y_p, ffn_tail = _ffn(x1.reshape(bp, seq, D_MODEL), gffn, wup, wfd, wdn, zeros_fix, zeros_fix, 512, 512)

    kv5 = lambda rows, b, t: rows.reshape(b, t, 2, N_KV, HEAD_DIM)
    out_p = (y_p, kv5(cmp_rows, bp, seq), kv5(slc_rows, bp, seq),
             kv5(win_rows[:, seq - min(WINDOW, seq):], bp, min(WINDOW, seq)),
             conv_tail[:, CONV_HALO - (CONV_K - 1):], ffn_tail[:, :FFN_CONV_K - 1])

    n_s = db * dseq
    pos_s = past_len + (jnp.arange(TOK_TILE) % dseq)
    cos_s, sin_s = _rope_tables(pos_s)
    xs_pad = jnp.pad(x_sample.reshape(1, n_s, D_MODEL), ((0, 0), (0, TOK_TILE - n_s), (0, 0))) if n_s < TOK_TILE \
        else x_sample.reshape(1, n_s, D_MODEL)
    assert xs_pad.shape[1] == TOK_TILE
    (glu_s, q_s, cmp_s, slc_s, win_s, gn_s, gm_s) = [a[0, :n_s] for a in _inproj(xs_pad, cos_s, sin_s, in_wts, False)]
    st32 = jnp.pad(state_conv, ((0, 0), (CONV_HALO - (CONV_K - 1), 0), (0, 0)))
    a_s, conv_tail_s = _conv_module(glu_s.reshape(db, dseq, C_CONV), st32, st32, *conv_wts, tt=dseq, rb=dseq)

    pos_minor = lambda c: c.transpose(0, 2, 3, 4, 1).reshape(c.shape[0], KV_W, c.shape[1])
    cache_cmp_t = pos_minor(cache_cmp_kv)
    cache_slc_t = pos_minor(cache_slc_kv)
    kcs, vcs = _cmp_stream(page_table, cache_cmp_t, _cmp_pair_weights(w_cmp_k), _cmp_pair_weights(w_cmp_v))
    nch_s = past_len // CMP_STRIDE
    n_cmp_s = -(-(past_len + dseq) // CMP_STRIDE) - 1
    n_slc_s = -(-(past_len + dseq) // SEL_BLOCK)
    nsp = -(-n_slc_s // LANES) * LANES
    cos_cs, sin_cs = _rope_tables(jnp.arange(nch_s) * CMP_STRIDE + (CMP_STRIDE - 1))
    ovt_s = jnp.asarray(_overlap_t(nsp, nch_s, 1, n_cmp_s), BF16)
    qbd, oc_s, pen_s = _cmp_sample(kcs, vcs, q_s.reshape(db, dseq, D_MODEL), cw_tail, cos_cs, sin_cs, ovt_s,
                                   past_len, n_slc_s, min(N_SEL, n_slc_s))
    m_s, l_s, acc_s = _slc_stream(page_table, cache_slc_t, qbd, pen_s)
    win3 = cache_win_kv.reshape(db, wlen, KV_W)
    o_s = _finish_sample(m_s, l_s, acc_s, oc_s, qbd, pen_s, slc_s.reshape(db, dseq, KV_W),
                         win_s.reshape(db, dseq, KV_W), pos_minor(cache_win_kv),
                         gn_s.reshape(db, dseq, N_KV * LANES), past_len, past_len // SEL_BLOCK)
    x1_s = _merge(o_s.reshape(n_s, D_MODEL), a_s.reshape(n_s, D_MODEL), gm_s, x_sample.reshape(n_s, D_MODEL),
                  wno, wout, n_s)
    z1 = jnp.zeros((db, dseq - 1, 2 * D_FF), F32)
    fix1 = jnp.concatenate([state_ffn_conv[:, 1:2], z1], axis=1).reshape(1, n_s, 2 * D_FF)
    fix2 = jnp.concatenate([state_ffn_conv, z1[:, 1:]], axis=1).reshape(1, n_s, 2 * D_FF)
    y_s, up_s = _ffn(x1_s.reshape(1, n_s, D_MODEL), gffn, wup, wfd, wdn, fix1, fix2, n_s, dseq)

    win_all = jnp.concatenate([win3, win_s.reshape(db, dseq, KV_W)], axis=1)
    keep = min(WINDOW, past_len + dseq)
    out_s = (y_s.reshape(db, dseq, D_MODEL), kv5(cmp_s, db, dseq), kv5(slc_s, db, dseq),
             kv5(win_all[:, wlen + dseq - keep:], db, keep),
             conv_tail_s[:, CONV_HALO - (CONV_K - 1):],
             up_s.reshape(db, dseq, 2 * D_FF)[:, dseq - (FFN_CONV_K - 1):])
    return (out_p[0], out_s[0]) + out_p[1:] + out_s[1:]
```
